```python
import math
import jax, jax.numpy as jnp
from jax import lax
import numpy as np

D_MODEL = 1024
BATCH = 16
SEQ = 4096
DEPTH = 2
DEC_BATCH = 16
DEC_SEQ = 16
PAST_LEN = 4096

CHUNK = 64
SSD_D_INNER = 2 * D_MODEL
SSD_HEAD_DIM = 64
SSD_N_HEADS = SSD_D_INNER // SSD_HEAD_DIM
SSD_N_GROUPS = 8
SSD_D_STATE = 128
SSD_CONV = 4
SSD_CONV_DIM = SSD_D_INNER + 2 * SSD_N_GROUPS * SSD_D_STATE
ATT_HEAD_DIM = 64
ATT_HEADS = D_MODEL // ATT_HEAD_DIM
ATT_KV_HEADS = ATT_HEADS // 4
ATT_GROUP = ATT_HEADS // ATT_KV_HEADS
WINDOW = 128
N_BUCKETS = 32
MAX_DISTANCE = 128
SC_DIM = D_MODEL
SC_WIDTH = 3
N_EXPERTS = 32
TOP_K = 4
D_FF = D_MODEL
SWIGLU_LIMIT = 7.0
SWIGLU_ALPHA = 1.702
MOE_BLOCK = 128
LN_EPS = 1e-5
ALPHA = (2.0 * DEPTH) ** 0.25
BETA = (8.0 * DEPTH) ** -0.25
IN_SIZES = (SSD_D_INNER, SSD_CONV_DIM, SSD_N_HEADS,
            ATT_HEADS * ATT_HEAD_DIM, ATT_KV_HEADS * ATT_HEAD_DIM, ATT_KV_HEADS * ATT_HEAD_DIM,
            SC_DIM, SC_DIM, SC_DIM,
            D_MODEL, D_MODEL, D_MODEL)
D_IN = sum(IN_SIZES)

kernel_name = 'hybrid_ssd_swa_shortconv_moe_stream_step'


def _split_points():
    return [int(v) for v in np.cumsum(IN_SIZES)[:-1]]


def layer_norm(x, g, b):
    xf = x.astype(jnp.float32)
    mu = jnp.mean(xf, -1, keepdims=True)
    var = jnp.mean(jnp.square(xf - mu), -1, keepdims=True)
    return ((xf - mu) * lax.rsqrt(var + LN_EPS) * g.astype(jnp.float32) + b.astype(jnp.float32)).astype(x.dtype)


def group_rmsnorm(y, w):
    b, s, d = y.shape
    yf = y.astype(jnp.float32).reshape(b, s, SSD_N_GROUPS, d // SSD_N_GROUPS)
    yf = yf * lax.rsqrt(jnp.mean(yf * yf, -1, keepdims=True) + LN_EPS)
    return (yf.reshape(b, s, d) * w.astype(jnp.float32)).astype(y.dtype)


def causal_conv(u, hist, w):
    width = w.shape[0]
    s = u.shape[1]
    up = jnp.concatenate([hist, u], axis=1)
    y = up[:, 0:s] * w[0]
    for j in range(1, width):
        y = y + up[:, j:j + s] * w[j]
    return y, up[:, s:]


def ssd_scan(xdt, adt, bm, cm, init, chunk):
    b, s, h, p = xdt.shape
    g, n = SSD_N_GROUPS, SSD_D_STATE
    r = h // g
    nc = s // chunk
    f32 = jnp.float32
    x = xdt.astype(f32).reshape(b, nc, chunk, g, r, p)
    a = adt.astype(f32).reshape(b, nc, chunk, g, r)
    bc = bm.astype(f32).reshape(b, nc, chunk, g, n)
    cc = cm.astype(f32).reshape(b, nc, chunk, g, n)
    a_cs = jnp.cumsum(a, axis=2)
    seg = a_cs[:, :, :, None] - a_cs[:, :, None, :]
    causal = jnp.tril(jnp.ones((chunk, chunk), dtype=bool))[:, :, None, None]
    decay = jnp.exp(jnp.where(causal, seg, -jnp.inf))
    cb = jnp.einsum('bclgn,bcsgn->bclsg', cc, bc)
    y_diag = jnp.einsum('bclsg,bclsgr,bcsgrp->bclgrp', cb, decay, x)
    decay_to_end = jnp.exp(a_cs[:, :, -1:] - a_cs)
    chunk_states = jnp.einsum('bclgn,bclgr,bclgrp->bcgrpn', bc, decay_to_end, x)
    chunk_decay = jnp.exp(a_cs[:, :, -1])

    def step(carry, inp):
        st, dec = inp
        return carry * dec[..., None, None] + st, carry

    init_f = init.astype(f32).reshape(b, g, r, p, n)
    final, prev = lax.scan(step, init_f, (jnp.moveaxis(chunk_states, 1, 0), jnp.moveaxis(chunk_decay, 1, 0)))
    prev = jnp.moveaxis(prev, 0, 1)
    y_off = jnp.einsum('bclgn,bcgrpn,bclgr->bclgrp', cc, prev, jnp.exp(a_cs))
    y = (y_diag + y_off).reshape(b, s, h, p)
    return y, final.reshape(b, h, p, n)


def ssd_branch(z, xbc, dt_raw, hist, init_state, conv_w, conv_b, dt_bias, a_log, d_skip, norm_w, w_out, chunk):
    b, s, _ = z.shape
    xbc_c, new_hist = causal_conv(xbc, hist, conv_w)
    xbc_c = jax.nn.silu(xbc_c + conv_b)
    gn = SSD_N_GROUPS * SSD_D_STATE
    xs = xbc_c[..., :SSD_D_INNER].reshape(b, s, SSD_N_HEADS, SSD_HEAD_DIM)
    bm = xbc_c[..., SSD_D_INNER:SSD_D_INNER + gn].reshape(b, s, SSD_N_GROUPS, SSD_D_STATE)
    cm = xbc_c[..., SSD_D_INNER + gn:].reshape(b, s, SSD_N_GROUPS, SSD_D_STATE)
    dt = jax.nn.softplus(dt_raw.astype(jnp.float32) + dt_bias.astype(jnp.float32))
    a = -jnp.exp(a_log.astype(jnp.float32))
    y, final = ssd_scan(xs.astype(jnp.float32) * dt[..., None], dt * a, bm, cm, init_state, chunk)
    y = y + d_skip.astype(jnp.float32)[:, None] * xs.astype(jnp.float32)
    y = y.astype(z.dtype).reshape(b, s, SSD_D_INNER) * jax.nn.silu(z)
    y = group_rmsnorm(y, norm_w)
    return y @ w_out, new_hist, final.astype(z.dtype)


def t5_bucket(rel):
    half = N_BUCKETS // 2
    max_exact = half // 2
    ret = jnp.where(rel > 0, half, 0)
    n = jnp.abs(rel)
    nf = jnp.maximum(n, 1).astype(jnp.float32)
    large = max_exact + (jnp.log(nf / max_exact) / math.log(MAX_DISTANCE / max_exact) * (half - max_exact)).astype(jnp.int32)
    large = jnp.minimum(large, half - 1)
    return ret + jnp.where(n < max_exact, n, large)


def rel_bias_heads(rel, table):
    bias = table[t5_bucket(rel)].astype(jnp.float32)
    q, k = rel.shape
    return jnp.transpose(bias, (2, 0, 1)).reshape(ATT_KV_HEADS, ATT_GROUP, q, k)


def sink_softmax(scores, sink):
    sink = sink.astype(jnp.float32)
    m = jnp.maximum(jnp.max(scores, -1, keepdims=True), sink)
    e = jnp.exp(scores - m)
    return e / (jnp.sum(e, -1, keepdims=True) + jnp.exp(sink - m))


def swa_prompt(q, k, v, sinks, table):
    b, s = q.shape[:2]
    nc = s // CHUNK
    nprev = WINDOW // CHUNK
    kw = WINDOW + CHUNK
    qb = q.reshape(b, nc, CHUNK, ATT_KV_HEADS, ATT_GROUP, ATT_HEAD_DIM)
    pad = jnp.zeros((b, WINDOW, ATT_KV_HEADS, ATT_HEAD_DIM), k.dtype)
    kp = jnp.concatenate([pad, k], 1).reshape(b, nc + nprev, CHUNK, ATT_KV_HEADS, ATT_HEAD_DIM)
    vp = jnp.concatenate([pad, v], 1).reshape(b, nc + nprev, CHUNK, ATT_KV_HEADS, ATT_HEAD_DIM)
    kb = jnp.concatenate([kp[:, j:j + nc] for j in range(nprev + 1)], axis=2)
    vb = jnp.concatenate([vp[:, j:j + nc] for j in range(nprev + 1)], axis=2)
    qoff = jnp.arange(CHUNK)
    koff = jnp.arange(kw) - WINDOW
    bias = rel_bias_heads(koff[None, :] - qoff[:, None], table)
    valid = (jnp.arange(nc)[:, None] * CHUNK + koff[None, :]) >= 0
    scores = jnp.einsum('bcqhgd,bckhd->bchgqk', qb, kb).astype(jnp.float32) * (ATT_HEAD_DIM ** -0.5) + bias
    scores = jnp.where(valid[None, :, None, None, None, :], scores, -1e30)
    probs = sink_softmax(scores, sinks.reshape(ATT_KV_HEADS, ATT_GROUP, 1, 1))
    out = jnp.einsum('bchgqk,bckhd->bcqhgd', probs.astype(v.dtype), vb)
    return out.reshape(b, s, ATT_HEADS * ATT_HEAD_DIM)


def swa_sample(q, k_new, v_new, cache_k, cache_v, sinks, table):
    b, sd = q.shape[:2]
    kall = jnp.concatenate([cache_k, k_new], 1)
    vall = jnp.concatenate([cache_v, v_new], 1)
    qoff = jnp.arange(sd)
    koff = jnp.concatenate([jnp.arange(WINDOW) - WINDOW, jnp.arange(sd)])
    bias = rel_bias_heads(koff[None, :] - qoff[:, None], table)
    qg = q.reshape(b, sd, ATT_KV_HEADS, ATT_GROUP, ATT_HEAD_DIM)
    scores = jnp.einsum('bqhgd,bkhd->bhgqk', qg, kall).astype(jnp.float32) * (ATT_HEAD_DIM ** -0.5) + bias
    probs = sink_softmax(scores, sinks.reshape(ATT_KV_HEADS, ATT_GROUP, 1, 1))
    out = jnp.einsum('bhgqk,bkhd->bqhgd', probs.astype(vall.dtype), vall)
    return out.reshape(b, sd, ATT_HEADS * ATT_HEAD_DIM)


def clamped_swiglu(h):
    gate = jnp.minimum(h[..., :D_FF], SWIGLU_LIMIT)
    up = jnp.clip(h[..., D_FF:], -SWIGLU_LIMIT, SWIGLU_LIMIT)
    return (up + 1.0) * gate * jax.nn.sigmoid(SWIGLU_ALPHA * gate)


def moe(x, router_w, router_b, w_up, b_up, w_down, b_down):
    b, s, d = x.shape
    t = b * s
    xt = x.reshape(t, d)
    logits = (xt @ router_w).astype(jnp.float32) + router_b.astype(jnp.float32)
    top_v, top_i = lax.top_k(logits, TOP_K)
    gates = jax.nn.softmax(top_v, axis=-1).astype(x.dtype)
    n_assign = t * TOP_K
    flat_e = top_i.reshape(-1)
    order = jnp.argsort(flat_e)
    sorted_e = flat_e[order]
    sorted_tok = (order // TOP_K).astype(jnp.int32)
    sorted_gate = gates.reshape(-1)[order]
    counts = jnp.bincount(flat_e, length=N_EXPERTS)
    padded = (counts + MOE_BLOCK - 1) // MOE_BLOCK * MOE_BLOCK
    start = jnp.cumsum(counts) - counts
    pend = jnp.cumsum(padded)
    pstart = pend - padded
    dest = pstart[sorted_e] + jnp.arange(n_assign) - start[sorted_e]
    n_blocks = (n_assign + N_EXPERTS * (MOE_BLOCK - 1) + MOE_BLOCK - 1) // MOE_BLOCK
    rows = n_blocks * MOE_BLOCK
    row_tok = jnp.full((rows,), t, jnp.int32).at[dest].set(sorted_tok)
    row_gate = jnp.zeros((rows,), x.dtype).at[dest].set(sorted_gate)
    block_e = jnp.minimum(jnp.searchsorted(pend, jnp.arange(n_blocks) * MOE_BLOCK, side='right'), N_EXPERTS - 1)
    xpad = jnp.concatenate([xt, jnp.zeros((1, d), xt.dtype)], 0)

    def body(y, blk):
        tok, g, e = blk
        h = xpad[tok] @ w_up[e] + b_up[e]
        out = clamped_swiglu(h) @ w_down[e] + b_down[e]
        return y.at[tok].add(out * g[:, None]), None

    y0 = jnp.zeros((t + 1, d), x.dtype)
    y, _ = lax.scan(body, y0, (row_tok.reshape(n_blocks, MOE_BLOCK), row_gate.reshape(n_blocks, MOE_BLOCK), block_e))
    return y[:t].reshape(b, s, d)


def trunk_layer(x, ssd_hist, ssd_state, sc_hist, kv_cache, rel_bias,
                w_in, ssd_conv_w, ssd_conv_b, dt_bias, a_log, d_skip, ssd_norm_w, w_ssd_out,
                sinks, w_att_out, sc_w, w_sc_out, w_o, ln1_g, ln1_b,
                router_w, router_b, w_up, b_up, w_down, b_down, ln2_g, ln2_b):
    b, s, _ = x.shape
    h = x @ w_in
    (z, xbc, dt_raw, q, k, v, sc_b, sc_c, sc_h, g_ssd, g_att, g_sc) = jnp.split(h, _split_points(), axis=-1)
    k = k.reshape(b, s, ATT_KV_HEADS, ATT_HEAD_DIM)
    v = v.reshape(b, s, ATT_KV_HEADS, ATT_HEAD_DIM)
    if kv_cache is None:
        att = swa_prompt(q, k, v, sinks, rel_bias)
        new_k, new_v = k[:, -WINDOW:], v[:, -WINDOW:]
        chunk = CHUNK
    else:
        att = swa_sample(q, k, v, kv_cache[0], kv_cache[1], sinks, rel_bias)
        new_k, new_v = k, v
        chunk = s
    y_ssd, new_ssd_hist, new_ssd_state = ssd_branch(z, xbc, dt_raw, ssd_hist, ssd_state, ssd_conv_w, ssd_conv_b,
                                                    dt_bias, a_log, d_skip, ssd_norm_w, w_ssd_out, chunk)
    sc_conv, new_sc_hist = causal_conv(sc_c * sc_h, sc_hist, sc_w)
    y_sc = (sc_b * sc_conv) @ w_sc_out
    y_att = att @ w_att_out
    merged = jax.nn.sigmoid(g_ssd) * y_ssd + jax.nn.sigmoid(g_att) * y_att + jax.nn.sigmoid(g_sc) * y_sc
    x = layer_norm(ALPHA * x + merged @ w_o, ln1_g, ln1_b)
    x = layer_norm(ALPHA * x + moe(x, router_w, router_b, w_up, b_up, w_down, b_down), ln2_g, ln2_b)
    return x, new_k, new_v, new_ssd_state, new_ssd_hist, new_sc_hist


def setup_inputs(seed: int = 0) -> dict:
    key = jax.random.key(seed)
    ks = iter(jax.random.split(key, 40))
    f32 = jnp.float32

    def nrm(shape, scale):
        return jax.random.normal(next(ks), shape, f32) * scale

    v_start = SSD_D_INNER + SSD_CONV_DIM + SSD_N_HEADS + ATT_HEADS * ATT_HEAD_DIM + ATT_KV_HEADS * ATT_HEAD_DIM
    v_size = ATT_KV_HEADS * ATT_HEAD_DIM
    col_scale = jnp.concatenate([jnp.ones((v_start,), f32), jnp.full((v_size,), BETA, f32),
                                 jnp.ones((D_IN - v_start - v_size,), f32)])
    dt0 = jnp.exp(jax.random.uniform(next(ks), (DEPTH, SSD_N_HEADS), f32) * (math.log(0.1) - math.log(0.001)) + math.log(0.001))
    return {
        'x_prompt': nrm((BATCH, SEQ, D_MODEL), 1.0),
        'x_sample': nrm((DEC_BATCH, DEC_SEQ, D_MODEL), 1.0),
        'cache_attn_k': nrm((DEPTH, DEC_BATCH, WINDOW, ATT_KV_HEADS, ATT_HEAD_DIM), 1.0),
        'cache_attn_v': nrm((DEPTH, DEC_BATCH, WINDOW, ATT_KV_HEADS, ATT_HEAD_DIM), BETA),
        'state_ssd': nrm((DEPTH, DEC_BATCH, SSD_N_HEADS, SSD_HEAD_DIM, SSD_D_STATE), 0.1),
        'state_ssd_conv': nrm((DEPTH, DEC_BATCH, SSD_CONV - 1, SSD_CONV_DIM), 1.0),
        'state_short_conv': nrm((DEPTH, DEC_BATCH, SC_WIDTH - 1, SC_DIM), 1.0),
        'w_in': nrm((DEPTH, D_MODEL, D_IN), D_MODEL ** -0.5) * col_scale,
        'ssd_conv_w': nrm((DEPTH, SSD_CONV, SSD_CONV_DIM), SSD_CONV ** -0.5),
        'ssd_conv_b': nrm((DEPTH, SSD_CONV_DIM), 0.02),
        'ssd_dt_bias': dt0 + jnp.log(-jnp.expm1(-dt0)),
        'ssd_a_log': jnp.log(jax.random.uniform(next(ks), (DEPTH, SSD_N_HEADS), f32, 1.0, 16.0)),
        'ssd_d': 1.0 + nrm((DEPTH, SSD_N_HEADS), 0.1),
        'ssd_norm_w': 1.0 + nrm((DEPTH, SSD_D_INNER), 0.02),
        'w_ssd_out': nrm((DEPTH, SSD_D_INNER, D_MODEL), BETA * SSD_D_INNER ** -0.5),
        'attn_sinks': nrm((DEPTH, ATT_HEADS), 0.5),
        'w_attn_out': nrm((DEPTH, ATT_HEADS * ATT_HEAD_DIM, D_MODEL), BETA * (ATT_HEADS * ATT_HEAD_DIM) ** -0.5),
        'rel_bias': nrm((N_BUCKETS, ATT_HEADS), 0.5),
        'sc_conv_w': nrm((DEPTH, SC_WIDTH, SC_DIM), SC_WIDTH ** -0.5),
        'w_sc_out': nrm((DEPTH, SC_DIM, D_MODEL), BETA * SC_DIM ** -0.5),
        'w_o': nrm((DEPTH, D_MODEL, D_MODEL), BETA * D_MODEL ** -0.5),
        'ln1_g': 1.0 + nrm((DEPTH, D_MODEL), 0.02),
        'ln1_b': nrm((DEPTH, D_MODEL), 0.02),
        'router_w': nrm((DEPTH, D_MODEL, N_EXPERTS), D_MODEL ** -0.5),
        'router_b': nrm((DEPTH, N_EXPERTS), 0.01),
        'w_up': nrm((DEPTH, N_EXPERTS, D_MODEL, 2 * D_FF), BETA * D_MODEL ** -0.5),
        'b_up': nrm((DEPTH, N_EXPERTS, 2 * D_FF), 0.02),
        'w_down': nrm((DEPTH, N_EXPERTS, D_FF, D_MODEL), BETA * D_FF ** -0.5),
        'b_down': nrm((DEPTH, N_EXPERTS, D_MODEL), 0.02),
        'ln2_g': 1.0 + nrm((DEPTH, D_MODEL), 0.02),
        'ln2_b': nrm((DEPTH, D_MODEL), 0.02),
    }


def reference(x_prompt, x_sample, cache_attn_k, cache_attn_v, state_ssd, state_ssd_conv, state_short_conv,
              w_in, ssd_conv_w, ssd_conv_b, ssd_dt_bias, ssd_a_log, ssd_d, ssd_norm_w, w_ssd_out,
              attn_sinks, w_attn_out, rel_bias, sc_conv_w, w_sc_out, w_o, ln1_g, ln1_b,
              router_w, router_b, w_up, b_up, w_down, b_down, ln2_g, ln2_b):
    bp = x_prompt.shape[0]
    dt = x_prompt.dtype
    yp, ys = x_prompt, x_sample
    kp, vp, sp, cp, scp = [], [], [], [], []
    kd, vd, sd, cd, scd = [], [], [], [], []
    for l in range(DEPTH):
        lp = (w_in[l], ssd_conv_w[l], ssd_conv_b[l], ssd_dt_bias[l], ssd_a_log[l], ssd_d[l], ssd_norm_w[l],
              w_ssd_out[l], attn_sinks[l], w_attn_out[l], sc_conv_w[l], w_sc_out[l], w_o[l], ln1_g[l], ln1_b[l],
              router_w[l], router_b[l], w_up[l], b_up[l], w_down[l], b_down[l], ln2_g[l], ln2_b[l])
        yp, k_, v_, s_, c_, sc_ = trunk_layer(
            yp, jnp.zeros((bp, SSD_CONV - 1, SSD_CONV_DIM), dt),
            jnp.zeros((bp, SSD_N_HEADS, SSD_HEAD_DIM, SSD_D_STATE), dt),
            jnp.zeros((bp, SC_WIDTH - 1, SC_DIM), dt), None, rel_bias, *lp)
        kp.append(k_); vp.append(v_); sp.append(s_); cp.append(c_); scp.append(sc_)
        ys, k_, v_, s_, c_, sc_ = trunk_layer(
            ys, state_ssd_conv[l], state_ssd[l], state_short_conv[l],
            (cache_attn_k[l], cache_attn_v[l]), rel_bias, *lp)
        kd.append(k_); vd.append(v_); sd.append(s_); cd.append(c_); scd.append(sc_)
    return (yp, ys,
            jnp.stack(kp), jnp.stack(vp), jnp.stack(sp), jnp.stack(cp), jnp.stack(scp),
            jnp.stack(kd), jnp.stack(vd), jnp.stack(sd), jnp.stack(cd), jnp.stack(scd))
```

```python
import functools
import math

import jax
import jax.numpy as jnp
import numpy as np
from jax import lax
from jax.experimental import pallas as pl
from jax.experimental.pallas import tpu as pltpu

F32 = jnp.float32
BF16 = jnp.bfloat16

D_MODEL = 1024
DEPTH = 2
CHUNK = 64
SSD_D_INNER = 2 * D_MODEL
SSD_HEAD_DIM = 64
SSD_N_HEADS = SSD_D_INNER // SSD_HEAD_DIM
SSD_N_GROUPS = 8
SSD_D_STATE = 128
SSD_CONV = 4
SSD_GN = SSD_N_GROUPS * SSD_D_STATE
SSD_CONV_DIM = SSD_D_INNER + 2 * SSD_GN
SSD_HEADS_PER_GROUP = SSD_N_HEADS // SSD_N_GROUPS
SSD_NORM_GROUP = SSD_D_INNER // SSD_N_GROUPS
ATT_HEAD_DIM = 64
ATT_HEADS = D_MODEL // ATT_HEAD_DIM
ATT_KV_HEADS = ATT_HEADS // 4
ATT_GROUP = ATT_HEADS // ATT_KV_HEADS
ATT_KV_DIM = ATT_KV_HEADS * ATT_HEAD_DIM
WINDOW = 128
N_BUCKETS = 32
MAX_DISTANCE = 128
SC_WIDTH = 3
N_EXPERTS = 32
TOP_K = 4
D_FF = D_MODEL
SWIGLU_LIMIT = 7.0
SWIGLU_ALPHA = 1.702
LN_EPS = 1e-5
ALPHA = (2.0 * DEPTH) ** 0.25

LANES = 128
SUBLANES = 8
ROW_TILES = D_MODEL // LANES
VMEM_LIMIT = 56 * 1024 * 1024

COL_Z, COL_XS, COL_BC = 0, 1, 2
COL_Q, COL_SCB, COL_SCC, COL_SCH, COL_GSSD, COL_GATT, COL_GSC = 6, 7, 8, 9, 10, 11, 12
N_MAIN = 13 * D_MODEL
OFF_DT = SSD_D_INNER + SSD_CONV_DIM
OFF_Q = OFF_DT + SSD_N_HEADS
OFF_K = OFF_Q + D_MODEL
OFF_SCB = OFF_K + 2 * ATT_KV_DIM


def _cparams(*sem):
    return pltpu.CompilerParams(dimension_semantics=sem, vmem_limit_bytes=VMEM_LIMIT)


def _matmul_kernel(x_ref, w_ref, o_ref, xb_ref):
    @pl.when(pl.program_id(1) == 0)
    def _():
        xb_ref[...] = x_ref[...].astype(BF16)

    o_ref[...] = jnp.dot(xb_ref[...], w_ref[...], preferred_element_type=F32).astype(o_ref.dtype)


def _matmul(x, w, out_dtype, tm, tn):
    t, k = x.shape
    n = w.shape[1]
    return pl.pallas_call(
        _matmul_kernel,
        grid=(t // tm, n // tn),
        in_specs=[pl.BlockSpec((tm, k), lambda i, j: (i, 0)),
                  pl.BlockSpec((k, tn), lambda i, j: (0, j))],
        out_specs=pl.BlockSpec((tm, tn), lambda i, j: (i, j)),
        out_shape=jax.ShapeDtypeStruct((t, n), out_dtype),
        scratch_shapes=[pltpu.VMEM((tm, k), BF16)],
        compiler_params=_cparams("arbitrary", "arbitrary"),
        name="in_proj",
    )(x, w)


def _ssd_kernel(x_ref, z_ref, xs_ref, bc_ref, hist_ref, st0_ref,
                wdt_ref, wdtT_ref, convw_ref, convb_ref, dtb_ref, dtbT_ref, alog_ref, alogT_ref,
                dexp_ref, normw_ref,
                y_ref, stout_ref, histout_ref,
                ext_ref, st_ref, ybuf_ref, *, L, nc):
    c = pl.program_id(1)

    @pl.when(c == 0)
    def _():
        ext_ref[0:SUBLANES, :] = hist_ref[0]
        st_ref[...] = st0_ref[0]

    xbc = jnp.concatenate([xs_ref[...], bc_ref[...]], axis=1).astype(F32)
    ext_ref[SUBLANES:SUBLANES + L, :] = xbc
    base = SUBLANES - (SSD_CONV - 1)
    conv = convb_ref[...]
    for j in range(SSD_CONV):
        conv = conv + ext_ref[base + j:base + j + L, :] * convw_ref[j:j + 1, :]
    tail = ext_ref[L:L + SUBLANES, :]
    ext_ref[0:SUBLANES, :] = tail
    histout_ref[0] = tail
    xc = conv * jax.nn.sigmoid(conv)
    xs_c = xc[:, :SSD_D_INNER]
    b_m = xc[:, SSD_D_INNER:SSD_D_INNER + SSD_GN]
    c_m = xc[:, SSD_D_INNER + SSD_GN:]

    xb = x_ref[...].astype(BF16)
    dtr = jnp.dot(xb, wdt_ref[...], preferred_element_type=F32)[:, :SSD_N_HEADS]
    dtrT = lax.dot_general(wdtT_ref[...], xb, (((1,), (1,)), ((), ())),
                           preferred_element_type=F32)

    def softplus(v):
        return jnp.maximum(v, 0.0) + jnp.log1p(jnp.exp(-jnp.abs(v)))

    dt = softplus(dtr + dtb_ref[...])
    dtT = softplus(dtrT + dtbT_ref[...])
    adt = dt * (-jnp.exp(alog_ref[...]))
    adtT = dtT * (-jnp.exp(alogT_ref[...]))
    ri = lax.broadcasted_iota(jnp.int32, (L, L), 0)
    ci = lax.broadcasted_iota(jnp.int32, (L, L), 1)
    tri = ri >= ci
    acs = jnp.dot(tri.astype(F32), adt, precision=lax.Precision.HIGHEST,
                  preferred_element_type=F32)
    acsT = jnp.dot(adtT, (ri <= ci).astype(F32), precision=lax.Precision.HIGHEST,
                   preferred_element_type=F32)
    totT = jnp.sum(adtT, axis=1, keepdims=True)
    eacs = jnp.exp(acs)
    wT = jnp.exp(totT - acsT) * dtT
    cdT = jnp.exp(totT)

    for g in range(SSD_N_GROUPS):
        bg = b_m[:, g * SSD_D_STATE:(g + 1) * SSD_D_STATE]
        cg_b = c_m[:, g * SSD_D_STATE:(g + 1) * SSD_D_STATE].astype(BF16)
        cb = lax.dot_general(cg_b, bg.astype(BF16), (((1,), (1,)), ((), ())),
                             preferred_element_type=F32)
        if L < LANES:
            bg = jnp.concatenate([bg, jnp.zeros((LANES - L, SSD_D_STATE), F32)], axis=0)
        bgT = bg.T[:, :L]
        for r in range(SSD_HEADS_PER_GROUP):
            h = g * SSD_HEADS_PER_GROUP + r
            col = acs[:, h:h + 1]
            row = acsT[h:h + 1, :]
            decay = jnp.exp(jnp.where(tri, col - row, -jnp.inf))
            m = (cb * decay * dtT[h:h + 1, :]).astype(BF16)
            xh_b = xs_c[:, h * SSD_HEAD_DIM:(h + 1) * SSD_HEAD_DIM].astype(BF16)
            st = st_ref[h]
            yd = jnp.dot(m, xh_b, preferred_element_type=F32)
            yo = jnp.dot(cg_b, st.astype(BF16), preferred_element_type=F32)
            ybuf_ref[:, h * SSD_HEAD_DIM:(h + 1) * SSD_HEAD_DIM] = yd + yo * eacs[:, h:h + 1]
            wh = (bgT * wT[h:h + 1, :]).astype(BF16)
            stc = jnp.dot(wh, xh_b, preferred_element_type=F32)
            st_ref[h] = st * cdT[h:h + 1, :] + stc

    y = ybuf_ref[...] + dexp_ref[...] * xs_c
    z = z_ref[...].astype(F32)
    y = y * (z * jax.nn.sigmoid(z))
    for g in range(SSD_N_GROUPS):
        sl = slice(g * SSD_NORM_GROUP, (g + 1) * SSD_NORM_GROUP)
        blk = y[:, sl]
        ms = jnp.mean(blk * blk, axis=-1, keepdims=True)
        y_ref[:, sl] = (blk * lax.rsqrt(ms + LN_EPS) * normw_ref[:, sl]).astype(y_ref.dtype)

    @pl.when(c == nc - 1)
    def _():
        stout_ref[0] = st_ref[...]


def _ssd(x2d, h_main, hist8, st0, p, bt, s, L):
    nc = s // L
    t = bt * s
    w2 = 2 * D_MODEL
    row = lambda b, c: (b * nc + c, 0)
    const2 = lambda b, c: (0, 0)
    kern = functools.partial(_ssd_kernel, L=L, nc=nc)
    return pl.pallas_call(
        kern,
        grid=(bt, nc),
        in_specs=[
            pl.BlockSpec((L, D_MODEL), row),
            pl.BlockSpec((L, w2), lambda b, c: (b * nc + c, COL_Z)),
            pl.BlockSpec((L, w2), lambda b, c: (b * nc + c, COL_XS)),
            pl.BlockSpec((L, w2), lambda b, c: (b * nc + c, COL_BC)),
            pl.BlockSpec((1, SUBLANES, SSD_CONV_DIM), lambda b, c: (b, 0, 0)),
            pl.BlockSpec((1, SSD_N_HEADS, SSD_D_STATE, SSD_HEAD_DIM), lambda b, c: (b, 0, 0, 0)),
            pl.BlockSpec((D_MODEL, LANES), const2),
            pl.BlockSpec((SSD_N_HEADS, D_MODEL), const2),
            pl.BlockSpec((SSD_CONV, SSD_CONV_DIM), const2),
            pl.BlockSpec((1, SSD_CONV_DIM), const2),
            pl.BlockSpec((1, SSD_N_HEADS), const2),
            pl.BlockSpec((SSD_N_HEADS, 1), const2),
            pl.BlockSpec((1, SSD_N_HEADS), const2),
            pl.BlockSpec((SSD_N_HEADS, 1), const2),
            pl.BlockSpec((1, SSD_D_INNER), const2),
            pl.BlockSpec((1, SSD_D_INNER), const2),
        ],
        out_specs=[
            pl.BlockSpec((L, SSD_D_INNER), row),
            pl.BlockSpec((1, SSD_N_HEADS, SSD_D_STATE, SSD_HEAD_DIM), lambda b, c: (b, 0, 0, 0)),
            pl.BlockSpec((1, SUBLANES, SSD_CONV_DIM), lambda b, c: (b, 0, 0)),
        ],
        out_shape=[
            jax.ShapeDtypeStruct((t, SSD_D_INNER), BF16),
            jax.ShapeDtypeStruct((bt, SSD_N_HEADS, SSD_D_STATE, SSD_HEAD_DIM), F32),
            jax.ShapeDtypeStruct((bt, SUBLANES, SSD_CONV_DIM), F32),
        ],
        scratch_shapes=[
            pltpu.VMEM((L + SUBLANES, SSD_CONV_DIM), F32),
            pltpu.VMEM((SSD_N_HEADS, SSD_D_STATE, SSD_HEAD_DIM), F32),
            pltpu.VMEM((L, SSD_D_INNER), F32),
        ],
        compiler_params=_cparams("arbitrary", "arbitrary"),
        name="ssd_scan",
    )(x2d, h_main, h_main, h_main, hist8, st0,
      p["w_dt"], p["w_dtT"], p["conv_w"], p["conv_b"], p["dt_b"], p["dt_bT"], p["a_log"], p["a_logT"],
      p["d_exp"], p["norm_w"])


def _attn_kernel(sinks_ref, q_ref, k_ref, v_ref, bias_ref, o_ref, *, cq, n_sub, mask_prefix):
    step = pl.program_id(1)
    kw = WINDOW + cq
    scale = ATT_HEAD_DIM ** -0.5
    for sub in range(n_sub):
        r0 = pl.multiple_of((step * n_sub + sub) * cq, cq)
        kk = k_ref[0, pl.ds(r0, kw), :]
        vv = v_ref[0, pl.ds(r0, kw), :]
        qq = q_ref[sub * cq:(sub + 1) * cq, :]
        if mask_prefix:
            valid = (r0 + lax.broadcasted_iota(jnp.int32, (1, kw), 1)) >= WINDOW
        for h in range(ATT_HEADS):
            hk = h // ATT_GROUP
            qh = qq[:, h * ATT_HEAD_DIM:(h + 1) * ATT_HEAD_DIM]
            kh = kk[:, hk * ATT_HEAD_DIM:(hk + 1) * ATT_HEAD_DIM]
            vh = vv[:, hk * ATT_HEAD_DIM:(hk + 1) * ATT_HEAD_DIM]
            sc = lax.dot_general(qh, kh, (((1,), (1,)), ((), ())), preferred_element_type=F32)
            sc = sc * scale + bias_ref[h]
            if mask_prefix:
                sc = jnp.where(valid, sc, -1e30)
            sink = sinks_ref[h]
            mx = jnp.maximum(jnp.max(sc, axis=-1, keepdims=True), sink)
            e = jnp.exp(sc - mx)
            den = jnp.sum(e, axis=-1, keepdims=True) + jnp.exp(sink - mx)
            pr = (e / den).astype(BF16)
            o_ref[sub * cq:(sub + 1) * cq, h * ATT_HEAD_DIM:(h + 1) * ATT_HEAD_DIM] = jnp.dot(
                pr, vh, preferred_element_type=F32).astype(o_ref.dtype)


def _attention(h_main, kall, vall, bias, sinks, bt, s, cq, n_sub, mask_prefix):
    tq = cq * n_sub
    nq = s // tq
    kw = WINDOW + cq
    kern = functools.partial(_attn_kernel, cq=cq, n_sub=n_sub, mask_prefix=mask_prefix)
    return pl.pallas_call(
        kern,
        grid=(bt, nq),
        in_specs=[
            pl.BlockSpec(memory_space=pltpu.SMEM),
            pl.BlockSpec((tq, D_MODEL), lambda b, i: (b * nq + i, COL_Q)),
            pl.BlockSpec((1, WINDOW + s, ATT_KV_DIM), lambda b, i: (b, 0, 0)),
            pl.BlockSpec((1, WINDOW + s, ATT_KV_DIM), lambda b, i: (b, 0, 0)),
            pl.BlockSpec((ATT_HEADS, cq, kw), lambda b, i: (0, 0, 0)),
        ],
        out_specs=pl.BlockSpec((tq, D_MODEL), lambda b, i: (b * nq + i, 0)),
        out_shape=jax.ShapeDtypeStruct((bt * s, D_MODEL), BF16),
        compiler_params=_cparams("arbitrary", "arbitrary"),
        name="swa_attention",
    )(sinks, h_main, kall, vall, bias)


def _merge_kernel(x_ref, ssd_ref, att_ref, scb_ref, scc_ref, sch_ref, gssd_ref, gatt_ref, gsc_ref, hist_ref,
                  wssd_ref, watt_ref, wsc_ref, wo_ref, scw_ref, lng_ref, lnb_ref, rw_ref, rb_ref,
                  x1r_ref, ir_ref, gate_ref, cnt_ref, histout_ref,
                  ext_ref, base_ref, *, tm, tiles_per_batch):
    i = pl.program_id(0)

    @pl.when(i == 0)
    def _():
        base_ref[...] = jnp.zeros_like(base_ref)

    @pl.when(i % tiles_per_batch == 0)
    def _():
        ext_ref[0:SUBLANES, :] = hist_ref[0]

    u = scc_ref[...].astype(F32) * sch_ref[...].astype(F32)
    ext_ref[SUBLANES:SUBLANES + tm, :] = u
    off = SUBLANES - (SC_WIDTH - 1)
    conv = ext_ref[off:off + tm, :] * scw_ref[0:1, :]
    for j in range(1, SC_WIDTH):
        conv = conv + ext_ref[off + j:off + j + tm, :] * scw_ref[j:j + 1, :]
    tail = ext_ref[tm:tm + SUBLANES, :]
    ext_ref[0:SUBLANES, :] = tail
    histout_ref[0] = tail
    sc_pre = (scb_ref[...].astype(F32) * conv).astype(BF16)

    y_sc = jnp.dot(sc_pre, wsc_ref[...], preferred_element_type=F32)
    y_ssd = jnp.dot(ssd_ref[...], wssd_ref[...], preferred_element_type=F32)
    y_att = jnp.dot(att_ref[...], watt_ref[...], preferred_element_type=F32)
    merged = (jax.nn.sigmoid(gssd_ref[...].astype(F32)) * y_ssd
              + jax.nn.sigmoid(gatt_ref[...].astype(F32)) * y_att
              + jax.nn.sigmoid(gsc_ref[...].astype(F32)) * y_sc)
    v = ALPHA * x_ref[...] + jnp.dot(merged.astype(BF16), wo_ref[...], preferred_element_type=F32)
    mu = jnp.mean(v, axis=-1, keepdims=True)
    vc = v - mu
    var = jnp.mean(vc * vc, axis=-1, keepdims=True)
    x1 = vc * lax.rsqrt(var + LN_EPS) * lng_ref[...] + lnb_ref[...]
    for j in range(ROW_TILES):
        x1r_ref[pl.ds(j, tm, stride=ROW_TILES), :] = x1[:, j * LANES:(j + 1) * LANES]

    lane = lax.broadcasted_iota(jnp.int32, (tm, LANES), 1)
    logits = jnp.dot(x1.astype(BF16), rw_ref[...], preferred_element_type=F32) + rb_ref[...]
    work = jnp.where(lane < N_EXPERTS, logits, -jnp.inf)
    vals, idxs = [], []
    onehot = jnp.zeros((tm, LANES), F32)
    for k in range(TOP_K):
        mv = jnp.max(work, axis=-1, keepdims=True)
        mi = jnp.min(jnp.where(work == mv, lane, LANES), axis=-1, keepdims=True)
        vals.append(mv)
        idxs.append(mi)
        work = jnp.where(lane == mi, -jnp.inf, work)
        onehot = onehot + (lane == mi + k * N_EXPERTS).astype(F32)
    es = [jnp.exp(vk - vals[0]) for vk in vals]
    den = es[0] + es[1] + es[2] + es[3]
    ri = lax.broadcasted_iota(jnp.int32, (tm, tm), 0)
    ci = lax.broadcasted_iota(jnp.int32, (tm, tm), 1)
    ltri = (ri > ci).astype(BF16)
    prefix = jnp.dot(ltri, onehot.astype(BF16), preferred_element_type=F32)
    cnt = jnp.sum(onehot, axis=0, keepdims=True)
    lane1 = lax.broadcasted_iota(jnp.int32, (1, LANES), 1)
    base = base_ref[0:1, :]
    offs = base
    tot = cnt
    for sh in range(1, TOP_K):
        rolled = pltpu.roll(cnt, sh * N_EXPERTS, 1)
        offs = offs + jnp.where(lane1 >= sh * N_EXPERTS, rolled, 0.0)
        tot = tot + rolled
    rank_all = prefix + offs
    ir = jnp.zeros((tm, LANES), jnp.int32)
    gt = jnp.zeros((tm, LANES), F32)
    for k in range(TOP_K):
        sel = lane == idxs[k] + k * N_EXPERTS
        rk = jnp.sum(jnp.where(sel, rank_all, 0.0), axis=-1, keepdims=True).astype(jnp.int32)
        ir = jnp.where(lane == k, idxs[k], ir)
        ir = jnp.where(lane == TOP_K + k, rk, ir)
        gt = jnp.where(lane == k, es[k] / den, gt)
    ir_ref[...] = ir
    gate_ref[...] = gt
    new_base = base + tot
    base_ref[...] = jnp.broadcast_to(new_base, base_ref.shape)
    cnt_ref[...] = jnp.broadcast_to(new_base, cnt_ref.shape)


def _merge(x2d, h_main, ssd_pre, att, schist8, p, bt, s, tm):
    t = bt * s
    nt = t // tm
    tiles_per_batch = s // tm
    row = lambda i: (i, 0)
    const2 = lambda i: (0, 0)
    hcol = lambda cidx: pl.BlockSpec((tm, D_MODEL), lambda i: (i, cidx))
    kern = functools.partial(_merge_kernel, tm=tm, tiles_per_batch=tiles_per_batch)
    return pl.pallas_call(
        kern,
        grid=(nt,),
        in_specs=[
            pl.BlockSpec((tm, D_MODEL), row),
            pl.BlockSpec((tm, SSD_D_INNER), row),
            pl.BlockSpec((tm, D_MODEL), row),
            hcol(COL_SCB), hcol(COL_SCC), hcol(COL_SCH), hcol(COL_GSSD), hcol(COL_GATT), hcol(COL_GSC),
            pl.BlockSpec((1, SUBLANES, D_MODEL), lambda i: (i // tiles_per_batch, 0, 0)),
            pl.BlockSpec((SSD_D_INNER, D_MODEL), const2),
            pl.BlockSpec((D_MODEL, D_MODEL), const2),
            pl.BlockSpec((D_MODEL, D_MODEL), const2),
            pl.BlockSpec((D_MODEL, D_MODEL), const2),
            pl.BlockSpec((SC_WIDTH, D_MODEL), const2),
            pl.BlockSpec((1, D_MODEL), const2),
            pl.BlockSpec((1, D_MODEL), const2),
            pl.BlockSpec((D_MODEL, LANES), const2),
            pl.BlockSpec((1, LANES), const2),
        ],
        out_specs=[
            pl.BlockSpec((tm * ROW_TILES, LANES), row),
            pl.BlockSpec((tm, LANES), row),
            pl.BlockSpec((tm, LANES), row),
            pl.BlockSpec((SUBLANES, LANES), const2),
            pl.BlockSpec((1, SUBLANES, D_MODEL), lambda i: (i // tiles_per_batch, 0, 0)),
        ],
        out_shape=[
            jax.ShapeDtypeStruct((t * ROW_TILES, LANES), F32),
            jax.ShapeDtypeStruct((t, LANES), jnp.int32),
            jax.ShapeDtypeStruct((t, LANES), F32),
            jax.ShapeDtypeStruct((SUBLANES, LANES), F32),
            jax.ShapeDtypeStruct((bt, SUBLANES, D_MODEL), F32),
        ],
        scratch_shapes=[
            pltpu.VMEM((tm + SUBLANES, D_MODEL), F32),
            pltpu.VMEM((SUBLANES, LANES), F32),
        ],
        compiler_params=_cparams("arbitrary"),
        name="merge_ln_router",
    )(x2d, ssd_pre, att, h_main, h_main, h_main, h_main, h_main, h_main, schist8,
      p["w_ssd_out"], p["w_att_out"], p["w_sc_out"], p["w_o"], p["sc_w"], p["ln1_g"], p["ln1_b"],
      p["router_w"], p["router_b"])


def _row_copy_wait(src_hbm, dst, sem, n_rows):
    pltpu.make_async_copy(src_hbm.at[pl.ds(0, n_rows * ROW_TILES), :],
                          dst.at[pl.ds(0, n_rows * ROW_TILES), :], sem).wait()


def _dispatch_kernel(pstart_ref, ir_ref, x_hbm, xs_in_hbm, xs_hbm, sem, *, tm):
    del xs_in_hbm
    i = pl.program_id(0)

    def issue(tk, carry):
        src = pl.multiple_of((i * tm + tk) * ROW_TILES, ROW_TILES)
        for k in range(TOP_K):
            dest = pstart_ref[ir_ref[tk * SUBLANES + k]] + ir_ref[tk * SUBLANES + TOP_K + k]
            dst = pl.multiple_of(dest * ROW_TILES, ROW_TILES)
            pltpu.make_async_copy(x_hbm.at[pl.ds(src, ROW_TILES), :],
                                  xs_hbm.at[pl.ds(dst, ROW_TILES), :], sem).start()
        return carry

    lax.fori_loop(0, tm, issue, 0)
    for _ in range(TOP_K):
        _row_copy_wait(x_hbm, xs_hbm, sem, tm)


def _dispatch(x1r, ir_flat, pstart, rows, t, tm):
    xs0 = jnp.zeros((rows * ROW_TILES, LANES), F32)
    return pl.pallas_call(
        functools.partial(_dispatch_kernel, tm=tm),
        grid_spec=pltpu.PrefetchScalarGridSpec(
            num_scalar_prefetch=1,
            grid=(t // tm,),
            in_specs=[
                pl.BlockSpec((tm * SUBLANES,), lambda i, ps: (i,), memory_space=pltpu.SMEM),
                pl.BlockSpec(memory_space=pl.ANY),
                pl.BlockSpec(memory_space=pl.ANY),
            ],
            out_specs=pl.BlockSpec(memory_space=pl.ANY),
            scratch_shapes=[pltpu.SemaphoreType.DMA],
        ),
        out_shape=jax.ShapeDtypeStruct((rows * ROW_TILES, LANES), F32),
        input_output_aliases={3: 0},
        compiler_params=_cparams("arbitrary"),
        name="moe_dispatch",
    )(pstart, ir_flat, x1r, xs0)


def _moe_kernel(be_ref, nv_ref, x_ref, wu_ref, bu_ref, wd_ref, bd_ref, o_ref, *, tmo):
    i = pl.program_id(0)

    @pl.when(i < nv_ref[0])
    def _():
        x = jnp.concatenate([x_ref[pl.ds(j, tmo, stride=ROW_TILES), :] for j in range(ROW_TILES)], axis=1)
        h = jnp.dot(x.astype(BF16), wu_ref[0], preferred_element_type=F32) + bu_ref[0]
        gate = jnp.minimum(h[:, :D_FF], SWIGLU_LIMIT)
        up = jnp.clip(h[:, D_FF:], -SWIGLU_LIMIT, SWIGLU_LIMIT)
        act = (up + 1.0) * gate * jax.nn.sigmoid(SWIGLU_ALPHA * gate)
        o = jnp.dot(act.astype(BF16), wd_ref[0], preferred_element_type=F32) + bd_ref[0]
        for j in range(ROW_TILES):
            o_ref[pl.ds(j, tmo, stride=ROW_TILES), :] = o[:, j * LANES:(j + 1) * LANES]

    @pl.when(i >= nv_ref[0])
    def _():
        o_ref[...] = jnp.zeros_like(o_ref)


def _moe_experts(xs, block_e, nvalid, p, n_blocks, tmo):
    return pl.pallas_call(
        functools.partial(_moe_kernel, tmo=tmo),
        grid_spec=pltpu.PrefetchScalarGridSpec(
            num_scalar_prefetch=2,
            grid=(n_blocks,),
            in_specs=[
                pl.BlockSpec((tmo * ROW_TILES, LANES), lambda i, be, nv: (i, 0)),
                pl.BlockSpec((1, D_MODEL, 2 * D_FF), lambda i, be, nv: (be[i], 0, 0)),
                pl.BlockSpec((1, 1, 2 * D_FF), lambda i, be, nv: (be[i], 0, 0)),
                pl.BlockSpec((1, D_FF, D_MODEL), lambda i, be, nv: (be[i], 0, 0)),
                pl.BlockSpec((1, 1, D_MODEL), lambda i, be, nv: (be[i], 0, 0)),
            ],
            out_specs=pl.BlockSpec((tmo * ROW_TILES, LANES), lambda i, be, nv: (i, 0)),
        ),
        out_shape=jax.ShapeDtypeStruct(xs.shape, F32),
        compiler_params=_cparams("arbitrary"),
        name="moe_experts",
    )(block_e, nvalid, xs, p["w_up"], p["b_up"], p["w_down"], p["b_down"])


def _combine_kernel(pstart_ref, ir_ref, x1r_ref, gate_ref, lng_ref, lnb_ref, eo_hbm, o_ref, buf_ref, sem, *, tm):
    def issue(tk, carry):
        for k in range(TOP_K):
            dest = pstart_ref[ir_ref[tk * SUBLANES + k]] + ir_ref[tk * SUBLANES + TOP_K + k]
            src = pl.multiple_of(dest * ROW_TILES, ROW_TILES)
            dst = pl.multiple_of(tk * ROW_TILES, ROW_TILES)
            pltpu.make_async_copy(eo_hbm.at[pl.ds(src, ROW_TILES), :],
                                  buf_ref.at[k, pl.ds(dst, ROW_TILES), :], sem).start()
        return carry

    lax.fori_loop(0, tm, issue, 0)
    for k in range(TOP_K):
        _row_copy_wait(eo_hbm, buf_ref.at[k], sem, tm)

    gates = gate_ref[...]
    vs = []
    tot = jnp.zeros((tm, 1), F32)
    for j in range(ROW_TILES):
        acc = ALPHA * x1r_ref[pl.ds(j, tm, stride=ROW_TILES), :]
        for k in range(TOP_K):
            acc = acc + buf_ref[k, pl.ds(j, tm, stride=ROW_TILES), :] * gates[:, k:k + 1]
        vs.append(acc)
        tot = tot + jnp.sum(acc, axis=-1, keepdims=True)
    mu = tot * (1.0 / D_MODEL)
    sq = jnp.zeros((tm, 1), F32)
    for j in range(ROW_TILES):
        vs[j] = vs[j] - mu
        sq = sq + jnp.sum(vs[j] * vs[j], axis=-1, keepdims=True)
    inv = lax.rsqrt(sq * (1.0 / D_MODEL) + LN_EPS)
    for j in range(ROW_TILES):
        sl = slice(j * LANES, (j + 1) * LANES)
        o_ref[:, sl] = vs[j] * inv * lng_ref[:, sl] + lnb_ref[:, sl]


def _combine(x1r, ir_flat, gates, pstart, eo, p, t, tm):
    return pl.pallas_call(
        functools.partial(_combine_kernel, tm=tm),
        grid_spec=pltpu.PrefetchScalarGridSpec(
            num_scalar_prefetch=1,
            grid=(t // tm,),
            in_specs=[
                pl.BlockSpec((tm * SUBLANES,), lambda i, ps: (i,), memory_space=pltpu.SMEM),
                pl.BlockSpec((tm * ROW_TILES, LANES), lambda i, ps: (i, 0)),
                pl.BlockSpec((tm, LANES), lambda i, ps: (i, 0)),
                pl.BlockSpec((1, D_MODEL), lambda i, ps: (0, 0)),
                pl.BlockSpec((1, D_MODEL), lambda i, ps: (0, 0)),
                pl.BlockSpec(memory_space=pl.ANY),
            ],
            out_specs=pl.BlockSpec((tm, D_MODEL), lambda i, ps: (i, 0)),
            scratch_shapes=[pltpu.VMEM((TOP_K, tm * ROW_TILES, LANES), F32), pltpu.SemaphoreType.DMA],
        ),
        out_shape=jax.ShapeDtypeStruct((t, D_MODEL), F32),
        compiler_params=_cparams("arbitrary"),
        name="moe_combine_ln",
    )(pstart, ir_flat, x1r, gates, p["ln2_g"], p["ln2_b"], eo)


def _moe(x1r, ir, gates, cnt, p, t, tm, tmo):
    n_assign = t * TOP_K
    n_blocks = (n_assign + N_EXPERTS * (tmo - 1) + tmo - 1) // tmo
    rows = n_blocks * tmo
    counts = cnt[0, :N_EXPERTS].astype(jnp.int32)
    padded = (counts + tmo - 1) // tmo * tmo
    pend = jnp.cumsum(padded)
    pstart = (pend - padded).astype(jnp.int32)
    block_e = jnp.minimum(jnp.searchsorted(pend, jnp.arange(n_blocks, dtype=jnp.int32) * tmo, side="right"),
                          N_EXPERTS - 1).astype(jnp.int32)
    nvalid = (pend[-1:] // tmo).astype(jnp.int32)
    ir_flat = ir[:, :SUBLANES].reshape(t * SUBLANES)
    xs = _dispatch(x1r, ir_flat, pstart, rows, t, tm)
    eo = _moe_experts(xs, block_e, nvalid, p, n_blocks, tmo)
    return _combine(x1r, ir_flat, gates, pstart, eo, p, t, tm)


def _t5_bucket(rel):
    half = N_BUCKETS // 2
    max_exact = half // 2
    ret = jnp.where(rel > 0, half, 0)
    n = jnp.abs(rel)
    nf = jnp.maximum(n, 1).astype(F32)
    large = max_exact + (jnp.log(nf / max_exact) / math.log(MAX_DISTANCE / max_exact)
                         * (half - max_exact)).astype(jnp.int32)
    large = jnp.minimum(large, half - 1)
    return ret + jnp.where(n < max_exact, n, large)


def _rel_bias(table, cq):
    koff = jnp.arange(WINDOW + cq) - WINDOW
    rel = koff[None, :] - jnp.arange(cq)[:, None]
    return jnp.transpose(table[_t5_bucket(rel)].astype(F32), (2, 0, 1))


def _pad_rows_front(a, rows):
    pad = jnp.zeros(a.shape[:1] + (rows - a.shape[1],) + a.shape[2:], a.dtype)
    return jnp.concatenate([pad, a], axis=1)


def _layer_params(l, w_in, ssd_conv_w, ssd_conv_b, ssd_dt_bias, ssd_a_log, ssd_d, ssd_norm_w, w_ssd_out,
                  attn_sinks, w_attn_out, sc_conv_w, w_sc_out, w_o, ln1_g, ln1_b,
                  router_w, router_b, w_up, b_up, w_down, b_down, ln2_g, ln2_b):
    wi = w_in[l]
    w_dt = wi[:, OFF_DT:OFF_Q]
    return {
        "w_main": jnp.concatenate([wi[:, :OFF_DT], wi[:, OFF_Q:OFF_K], wi[:, OFF_SCB:]], axis=1).astype(BF16),
        "w_kv": wi[:, OFF_K:OFF_SCB].astype(BF16),
        "w_dt": jnp.pad(w_dt, ((0, 0), (0, LANES - SSD_N_HEADS))).astype(BF16),
        "w_dtT": w_dt.T.astype(BF16),
        "conv_w": ssd_conv_w[l],
        "conv_b": ssd_conv_b[l][None, :],
        "dt_b": ssd_dt_bias[l][None, :],
        "dt_bT": ssd_dt_bias[l][:, None],
        "a_log": ssd_a_log[l][None, :],
        "a_logT": ssd_a_log[l][:, None],
        "d_exp": jnp.repeat(ssd_d[l], SSD_HEAD_DIM)[None, :],
        "norm_w": ssd_norm_w[l][None, :],
        "w_ssd_out": w_ssd_out[l].astype(BF16),
        "sinks": attn_sinks[l],
        "w_att_out": w_attn_out[l].astype(BF16),
        "sc_w": sc_conv_w[l],
        "w_sc_out": w_sc_out[l].astype(BF16),
        "w_o": w_o[l].astype(BF16),
        "ln1_g": ln1_g[l][None, :],
        "ln1_b": ln1_b[l][None, :],
        "router_w": jnp.pad(router_w[l], ((0, 0), (0, LANES - N_EXPERTS))).astype(BF16),
        "router_b": jnp.pad(router_b[l], (0, LANES - N_EXPERTS))[None, :],
        "w_up": w_up[l].astype(BF16),
        "b_up": b_up[l][:, None, :],
        "w_down": w_down[l].astype(BF16),
        "b_down": b_down[l][:, None, :],
        "ln2_g": ln2_g[l][None, :],
        "ln2_b": ln2_b[l][None, :],
    }


def _trunk_layer(x2d, bt, s, p, bias, ssd_hist, ssd_state, sc_hist, kv_cache, cfg):
    t = bt * s
    h_main = _matmul(x2d, p["w_main"], BF16, cfg["tm_in"], cfg["tn_in"])
    h_kv = _matmul(x2d, p["w_kv"], F32, cfg["tm_in"], 2 * ATT_KV_DIM)
    k_new = h_kv[:, :ATT_KV_DIM].reshape(bt, s, ATT_KV_DIM)
    v_new = h_kv[:, ATT_KV_DIM:].reshape(bt, s, ATT_KV_DIM)

    if kv_cache is None:
        zpad = jnp.zeros((bt, WINDOW, ATT_KV_DIM), BF16)
        kall = jnp.concatenate([zpad, k_new.astype(BF16)], axis=1)
        vall = jnp.concatenate([zpad, v_new.astype(BF16)], axis=1)
    else:
        kall = jnp.concatenate([kv_cache[0].astype(BF16), k_new.astype(BF16)], axis=1)
        vall = jnp.concatenate([kv_cache[1].astype(BF16), v_new.astype(BF16)], axis=1)
    att = _attention(h_main, kall, vall, bias, p["sinks"], bt, s, cfg["cq"], cfg["n_sub"], kv_cache is None)

    if ssd_hist is None:
        hist8 = jnp.zeros((bt, SUBLANES, SSD_CONV_DIM), F32)
        st0 = jnp.zeros((bt, SSD_N_HEADS, SSD_D_STATE, SSD_HEAD_DIM), F32)
        schist8 = jnp.zeros((bt, SUBLANES, D_MODEL), F32)
    else:
        hist8 = _pad_rows_front(ssd_hist, SUBLANES)
        st0 = jnp.swapaxes(ssd_state, 2, 3)
        schist8 = _pad_rows_front(sc_hist, SUBLANES)
    ssd_pre, st_out, hist_out = _ssd(x2d, h_main, hist8, st0, p, bt, s, cfg["cq"])

    x1r, ir, gates, cnt, schist_out = _merge(x2d, h_main, ssd_pre, att, schist8, p, bt, s, cfg["tm_merge"])
    x2 = _moe(x1r, ir, gates, cnt, p, t, cfg["tm_moe"], cfg["tmo"])

    n_keep = min(s, WINDOW)
    new_k = k_new[:, s - n_keep:].reshape(bt, n_keep, ATT_KV_HEADS, ATT_HEAD_DIM)
    new_v = v_new[:, s - n_keep:].reshape(bt, n_keep, ATT_KV_HEADS, ATT_HEAD_DIM)
    new_state = jnp.swapaxes(st_out, 2, 3)
    new_hist = hist_out[:, SUBLANES - (SSD_CONV - 1):]
    new_sc_hist = schist_out[:, SUBLANES - (SC_WIDTH - 1):]
    return x2, new_k, new_v, new_state, new_hist, new_sc_hist


def _config(bt, s, prompt):
    t = bt * s
    if prompt:
        return {"tm_in": min(t, 1024), "tn_in": 1024, "cq": CHUNK, "n_sub": min(4, s // CHUNK),
                "tm_merge": min(s, 256), "tm_moe": min(t, 256), "tmo": min(512, max(64, t // 32))}
    return {"tm_in": min(t, 256), "tn_in": 1024, "cq": s, "n_sub": 1,
            "tm_merge": s, "tm_moe": min(t, 256), "tmo": 64}


def _forward(x_prompt, x_sample, cache_attn_k, cache_attn_v, state_ssd, state_ssd_conv, state_short_conv,
             rel_bias, layer_weights):
    bp, sp, _ = x_prompt.shape
    bd, sd, _ = x_sample.shape
    cfg_p = _config(bp, sp, True)
    cfg_d = _config(bd, sd, False)
    bias_p = _rel_bias(rel_bias, cfg_p["cq"])
    bias_d = _rel_bias(rel_bias, cfg_d["cq"])
    yp = x_prompt.reshape(bp * sp, D_MODEL)
    ys = x_sample.reshape(bd * sd, D_MODEL)
    outs_p, outs_d = [], []
    for l in range(DEPTH):
        p = _layer_params(l, *layer_weights)
        yp, *rest = _trunk_layer(yp, bp, sp, p, bias_p, None, None, None, None, cfg_p)
        outs_p.append(rest)
        cache = (cache_attn_k[l].reshape(bd, WINDOW, ATT_KV_DIM), cache_attn_v[l].reshape(bd, WINDOW, ATT_KV_DIM))
        ys, *rest = _trunk_layer(ys, bd, sd, p, bias_d, state_ssd_conv[l], state_ssd[l], state_short_conv[l],
                                 cache, cfg_d)
        outs_d.append(rest)
    stack = lambda outs, i: jnp.stack([o[i] for o in outs])
    return (yp.reshape(bp, sp, D_MODEL), ys.reshape(bd, sd, D_MODEL),
            stack(outs_p, 0), stack(outs_p, 1), stack(outs_p, 2), stack(outs_p, 3), stack(outs_p, 4),
            stack(outs_d, 0), stack(outs_d, 1), stack(outs_d, 2), stack(outs_d, 3), stack(outs_d, 4))


def kernel(x_prompt, x_sample, cache_attn_k, cache_attn_v, state_ssd, state_ssd_conv, state_short_conv, w_in, ssd_conv_w, ssd_conv_b, ssd_dt_bias, ssd_a_log, ssd_d, ssd_norm_w, w_ssd_out, attn_sinks, w_attn_out, rel_bias, sc_conv_w, w_sc_out, w_o, ln1_g, ln1_b, router_w, router_b, w_up, b_up, w_down, b_down, ln2_g, ln2_b):
    layer_weights = (w_in, ssd_conv_w, ssd_conv_b, ssd_dt_bias, ssd_a_log, ssd_d, ssd_norm_w, w_ssd_out,
                     attn_sinks, w_attn_out, sc_conv_w, w_sc_out, w_o, ln1_g, ln1_b,
                     router_w, router_b, w_up, b_up, w_down, b_down, ln2_g, ln2_b)
    return _forward(x_prompt, x_sample, cache_attn_k, cache_attn_v, state_ssd, state_ssd_conv, state_short_conv,
                    rel_bias, layer_weights)
```

```python
import functools
import math

import jax
import jax.numpy as jnp
import numpy as np
from jax import lax
from jax.experimental import pallas as pl
from jax.experimental.pallas import tpu as pltpu

F32 = jnp.float32
BF16 = jnp.bfloat16

D_MODEL = 1024
DEPTH = 2
CHUNK = 64
SSD_D_INNER = 2 * D_MODEL
SSD_HEAD_DIM = 64
SSD_N_HEADS = SSD_D_INNER // SSD_HEAD_DIM
SSD_N_GROUPS = 8
SSD_D_STATE = 128
SSD_CONV = 4
SSD_GN = SSD_N_GROUPS * SSD_D_STATE
SSD_CONV_DIM = SSD_D_INNER + 2 * SSD_GN
SSD_HEADS_PER_GROUP = SSD_N_HEADS // SSD_N_GROUPS
SSD_NORM_GROUP = SSD_D_INNER // SSD_N_GROUPS
ATT_HEAD_DIM = 64
ATT_HEADS = D_MODEL // ATT_HEAD_DIM
ATT_KV_HEADS = ATT_HEADS // 4
ATT_GROUP = ATT_HEADS // ATT_KV_HEADS
ATT_KV_DIM = ATT_KV_HEADS * ATT_HEAD_DIM
WINDOW = 128
N_BUCKETS = 32
MAX_DISTANCE = 128
SC_WIDTH = 3
N_EXPERTS = 32
TOP_K = 4
D_FF = D_MODEL
SWIGLU_LIMIT = 7.0
SWIGLU_ALPHA = 1.702
LN_EPS = 1e-5
ALPHA = (2.0 * DEPTH) ** 0.25

LANES = 128
SUBLANES = 8
ROW_TILES = D_MODEL // LANES
VMEM_LIMIT = 56 * 1024 * 1024

COL_Z, COL_XS, COL_BC = 0, 1, 2
COL_Q, COL_SCB, COL_SCC, COL_SCH, COL_GSSD, COL_GATT, COL_GSC = 6, 7, 8, 9, 10, 11, 12
N_MAIN = 13 * D_MODEL
OFF_DT = SSD_D_INNER + SSD_CONV_DIM
OFF_Q = OFF_DT + SSD_N_HEADS
OFF_K = OFF_Q + D_MODEL
OFF_SCB = OFF_K + 2 * ATT_KV_DIM


def _cparams(*sem):
    return pltpu.CompilerParams(dimension_semantics=sem, vmem_limit_bytes=VMEM_LIMIT)


def _matmul_kernel(x_ref, w_ref, o_ref, xb_ref):
    @pl.when(pl.program_id(1) == 0)
    def _():
        xb_ref[...] = x_ref[...].astype(BF16)

    o_ref[...] = jnp.dot(xb_ref[...], w_ref[...], preferred_element_type=F32).astype(o_ref.dtype)


def _matmul(x, w, out_dtype, tm, tn):
    t, k = x.shape
    n = w.shape[1]
    return pl.pallas_call(
        _matmul_kernel,
        grid=(t // tm, n // tn),
        in_specs=[pl.BlockSpec((tm, k), lambda i, j: (i, 0)),
                  pl.BlockSpec((k, tn), lambda i, j: (0, j))],
        out_specs=pl.BlockSpec((tm, tn), lambda i, j: (i, j)),
        out_shape=jax.ShapeDtypeStruct((t, n), out_dtype),
        scratch_shapes=[pltpu.VMEM((tm, k), BF16)],
        compiler_params=_cparams("arbitrary", "arbitrary"),
        name="in_proj",
    )(x, w)


def _ssd_kernel(x_ref, z_ref, xs_ref, bc_ref, hist_ref, st0_ref,
                wdt_ref, wdtT_ref, convw_ref, convb_ref, dtb_ref, dtbT_ref, alog_ref, alogT_ref,
                dexp_ref, normw_ref,
                y_ref, stout_ref, histout_ref,
                ext_ref, st_ref, ybuf_ref, *, L, nc):
    c = pl.program_id(1)

    @pl.when(c == 0)
    def _():
        ext_ref[0:SUBLANES, :] = hist_ref[0]
        st_ref[...] = st0_ref[0]

    xbc = jnp.concatenate([xs_ref[...], bc_ref[...]], axis=1).astype(F32)
    ext_ref[SUBLANES:SUBLANES + L, :] = xbc
    base = SUBLANES - (SSD_CONV - 1)
    conv = convb_ref[...]
    for j in range(SSD_CONV):
        conv = conv + ext_ref[base + j:base + j + L, :] * convw_ref[j:j + 1, :]
    tail = ext_ref[L:L + SUBLANES, :]
    ext_ref[0:SUBLANES, :] = tail
    histout_ref[0] = tail
    xc = conv * jax.nn.sigmoid(conv)
    xs_c = xc[:, :SSD_D_INNER]
    b_m = xc[:, SSD_D_INNER:SSD_D_INNER + SSD_GN]
    c_m = xc[:, SSD_D_INNER + SSD_GN:]

    xb = x_ref[...].astype(BF16)
    dtr = jnp.dot(xb, wdt_ref[...], preferred_element_type=F32)[:, :SSD_N_HEADS]
    dtrT = lax.dot_general(wdtT_ref[...], xb, (((1,), (1,)), ((), ())),
                           preferred_element_type=F32)

    def softplus(v):
        return jnp.maximum(v, 0.0) + jnp.log1p(jnp.exp(-jnp.abs(v)))

    dt = softplus(dtr + dtb_ref[...])
    dtT = softplus(dtrT + dtbT_ref[...])
    adt = dt * (-jnp.exp(alog_ref[...]))
    adtT = dtT * (-jnp.exp(alogT_ref[...]))
    ri = lax.broadcasted_iota(jnp.int32, (L, L), 0)
    ci = lax.broadcasted_iota(jnp.int32, (L, L), 1)
    tri = ri >= ci
    acs = jnp.dot(tri.astype(F32), adt, precision=lax.Precision.HIGHEST,
                  preferred_element_type=F32)
    acsT = jnp.dot(adtT, (ri <= ci).astype(F32), precision=lax.Precision.HIGHEST,
                   preferred_element_type=F32)
    totT = jnp.sum(adtT, axis=1, keepdims=True)
    eacs = jnp.exp(acs)
    wT = jnp.exp(totT - acsT) * dtT
    cdT = jnp.exp(totT)

    for g in range(SSD_N_GROUPS):
        bg = b_m[:, g * SSD_D_STATE:(g + 1) * SSD_D_STATE]
        cg_b = c_m[:, g * SSD_D_STATE:(g + 1) * SSD_D_STATE].astype(BF16)
        cb = lax.dot_general(cg_b, bg.astype(BF16), (((1,), (1,)), ((), ())),
                             preferred_element_type=F32)
        if L < LANES:
            bg = jnp.concatenate([bg, jnp.zeros((LANES - L, SSD_D_STATE), F32)], axis=0)
        bgT = bg.T[:, :L]
        for r in range(SSD_HEADS_PER_GROUP):
            h = g * SSD_HEADS_PER_GROUP + r
            col = acs[:, h:h + 1]
            row = acsT[h:h + 1, :]
            decay = jnp.exp(jnp.where(tri, col - row, -jnp.inf))
            m = (cb * decay * dtT[h:h + 1, :]).astype(BF16)
            xh_b = xs_c[:, h * SSD_HEAD_DIM:(h + 1) * SSD_HEAD_DIM].astype(BF16)
            st = st_ref[h]
            yd = jnp.dot(m, xh_b, preferred_element_type=F32)
            yo = jnp.dot(cg_b, st.astype(BF16), preferred_element_type=F32)
            ybuf_ref[:, h * SSD_HEAD_DIM:(h + 1) * SSD_HEAD_DIM] = yd + yo * eacs[:, h:h + 1]
            wh = (bgT * wT[h:h + 1, :]).astype(BF16)
            stc = jnp.dot(wh, xh_b, preferred_element_type=F32)
            st_ref[h] = st * cdT[h:h + 1, :] + stc

    y = ybuf_ref[...] + dexp_ref[...] * xs_c
    z = z_ref[...].astype(F32)
    y = y * (z * jax.nn.sigmoid(z))
    for g in range(SSD_N_GROUPS):
        sl = slice(g * SSD_NORM_GROUP, (g + 1) * SSD_NORM_GROUP)
        blk = y[:, sl]
        ms = jnp.mean(blk * blk, axis=-1, keepdims=True)
        y_ref[:, sl] = (blk * lax.rsqrt(ms + LN_EPS) * normw_ref[:, sl]).astype(y_ref.dtype)

    @pl.when(c == nc - 1)
    def _():
        stout_ref[0] = st_ref[...]


def _ssd(x2d, h_main, hist8, st0, p, bt, s, L):
    nc = s // L
    t = bt * s
    w2 = 2 * D_MODEL
    row = lambda b, c: (b * nc + c, 0)
    const2 = lambda b, c: (0, 0)
    kern = functools.partial(_ssd_kernel, L=L, nc=nc)
    return pl.pallas_call(
        kern,
        grid=(bt, nc),
        in_specs=[
            pl.BlockSpec((L, D_MODEL), row),
            pl.BlockSpec((L, w2), lambda b, c: (b * nc + c, COL_Z)),
            pl.BlockSpec((L, w2), lambda b, c: (b * nc + c, COL_XS)),
            pl.BlockSpec((L, w2), lambda b, c: (b * nc + c, COL_BC)),
            pl.BlockSpec((1, SUBLANES, SSD_CONV_DIM), lambda b, c: (b, 0, 0)),
            pl.BlockSpec((1, SSD_N_HEADS, SSD_D_STATE, SSD_HEAD_DIM), lambda b, c: (b, 0, 0, 0)),
            pl.BlockSpec((D_MODEL, LANES), const2),
            pl.BlockSpec((SSD_N_HEADS, D_MODEL), const2),
            pl.BlockSpec((SSD_CONV, SSD_CONV_DIM), const2),
            pl.BlockSpec((1, SSD_CONV_DIM), const2),
            pl.BlockSpec((1, SSD_N_HEADS), const2),
            pl.BlockSpec((SSD_N_HEADS, 1), const2),
            pl.BlockSpec((1, SSD_N_HEADS), const2),
            pl.BlockSpec((SSD_N_HEADS, 1), const2),
            pl.BlockSpec((1, SSD_D_INNER), const2),
            pl.BlockSpec((1, SSD_D_INNER), const2),
        ],
        out_specs=[
            pl.BlockSpec((L, SSD_D_INNER), row),
            pl.BlockSpec((1, SSD_N_HEADS, SSD_D_STATE, SSD_HEAD_DIM), lambda b, c: (b, 0, 0, 0)),
            pl.BlockSpec((1, SUBLANES, SSD_CONV_DIM), lambda b, c: (b, 0, 0)),
        ],
        out_shape=[
            jax.ShapeDtypeStruct((t, SSD_D_INNER), BF16),
            jax.ShapeDtypeStruct((bt, SSD_N_HEADS, SSD_D_STATE, SSD_HEAD_DIM), F32),
            jax.ShapeDtypeStruct((bt, SUBLANES, SSD_CONV_DIM), F32),
        ],
        scratch_shapes=[
            pltpu.VMEM((L + SUBLANES, SSD_CONV_DIM), F32),
            pltpu.VMEM((SSD_N_HEADS, SSD_D_STATE, SSD_HEAD_DIM), F32),
            pltpu.VMEM((L, SSD_D_INNER), F32),
        ],
        compiler_params=_cparams("arbitrary", "arbitrary"),
        name="ssd_scan",
    )(x2d, h_main, h_main, h_main, hist8, st0,
      p["w_dt"], p["w_dtT"], p["conv_w"], p["conv_b"], p["dt_b"], p["dt_bT"], p["a_log"], p["a_logT"],
      p["d_exp"], p["norm_w"])


def _attn_kernel(q_ref, k_ref, v_ref, bias_ref, sink_ref, o_ref, *, cq, n_sub, mask_prefix):
    step = pl.program_id(1)
    kw = WINDOW + cq
    scale = ATT_HEAD_DIM ** -0.5
    gw = ATT_GROUP * ATT_HEAD_DIM
    for sub in range(n_sub):
        r0 = pl.multiple_of((step * n_sub + sub) * cq, cq)
        kk = k_ref[0, pl.ds(r0, kw), :]
        vv = v_ref[0, pl.ds(r0, kw), :]
        qq = q_ref[sub * cq:(sub + 1) * cq, :]
        if mask_prefix:
            valid = (r0 + lax.broadcasted_iota(jnp.int32, (1, kw), 1)) >= WINDOW
        scs = []
        for hk in range(ATT_KV_HEADS):
            q4 = jnp.concatenate(
                [qq[:, hk * gw + g * ATT_HEAD_DIM:hk * gw + (g + 1) * ATT_HEAD_DIM] for g in range(ATT_GROUP)],
                axis=0)
            kh = kk[:, hk * ATT_HEAD_DIM:(hk + 1) * ATT_HEAD_DIM]
            sc = lax.dot_general(q4, kh, (((1,), (1,)), ((), ())), preferred_element_type=F32)
            sc = sc * scale + bias_ref[hk]
            if mask_prefix:
                sc = jnp.where(valid, sc, -1e30)
            scs.append(sc)
        es, rs = [], []
        for hk in range(ATT_KV_HEADS):
            sink = sink_ref[hk]
            mx = jnp.maximum(jnp.max(scs[hk], axis=-1, keepdims=True), sink)
            e = jnp.exp(scs[hk] - mx)
            den = jnp.sum(e, axis=-1, keepdims=True) + jnp.exp(sink - mx)
            es.append(e.astype(BF16))
            rs.append(1.0 / den)
        for hk in range(ATT_KV_HEADS):
            vh = vv[:, hk * ATT_HEAD_DIM:(hk + 1) * ATT_HEAD_DIM]
            o4 = jnp.dot(es[hk], vh, preferred_element_type=F32) * rs[hk]
            o_ref[sub * cq:(sub + 1) * cq, hk * gw:(hk + 1) * gw] = jnp.concatenate(
                [o4[g * cq:(g + 1) * cq, :] for g in range(ATT_GROUP)], axis=1).astype(o_ref.dtype)


def _attention(h_main, kall, vall, bias, sinks, bt, s, cq, n_sub, mask_prefix):
    tq = cq * n_sub
    nq = s // tq
    kw = WINDOW + cq
    bias4 = bias.reshape(ATT_KV_HEADS, ATT_GROUP * cq, kw)
    sink4 = jnp.repeat(sinks, cq).reshape(ATT_KV_HEADS, ATT_GROUP * cq, 1)
    kern = functools.partial(_attn_kernel, cq=cq, n_sub=n_sub, mask_prefix=mask_prefix)
    return pl.pallas_call(
        kern,
        grid=(bt, nq),
        in_specs=[
            pl.BlockSpec((tq, D_MODEL), lambda b, i: (b * nq + i, COL_Q)),
            pl.BlockSpec((1, WINDOW + s, ATT_KV_DIM), lambda b, i: (b, 0, 0)),
            pl.BlockSpec((1, WINDOW + s, ATT_KV_DIM), lambda b, i: (b, 0, 0)),
            pl.BlockSpec((ATT_KV_HEADS, ATT_GROUP * cq, kw), lambda b, i: (0, 0, 0)),
            pl.BlockSpec((ATT_KV_HEADS, ATT_GROUP * cq, 1), lambda b, i: (0, 0, 0)),
        ],
        out_specs=pl.BlockSpec((tq, D_MODEL), lambda b, i: (b * nq + i, 0)),
        out_shape=jax.ShapeDtypeStruct((bt * s, D_MODEL), BF16),
        compiler_params=_cparams("arbitrary", "arbitrary"),
        name="swa_attention",
    )(h_main, kall, vall, bias4, sink4)


def _merge_kernel(x_ref, ssd_ref, att_ref, scb_ref, scc_ref, sch_ref, gssd_ref, gatt_ref, gsc_ref, hist_ref,
                  wssd_ref, watt_ref, wsc_ref, wo_ref, scw_ref, lng_ref, lnb_ref, rw_ref, rb_ref,
                  x1r_ref, ir_ref, gate_ref, cnt_ref, histout_ref,
                  ext_ref, base_ref, *, tm, tiles_per_batch):
    i = pl.program_id(0)

    @pl.when(i == 0)
    def _():
        base_ref[...] = jnp.zeros_like(base_ref)

    @pl.when(i % tiles_per_batch == 0)
    def _():
        ext_ref[0:SUBLANES, :] = hist_ref[0]

    u = scc_ref[...].astype(F32) * sch_ref[...].astype(F32)
    ext_ref[SUBLANES:SUBLANES + tm, :] = u
    off = SUBLANES - (SC_WIDTH - 1)
    conv = ext_ref[off:off + tm, :] * scw_ref[0:1, :]
    for j in range(1, SC_WIDTH):
        conv = conv + ext_ref[off + j:off + j + tm, :] * scw_ref[j:j + 1, :]
    tail = ext_ref[tm:tm + SUBLANES, :]
    ext_ref[0:SUBLANES, :] = tail
    histout_ref[0] = tail
    sc_pre = (scb_ref[...].astype(F32) * conv).astype(BF16)

    y_sc = jnp.dot(sc_pre, wsc_ref[...], preferred_element_type=F32)
    y_ssd = jnp.dot(ssd_ref[...], wssd_ref[...], preferred_element_type=F32)
    y_att = jnp.dot(att_ref[...], watt_ref[...], preferred_element_type=F32)
    merged = (jax.nn.sigmoid(gssd_ref[...].astype(F32)) * y_ssd
              + jax.nn.sigmoid(gatt_ref[...].astype(F32)) * y_att
              + jax.nn.sigmoid(gsc_ref[...].astype(F32)) * y_sc)
    v = ALPHA * x_ref[...] + jnp.dot(merged.astype(BF16), wo_ref[...], preferred_element_type=F32)
    mu = jnp.mean(v, axis=-1, keepdims=True)
    vc = v - mu
    var = jnp.mean(vc * vc, axis=-1, keepdims=True)
    x1 = vc * lax.rsqrt(var + LN_EPS) * lng_ref[...] + lnb_ref[...]
    for j in range(ROW_TILES):
        x1r_ref[pl.ds(j, tm, stride=ROW_TILES), :] = x1[:, j * LANES:(j + 1) * LANES]

    lane = lax.broadcasted_iota(jnp.int32, (tm, LANES), 1)
    logits = jnp.dot(x1.astype(BF16), rw_ref[...], preferred_element_type=F32) + rb_ref[...]
    work = jnp.where(lane < N_EXPERTS, logits, -jnp.inf)
    vals, idxs = [], []
    onehot = jnp.zeros((tm, LANES), F32)
    for k in range(TOP_K):
        mv = jnp.max(work, axis=-1, keepdims=True)
        mi = jnp.min(jnp.where(work == mv, lane, LANES), axis=-1, keepdims=True)
        vals.append(mv)
        idxs.append(mi)
        work = jnp.where(lane == mi, -jnp.inf, work)
        onehot = onehot + (lane == mi + k * N_EXPERTS).astype(F32)
    es = [jnp.exp(vk - vals[0]) for vk in vals]
    den = es[0] + es[1] + es[2] + es[3]
    ri = lax.broadcasted_iota(jnp.int32, (tm, tm), 0)
    ci = lax.broadcasted_iota(jnp.int32, (tm, tm), 1)
    ltri = (ri > ci).astype(BF16)
    prefix = jnp.dot(ltri, onehot.astype(BF16), preferred_element_type=F32)
    cnt = jnp.sum(onehot, axis=0, keepdims=True)
    lane1 = lax.broadcasted_iota(jnp.int32, (1, LANES), 1)
    base = base_ref[0:1, :]
    offs = base
    tot = cnt
    for sh in range(1, TOP_K):
        rolled = pltpu.roll(cnt, sh * N_EXPERTS, 1)
        offs = offs + jnp.where(lane1 >= sh * N_EXPERTS, rolled, 0.0)
        tot = tot + rolled
    rank_all = prefix + offs
    ir = jnp.zeros((tm, LANES), jnp.int32)
    gt = jnp.zeros((tm, LANES), F32)
    for k in range(TOP_K):
        sel = lane == idxs[k] + k * N_EXPERTS
        rk = jnp.sum(jnp.where(sel, rank_all, 0.0), axis=-1, keepdims=True).astype(jnp.int32)
        ir = jnp.where(lane == k, idxs[k], ir)
        ir = jnp.where(lane == TOP_K + k, rk, ir)
        gt = jnp.where(lane == k, es[k] / den, gt)
    ir_ref[...] = ir
    gate_ref[...] = gt
    new_base = base + tot
    base_ref[...] = jnp.broadcast_to(new_base, base_ref.shape)
    cnt_ref[...] = jnp.broadcast_to(new_base, cnt_ref.shape)


def _merge(x2d, h_main, ssd_pre, att, schist8, p, bt, s, tm):
    t = bt * s
    nt = t // tm
    tiles_per_batch = s // tm
    row = lambda i: (i, 0)
    const2 = lambda i: (0, 0)
    hcol = lambda cidx: pl.BlockSpec((tm, D_MODEL), lambda i: (i, cidx))
    kern = functools.partial(_merge_kernel, tm=tm, tiles_per_batch=tiles_per_batch)
    return pl.pallas_call(
        kern,
        grid=(nt,),
        in_specs=[
            pl.BlockSpec((tm, D_MODEL), row),
            pl.BlockSpec((tm, SSD_D_INNER), row),
            pl.BlockSpec((tm, D_MODEL), row),
            hcol(COL_SCB), hcol(COL_SCC), hcol(COL_SCH), hcol(COL_GSSD), hcol(COL_GATT), hcol(COL_GSC),
            pl.BlockSpec((1, SUBLANES, D_MODEL), lambda i: (i // tiles_per_batch, 0, 0)),
            pl.BlockSpec((SSD_D_INNER, D_MODEL), const2),
            pl.BlockSpec((D_MODEL, D_MODEL), const2),
            pl.BlockSpec((D_MODEL, D_MODEL), const2),
            pl.BlockSpec((D_MODEL, D_MODEL), const2),
            pl.BlockSpec((SC_WIDTH, D_MODEL), const2),
            pl.BlockSpec((1, D_MODEL), const2),
            pl.BlockSpec((1, D_MODEL), const2),
            pl.BlockSpec((D_MODEL, LANES), const2),
            pl.BlockSpec((1, LANES), const2),
        ],
        out_specs=[
            pl.BlockSpec((tm * ROW_TILES, LANES), row),
            pl.BlockSpec((tm, LANES), row),
            pl.BlockSpec((tm, LANES), row),
            pl.BlockSpec((SUBLANES, LANES), const2),
            pl.BlockSpec((1, SUBLANES, D_MODEL), lambda i: (i // tiles_per_batch, 0, 0)),
        ],
        out_shape=[
            jax.ShapeDtypeStruct((t * ROW_TILES, LANES), F32),
            jax.ShapeDtypeStruct((t, LANES), jnp.int32),
            jax.ShapeDtypeStruct((t, LANES), F32),
            jax.ShapeDtypeStruct((SUBLANES, LANES), F32),
            jax.ShapeDtypeStruct((bt, SUBLANES, D_MODEL), F32),
        ],
        scratch_shapes=[
            pltpu.VMEM((tm + SUBLANES, D_MODEL), F32),
            pltpu.VMEM((SUBLANES, LANES), F32),
        ],
        compiler_params=_cparams("arbitrary"),
        name="merge_ln_router",
    )(x2d, ssd_pre, att, h_main, h_main, h_main, h_main, h_main, h_main, schist8,
      p["w_ssd_out"], p["w_att_out"], p["w_sc_out"], p["w_o"], p["sc_w"], p["ln1_g"], p["ln1_b"],
      p["router_w"], p["router_b"])


def _row_copy_wait(src_hbm, dst, sem, n_rows):
    pltpu.make_async_copy(src_hbm.at[pl.ds(0, n_rows * ROW_TILES), :],
                          dst.at[pl.ds(0, n_rows * ROW_TILES), :], sem).wait()


def _dispatch_kernel(pstart_ref, ir_ref, x_ref, xs_in_hbm, xs_hbm, sem, *, tm):
    del xs_in_hbm

    def issue(tk, carry):
        src = pl.multiple_of(tk * ROW_TILES, ROW_TILES)
        for k in range(TOP_K):
            dest = pstart_ref[ir_ref[tk * SUBLANES + k]] + ir_ref[tk * SUBLANES + TOP_K + k]
            dst = pl.multiple_of(dest * ROW_TILES, ROW_TILES)
            pltpu.make_async_copy(x_ref.at[pl.ds(src, ROW_TILES), :],
                                  xs_hbm.at[pl.ds(dst, ROW_TILES), :], sem).start()
        return carry

    lax.fori_loop(0, tm, issue, 0)
    for _ in range(TOP_K):
        _row_copy_wait(x_ref, xs_hbm, sem, tm)


def _dispatch(x1r, ir_flat, pstart, rows, t, tm):
    xs0 = jnp.zeros((rows * ROW_TILES, LANES), F32)
    return pl.pallas_call(
        functools.partial(_dispatch_kernel, tm=tm),
        grid_spec=pltpu.PrefetchScalarGridSpec(
            num_scalar_prefetch=1,
            grid=(t // tm,),
            in_specs=[
                pl.BlockSpec((tm * SUBLANES,), lambda i, ps: (i,), memory_space=pltpu.SMEM),
                pl.BlockSpec((tm * ROW_TILES, LANES), lambda i, ps: (i, 0)),
                pl.BlockSpec(memory_space=pl.ANY),
            ],
            out_specs=pl.BlockSpec(memory_space=pl.ANY),
            scratch_shapes=[pltpu.SemaphoreType.DMA],
        ),
        out_shape=jax.ShapeDtypeStruct((rows * ROW_TILES, LANES), F32),
        input_output_aliases={3: 0},
        compiler_params=_cparams("arbitrary"),
        name="moe_dispatch",
    )(pstart, ir_flat, x1r, xs0)


def _moe_kernel(be_ref, nv_ref, x_ref, wu_ref, bu_ref, wd_ref, bd_ref, o_ref, *, tmo):
    i = pl.program_id(0)

    @pl.when(i < nv_ref[0])
    def _():
        x = jnp.concatenate([x_ref[pl.ds(j, tmo, stride=ROW_TILES), :] for j in range(ROW_TILES)], axis=1)
        h = jnp.dot(x.astype(BF16), wu_ref[0], preferred_element_type=F32) + bu_ref[0]
        gate = jnp.minimum(h[:, :D_FF], SWIGLU_LIMIT)
        up = jnp.clip(h[:, D_FF:], -SWIGLU_LIMIT, SWIGLU_LIMIT)
        act = (up + 1.0) * gate * jax.nn.sigmoid(SWIGLU_ALPHA * gate)
        o = jnp.dot(act.astype(BF16), wd_ref[0], preferred_element_type=F32) + bd_ref[0]
        for j in range(ROW_TILES):
            o_ref[pl.ds(j, tmo, stride=ROW_TILES), :] = o[:, j * LANES:(j + 1) * LANES]

    @pl.when(i >= nv_ref[0])
    def _():
        o_ref[...] = jnp.zeros_like(o_ref)


def _moe_experts(xs, block_e, nvalid, p, n_blocks, tmo):
    return pl.pallas_call(
        functools.partial(_moe_kernel, tmo=tmo),
        grid_spec=pltpu.PrefetchScalarGridSpec(
            num_scalar_prefetch=2,
            grid=(n_blocks,),
            in_specs=[
                pl.BlockSpec((tmo * ROW_TILES, LANES), lambda i, be, nv: (i, 0)),
                pl.BlockSpec((1, D_MODEL, 2 * D_FF), lambda i, be, nv: (be[i], 0, 0)),
                pl.BlockSpec((1, 1, 2 * D_FF), lambda i, be, nv: (be[i], 0, 0)),
                pl.BlockSpec((1, D_FF, D_MODEL), lambda i, be, nv: (be[i], 0, 0)),
                pl.BlockSpec((1, 1, D_MODEL), lambda i, be, nv: (be[i], 0, 0)),
            ],
            out_specs=pl.BlockSpec((tmo * ROW_TILES, LANES), lambda i, be, nv: (i, 0)),
        ),
        out_shape=jax.ShapeDtypeStruct(xs.shape, F32),
        compiler_params=_cparams("arbitrary"),
        name="moe_experts",
    )(block_e, nvalid, xs, p["w_up"], p["b_up"], p["w_down"], p["b_down"])


def _combine_kernel(pstart_ref, ir_ref, x1r_ref, gate_ref, lng_ref, lnb_ref, eo_hbm, o_ref, buf_ref, sem, *, tm):
    def issue(tk, carry):
        for k in range(TOP_K):
            dest = pstart_ref[ir_ref[tk * SUBLANES + k]] + ir_ref[tk * SUBLANES + TOP_K + k]
            src = pl.multiple_of(dest * ROW_TILES, ROW_TILES)
            dst = pl.multiple_of(tk * ROW_TILES, ROW_TILES)
            pltpu.make_async_copy(eo_hbm.at[pl.ds(src, ROW_TILES), :],
                                  buf_ref.at[k, pl.ds(dst, ROW_TILES), :], sem).start()
        return carry

    lax.fori_loop(0, tm, issue, 0)
    for k in range(TOP_K):
        _row_copy_wait(eo_hbm, buf_ref.at[k], sem, tm)

    gates = gate_ref[...]
    vs = []
    tot = jnp.zeros((tm, 1), F32)
    for j in range(ROW_TILES):
        acc = ALPHA * x1r_ref[pl.ds(j, tm, stride=ROW_TILES), :]
        for k in range(TOP_K):
            acc = acc + buf_ref[k, pl.ds(j, tm, stride=ROW_TILES), :] * gates[:, k:k + 1]
        vs.append(acc)
        tot = tot + jnp.sum(acc, axis=-1, keepdims=True)
    mu = tot * (1.0 / D_MODEL)
    sq = jnp.zeros((tm, 1), F32)
    for j in range(ROW_TILES):
        vs[j] = vs[j] - mu
        sq = sq + jnp.sum(vs[j] * vs[j], axis=-1, keepdims=True)
    inv = lax.rsqrt(sq * (1.0 / D_MODEL) + LN_EPS)
    for j in range(ROW_TILES):
        sl = slice(j * LANES, (j + 1) * LANES)
        o_ref[:, sl] = vs[j] * inv * lng_ref[:, sl] + lnb_ref[:, sl]


def _combine(x1r, ir_flat, gates, pstart, eo, p, t, tm):
    return pl.pallas_call(
        functools.partial(_combine_kernel, tm=tm),
        grid_spec=pltpu.PrefetchScalarGridSpec(
            num_scalar_prefetch=1,
            grid=(t // tm,),
            in_specs=[
                pl.BlockSpec((tm * SUBLANES,), lambda i, ps: (i,), memory_space=pltpu.SMEM),
                pl.BlockSpec((tm * ROW_TILES, LANES), lambda i, ps: (i, 0)),
                pl.BlockSpec((tm, LANES), lambda i, ps: (i, 0)),
                pl.BlockSpec((1, D_MODEL), lambda i, ps: (0, 0)),
                pl.BlockSpec((1, D_MODEL), lambda i, ps: (0, 0)),
                pl.BlockSpec(memory_space=pl.ANY),
            ],
            out_specs=pl.BlockSpec((tm, D_MODEL), lambda i, ps: (i, 0)),
            scratch_shapes=[pltpu.VMEM((TOP_K, tm * ROW_TILES, LANES), F32), pltpu.SemaphoreType.DMA],
        ),
        out_shape=jax.ShapeDtypeStruct((t, D_MODEL), F32),
        compiler_params=_cparams("arbitrary"),
        name="moe_combine_ln",
    )(pstart, ir_flat, x1r, gates, p["ln2_g"], p["ln2_b"], eo)


def _moe(x1r, ir, gates, cnt, p, t, tm, tmo):
    n_assign = t * TOP_K
    n_blocks = (n_assign + N_EXPERTS * (tmo - 1) + tmo - 1) // tmo
    rows = n_blocks * tmo
    counts = cnt[0, :N_EXPERTS].astype(jnp.int32)
    padded = (counts + tmo - 1) // tmo * tmo
    pend = jnp.cumsum(padded)
    pstart = (pend - padded).astype(jnp.int32)
    block_start = jnp.arange(n_blocks, dtype=jnp.int32) * tmo
    block_e = jnp.minimum(jnp.sum((pend[None, :] <= block_start[:, None]).astype(jnp.int32), axis=1),
                          N_EXPERTS - 1)
    nvalid = (pend[-1:] // tmo).astype(jnp.int32)
    ir_flat = ir[:, :SUBLANES].reshape(t * SUBLANES)
    xs = _dispatch(x1r, ir_flat, pstart, rows, t, tm)
    eo = _moe_experts(xs, block_e, nvalid, p, n_blocks, tmo)
    return _combine(x1r, ir_flat, gates, pstart, eo, p, t, tm)


def _t5_bucket(rel):
    half = N_BUCKETS // 2
    max_exact = half // 2
    ret = jnp.where(rel > 0, half, 0)
    n = jnp.abs(rel)
    nf = jnp.maximum(n, 1).astype(F32)
    large = max_exact + (jnp.log(nf / max_exact) / math.log(MAX_DISTANCE / max_exact)
                         * (half - max_exact)).astype(jnp.int32)
    large = jnp.minimum(large, half - 1)
    return ret + jnp.where(n < max_exact, n, large)


def _rel_bias(table, cq):
    koff = jnp.arange(WINDOW + cq) - WINDOW
    rel = koff[None, :] - jnp.arange(cq)[:, None]
    return jnp.transpose(table[_t5_bucket(rel)].astype(F32), (2, 0, 1))


def _pad_rows_front(a, rows):
    pad = jnp.zeros(a.shape[:1] + (rows - a.shape[1],) + a.shape[2:], a.dtype)
    return jnp.concatenate([pad, a], axis=1)


def _layer_params(l, w_in, ssd_conv_w, ssd_conv_b, ssd_dt_bias, ssd_a_log, ssd_d, ssd_norm_w, w_ssd_out,
                  attn_sinks, w_attn_out, sc_conv_w, w_sc_out, w_o, ln1_g, ln1_b,
                  router_w, router_b, w_up, b_up, w_down, b_down, ln2_g, ln2_b):
    wi = w_in[l]
    w_dt = wi[:, OFF_DT:OFF_Q]
    return {
        "w_main": jnp.concatenate([wi[:, :OFF_DT], wi[:, OFF_Q:OFF_K], wi[:, OFF_SCB:]], axis=1).astype(BF16),
        "w_kv": wi[:, OFF_K:OFF_SCB].astype(BF16),
        "w_dt": jnp.pad(w_dt, ((0, 0), (0, LANES - SSD_N_HEADS))).astype(BF16),
        "w_dtT": w_dt.T.astype(BF16),
        "conv_w": ssd_conv_w[l],
        "conv_b": ssd_conv_b[l][None, :],
        "dt_b": ssd_dt_bias[l][None, :],
        "dt_bT": ssd_dt_bias[l][:, None],
        "a_log": ssd_a_log[l][None, :],
        "a_logT": ssd_a_log[l][:, None],
        "d_exp": jnp.repeat(ssd_d[l], SSD_HEAD_DIM)[None, :],
        "norm_w": ssd_norm_w[l][None, :],
        "w_ssd_out": w_ssd_out[l].astype(BF16),
        "sinks": attn_sinks[l],
        "w_att_out": w_attn_out[l].astype(BF16),
        "sc_w": sc_conv_w[l],
        "w_sc_out": w_sc_out[l].astype(BF16),
        "w_o": w_o[l].astype(BF16),
        "ln1_g": ln1_g[l][None, :],
        "ln1_b": ln1_b[l][None, :],
        "router_w": jnp.pad(router_w[l], ((0, 0), (0, LANES - N_EXPERTS))).astype(BF16),
        "router_b": jnp.pad(router_b[l], (0, LANES - N_EXPERTS))[None, :],
        "w_up": w_up[l].astype(BF16),
        "b_up": b_up[l][:, None, :],
        "w_down": w_down[l].astype(BF16),
        "b_down": b_down[l][:, None, :],
        "ln2_g": ln2_g[l][None, :],
        "ln2_b": ln2_b[l][None, :],
    }


def _trunk_layer(x2d, bt, s, p, bias, ssd_hist, ssd_state, sc_hist, kv_cache, cfg):
    t = bt * s
    h_main = _matmul(x2d, p["w_main"], BF16, cfg["tm_in"], cfg["tn_in"])
    h_kv = _matmul(x2d, p["w_kv"], F32, cfg["tm_in"], 2 * ATT_KV_DIM)
    k_new = h_kv[:, :ATT_KV_DIM].reshape(bt, s, ATT_KV_DIM)
    v_new = h_kv[:, ATT_KV_DIM:].reshape(bt, s, ATT_KV_DIM)

    if kv_cache is None:
        zpad = jnp.zeros((bt, WINDOW, ATT_KV_DIM), BF16)
        kall = jnp.concatenate([zpad, k_new.astype(BF16)], axis=1)
        vall = jnp.concatenate([zpad, v_new.astype(BF16)], axis=1)
    else:
        kall = jnp.concatenate([kv_cache[0].astype(BF16), k_new.astype(BF16)], axis=1)
        vall = jnp.concatenate([kv_cache[1].astype(BF16), v_new.astype(BF16)], axis=1)
    att = _attention(h_main, kall, vall, bias, p["sinks"], bt, s, cfg["cq"], cfg["n_sub"], kv_cache is None)

    if ssd_hist is None:
        hist8 = jnp.zeros((bt, SUBLANES, SSD_CONV_DIM), F32)
        st0 = jnp.zeros((bt, SSD_N_HEADS, SSD_D_STATE, SSD_HEAD_DIM), F32)
        schist8 = jnp.zeros((bt, SUBLANES, D_MODEL), F32)
    else:
        hist8 = _pad_rows_front(ssd_hist, SUBLANES)
        st0 = jnp.swapaxes(ssd_state, 2, 3)
        schist8 = _pad_rows_front(sc_hist, SUBLANES)
    ssd_pre, st_out, hist_out = _ssd(x2d, h_main, hist8, st0, p, bt, s, cfg["cq"])

    x1r, ir, gates, cnt, schist_out = _merge(x2d, h_main, ssd_pre, att, schist8, p, bt, s, cfg["tm_merge"])
    x2 = _moe(x1r, ir, gates, cnt, p, t, cfg["tm_moe"], cfg["tmo"])

    n_keep = min(s, WINDOW)
    new_k = k_new[:, s - n_keep:].reshape(bt, n_keep, ATT_KV_HEADS, ATT_HEAD_DIM)
    new_v = v_new[:, s - n_keep:].reshape(bt, n_keep, ATT_KV_HEADS, ATT_HEAD_DIM)
    new_state = jnp.swapaxes(st_out, 2, 3)
    new_hist = hist_out[:, SUBLANES - (SSD_CONV - 1):]
    new_sc_hist = schist_out[:, SUBLANES - (SC_WIDTH - 1):]
    return x2, new_k, new_v, new_state, new_hist, new_sc_hist


def _config(bt, s, prompt):
    t = bt * s
    if prompt:
        return {"tm_in": min(t, 1024), "tn_in": 1024, "cq": CHUNK, "n_sub": min(4, s // CHUNK),
                "tm_merge": min(s, 256), "tm_moe": min(t, 256), "tmo": min(512, max(64, t // 32))}
    return {"tm_in": min(t, 256), "tn_in": 1024, "cq": s, "n_sub": 1,
            "tm_merge": s, "tm_moe": min(t, 256), "tmo": 64}


def _forward(x_prompt, x_sample, cache_attn_k, cache_attn_v, state_ssd, state_ssd_conv, state_short_conv,
             rel_bias, layer_weights):
    bp, sp, _ = x_prompt.shape
    bd, sd, _ = x_sample.shape
    cfg_p = _config(bp, sp, True)
    cfg_d = _config(bd, sd, False)
    bias_p = _rel_bias(rel_bias, cfg_p["cq"])
    bias_d = _rel_bias(rel_bias, cfg_d["cq"])
    yp = x_prompt.reshape(bp * sp, D_MODEL)
    ys = x_sample.reshape(bd * sd, D_MODEL)
    outs_p, outs_d = [], []
    for l in range(DEPTH):
        p = _layer_params(l, *layer_weights)
        yp, *rest = _trunk_layer(yp, bp, sp, p, bias_p, None, None, None, None, cfg_p)
        outs_p.append(rest)
        cache = (cache_attn_k[l].reshape(bd, WINDOW, ATT_KV_DIM), cache_attn_v[l].reshape(bd, WINDOW, ATT_KV_DIM))
        ys, *rest = _trunk_layer(ys, bd, sd, p, bias_d, state_ssd_conv[l], state_ssd[l], state_short_conv[l],
                                 cache, cfg_d)
        outs_d.append(rest)
    stack = lambda outs, i: jnp.stack([o[i] for o in outs])
    return (yp.reshape(bp, sp, D_MODEL), ys.reshape(bd, sd, D_MODEL),
            stack(outs_p, 0), stack(outs_p, 1), stack(outs_p, 2), stack(outs_p, 3), stack(outs_p, 4),
            stack(outs_d, 0), stack(outs_d, 1), stack(outs_d, 2), stack(outs_d, 3), stack(outs_d, 4))


def kernel(x_prompt, x_sample, cache_attn_k, cache_attn_v, state_ssd, state_ssd_conv, state_short_conv, w_in, ssd_conv_w, ssd_conv_b, ssd_dt_bias, ssd_a_log, ssd_d, ssd_norm_w, w_ssd_out, attn_sinks, w_attn_out, rel_bias, sc_conv_w, w_sc_out, w_o, ln1_g, ln1_b, router_w, router_b, w_up, b_up, w_down, b_down, ln2_g, ln2_b):
    layer_weights = (w_in, ssd_conv_w, ssd_conv_b, ssd_dt_bias, ssd_a_log, ssd_d, ssd_norm_w, w_ssd_out,
                     attn_sinks, w_attn_out, sc_conv_w, w_sc_out, w_o, ln1_g, ln1_b,
                     router_w, router_b, w_up, b_up, w_down, b_down, ln2_g, ln2_b)
    return _forward(x_prompt, x_sample, cache_attn_k, cache_attn_v, state_ssd, state_ssd_conv, state_short_conv,
                    rel_bias, layer_weights)
```

```python
import functools
import math

import jax
import jax.numpy as jnp
import numpy as np
from jax import lax
from jax.experimental import pallas as pl
from jax.experimental.pallas import tpu as pltpu

F32 = jnp.float32
BF16 = jnp.bfloat16

D_MODEL = 1024
DEPTH = 2
CHUNK = 64
SSD_D_INNER = 2 * D_MODEL
SSD_HEAD_DIM = 64
SSD_N_HEADS = SSD_D_INNER // SSD_HEAD_DIM
SSD_N_GROUPS = 8
SSD_D_STATE = 128
SSD_CONV = 4
SSD_GN = SSD_N_GROUPS * SSD_D_STATE
SSD_CONV_DIM = SSD_D_INNER + 2 * SSD_GN
SSD_HEADS_PER_GROUP = SSD_N_HEADS // SSD_N_GROUPS
SSD_NORM_GROUP = SSD_D_INNER // SSD_N_GROUPS
ATT_HEAD_DIM = 64
ATT_HEADS = D_MODEL // ATT_HEAD_DIM
ATT_KV_HEADS = ATT_HEADS // 4
ATT_GROUP = ATT_HEADS // ATT_KV_HEADS
ATT_KV_DIM = ATT_KV_HEADS * ATT_HEAD_DIM
WINDOW = 128
N_BUCKETS = 32
MAX_DISTANCE = 128
SC_WIDTH = 3
N_EXPERTS = 32
TOP_K = 4
D_FF = D_MODEL
SWIGLU_LIMIT = 7.0
SWIGLU_ALPHA = 1.702
LN_EPS = 1e-5
ALPHA = (2.0 * DEPTH) ** 0.25

LANES = 128
SUBLANES = 8
ROW_TILES = D_MODEL // LANES
VMEM_LIMIT = 56 * 1024 * 1024
ZERO_ROWS = 16

COL_Z, COL_XS, COL_BC = 0, 1, 2
COL_Q, COL_SCB, COL_SCC, COL_SCH, COL_GSSD, COL_GATT, COL_GSC = 6, 7, 8, 9, 10, 11, 12
N_MAIN = 13 * D_MODEL
OFF_DT = SSD_D_INNER + SSD_CONV_DIM
OFF_Q = OFF_DT + SSD_N_HEADS
OFF_K = OFF_Q + D_MODEL
OFF_SCB = OFF_K + 2 * ATT_KV_DIM


def _cparams(*sem):
    return pltpu.CompilerParams(dimension_semantics=sem, vmem_limit_bytes=VMEM_LIMIT)


def _matmul_kernel(x_ref, w_ref, o_ref, xb_ref):
    @pl.when(pl.program_id(1) == 0)
    def _():
        xb_ref[...] = x_ref[...].astype(BF16)

    o_ref[...] = jnp.dot(xb_ref[...], w_ref[...], preferred_element_type=F32).astype(o_ref.dtype)


def _matmul(x, w, out_dtype, tm, tn):
    t, k = x.shape
    n = w.shape[1]
    return pl.pallas_call(
        _matmul_kernel,
        grid=(t // tm, n // tn),
        in_specs=[pl.BlockSpec((tm, k), lambda i, j: (i, 0)),
                  pl.BlockSpec((k, tn), lambda i, j: (0, j))],
        out_specs=pl.BlockSpec((tm, tn), lambda i, j: (i, j)),
        out_shape=jax.ShapeDtypeStruct((t, n), out_dtype),
        scratch_shapes=[pltpu.VMEM((tm, k), BF16)],
        compiler_params=_cparams("arbitrary", "arbitrary"),
        name="in_proj",
    )(x, w)


def _ssd_kernel(x_ref, z_ref, xs_ref, bc_ref, hist_ref, st0_ref,
                wdt_ref, wdtT_ref, convw_ref, convb_ref, dtb_ref, dtbT_ref, alog_ref, alogT_ref,
                dexp_ref, normw_ref,
                y_ref, stout_ref, histout_ref,
                ext_ref, st_ref, ybuf_ref, *, L, nc):
    c = pl.program_id(1)

    @pl.when(c == 0)
    def _():
        ext_ref[0:SUBLANES, :] = hist_ref[0]
        st_ref[...] = st0_ref[0]

    xbc = jnp.concatenate([xs_ref[...], bc_ref[...]], axis=1).astype(F32)
    ext_ref[SUBLANES:SUBLANES + L, :] = xbc
    base = SUBLANES - (SSD_CONV - 1)
    conv = convb_ref[...]
    for j in range(SSD_CONV):
        conv = conv + ext_ref[base + j:base + j + L, :] * convw_ref[j:j + 1, :]
    tail = ext_ref[L:L + SUBLANES, :]
    ext_ref[0:SUBLANES, :] = tail
    histout_ref[0] = tail
    xc = conv * jax.nn.sigmoid(conv)
    xs_c = xc[:, :SSD_D_INNER]
    b_m = xc[:, SSD_D_INNER:SSD_D_INNER + SSD_GN]
    c_m = xc[:, SSD_D_INNER + SSD_GN:]

    xb = x_ref[...].astype(BF16)
    dtr = jnp.dot(xb, wdt_ref[...], preferred_element_type=F32)[:, :SSD_N_HEADS]
    dtrT = lax.dot_general(wdtT_ref[...], xb, (((1,), (1,)), ((), ())),
                           preferred_element_type=F32)

    def softplus(v):
        return jnp.maximum(v, 0.0) + jnp.log1p(jnp.exp(-jnp.abs(v)))

    dt = softplus(dtr + dtb_ref[...])
    dtT = softplus(dtrT + dtbT_ref[...])
    adt = dt * (-jnp.exp(alog_ref[...]))
    adtT = dtT * (-jnp.exp(alogT_ref[...]))
    ri = lax.broadcasted_iota(jnp.int32, (L, L), 0)
    ci = lax.broadcasted_iota(jnp.int32, (L, L), 1)
    tri = ri >= ci
    acs = jnp.dot(tri.astype(F32), adt, precision=lax.Precision.HIGHEST,
                  preferred_element_type=F32)
    acsT = jnp.dot(adtT, (ri <= ci).astype(F32), precision=lax.Precision.HIGHEST,
                   preferred_element_type=F32)
    totT = jnp.sum(adtT, axis=1, keepdims=True)
    eacs = jnp.exp(acs)
    wT = jnp.exp(totT - acsT) * dtT
    cdT = jnp.exp(totT)

    for g in range(SSD_N_GROUPS):
        bg = b_m[:, g * SSD_D_STATE:(g + 1) * SSD_D_STATE]
        cg_b = c_m[:, g * SSD_D_STATE:(g + 1) * SSD_D_STATE].astype(BF16)
        cb = lax.dot_general(cg_b, bg.astype(BF16), (((1,), (1,)), ((), ())),
                             preferred_element_type=F32)
        if L < LANES:
            bg = jnp.concatenate([bg, jnp.zeros((LANES - L, SSD_D_STATE), F32)], axis=0)
        bgT = bg.T[:, :L]
        for r in range(SSD_HEADS_PER_GROUP):
            h = g * SSD_HEADS_PER_GROUP + r
            col = acs[:, h:h + 1]
            row = acsT[h:h + 1, :]
            decay = jnp.exp(jnp.where(tri, col - row, -jnp.inf))
            m = (cb * decay * dtT[h:h + 1, :]).astype(BF16)
            xh_b = xs_c[:, h * SSD_HEAD_DIM:(h + 1) * SSD_HEAD_DIM].astype(BF16)
            st = st_ref[h]
            yd = jnp.dot(m, xh_b, preferred_element_type=F32)
            yo = jnp.dot(cg_b, st.astype(BF16), preferred_element_type=F32)
            ybuf_ref[:, h * SSD_HEAD_DIM:(h + 1) * SSD_HEAD_DIM] = yd + yo * eacs[:, h:h + 1]
            wh = (bgT * wT[h:h + 1, :]).astype(BF16)
            stc = jnp.dot(wh, xh_b, preferred_element_type=F32)
            st_ref[h] = st * cdT[h:h + 1, :] + stc

    y = ybuf_ref[...] + dexp_ref[...] * xs_c
    z = z_ref[...].astype(F32)
    y = y * (z * jax.nn.sigmoid(z))
    for g in range(SSD_N_GROUPS):
        sl = slice(g * SSD_NORM_GROUP, (g + 1) * SSD_NORM_GROUP)
        blk = y[:, sl]
        ms = jnp.mean(blk * blk, axis=-1, keepdims=True)
        y_ref[:, sl] = (blk * lax.rsqrt(ms + LN_EPS) * normw_ref[:, sl]).astype(y_ref.dtype)

    @pl.when(c == nc - 1)
    def _():
        stout_ref[0] = st_ref[...]


def _ssd(x2d, h_main, hist8, st0, p, bt, s, L):
    nc = s // L
    t = bt * s
    w2 = 2 * D_MODEL
    row = lambda b, c: (b * nc + c, 0)
    const2 = lambda b, c: (0, 0)
    kern = functools.partial(_ssd_kernel, L=L, nc=nc)
    return pl.pallas_call(
        kern,
        grid=(bt, nc),
        in_specs=[
            pl.BlockSpec((L, D_MODEL), row),
            pl.BlockSpec((L, w2), lambda b, c: (b * nc + c, COL_Z)),
            pl.BlockSpec((L, w2), lambda b, c: (b * nc + c, COL_XS)),
            pl.BlockSpec((L, w2), lambda b, c: (b * nc + c, COL_BC)),
            pl.BlockSpec((1, SUBLANES, SSD_CONV_DIM), lambda b, c: (b, 0, 0)),
            pl.BlockSpec((1, SSD_N_HEADS, SSD_D_STATE, SSD_HEAD_DIM), lambda b, c: (b, 0, 0, 0)),
            pl.BlockSpec((D_MODEL, LANES), const2),
            pl.BlockSpec((SSD_N_HEADS, D_MODEL), const2),
            pl.BlockSpec((SSD_CONV, SSD_CONV_DIM), const2),
            pl.BlockSpec((1, SSD_CONV_DIM), const2),
            pl.BlockSpec((1, SSD_N_HEADS), const2),
            pl.BlockSpec((SSD_N_HEADS, 1), const2),
            pl.BlockSpec((1, SSD_N_HEADS), const2),
            pl.BlockSpec((SSD_N_HEADS, 1), const2),
            pl.BlockSpec((1, SSD_D_INNER), const2),
            pl.BlockSpec((1, SSD_D_INNER), const2),
        ],
        out_specs=[
            pl.BlockSpec((L, SSD_D_INNER), row),
            pl.BlockSpec((1, SSD_N_HEADS, SSD_D_STATE, SSD_HEAD_DIM), lambda b, c: (b, 0, 0, 0)),
            pl.BlockSpec((1, SUBLANES, SSD_CONV_DIM), lambda b, c: (b, 0, 0)),
        ],
        out_shape=[
            jax.ShapeDtypeStruct((t, SSD_D_INNER), BF16),
            jax.ShapeDtypeStruct((bt, SSD_N_HEADS, SSD_D_STATE, SSD_HEAD_DIM), F32),
            jax.ShapeDtypeStruct((bt, SUBLANES, SSD_CONV_DIM), F32),
        ],
        scratch_shapes=[
            pltpu.VMEM((L + SUBLANES, SSD_CONV_DIM), F32),
            pltpu.VMEM((SSD_N_HEADS, SSD_D_STATE, SSD_HEAD_DIM), F32),
            pltpu.VMEM((L, SSD_D_INNER), F32),
        ],
        compiler_params=_cparams("arbitrary", "arbitrary"),
        name="ssd_scan",
    )(x2d, h_main, h_main, h_main, hist8, st0,
      p["w_dt"], p["w_dtT"], p["conv_w"], p["conv_b"], p["dt_b"], p["dt_bT"], p["a_log"], p["a_logT"],
      p["d_exp"], p["norm_w"])


def _attn_kernel(q_ref, k_ref, v_ref, bias_ref, sink_ref, o_ref, *, cq, n_sub, mask_prefix):
    step = pl.program_id(1)
    kw = WINDOW + cq
    scale = ATT_HEAD_DIM ** -0.5
    gw = ATT_GROUP * ATT_HEAD_DIM
    for sub in range(n_sub):
        r0 = pl.multiple_of((step * n_sub + sub) * cq, cq)
        kk = k_ref[0, pl.ds(r0, kw), :]
        vv = v_ref[0, pl.ds(r0, kw), :]
        qq = q_ref[sub * cq:(sub + 1) * cq, :]
        if mask_prefix:
            valid = (r0 + lax.broadcasted_iota(jnp.int32, (1, kw), 1)) >= WINDOW
        scs = []
        for hk in range(ATT_KV_HEADS):
            q4 = jnp.concatenate(
                [qq[:, hk * gw + g * ATT_HEAD_DIM:hk * gw + (g + 1) * ATT_HEAD_DIM] for g in range(ATT_GROUP)],
                axis=0)
            kh = kk[:, hk * ATT_HEAD_DIM:(hk + 1) * ATT_HEAD_DIM]
            sc = lax.dot_general(q4, kh, (((1,), (1,)), ((), ())), preferred_element_type=F32)
            sc = sc * scale + bias_ref[hk]
            if mask_prefix:
                sc = jnp.where(valid, sc, -1e30)
            scs.append(sc)
        es, rs = [], []
        for hk in range(ATT_KV_HEADS):
            sink = sink_ref[hk]
            mx = jnp.maximum(jnp.max(scs[hk], axis=-1, keepdims=True), sink)
            e = jnp.exp(scs[hk] - mx)
            den = jnp.sum(e, axis=-1, keepdims=True) + jnp.exp(sink - mx)
            es.append(e.astype(BF16))
            rs.append(1.0 / den)
        for hk in range(ATT_KV_HEADS):
            vh = vv[:, hk * ATT_HEAD_DIM:(hk + 1) * ATT_HEAD_DIM]
            o4 = jnp.dot(es[hk], vh, preferred_element_type=F32) * rs[hk]
            o_ref[sub * cq:(sub + 1) * cq, hk * gw:(hk + 1) * gw] = jnp.concatenate(
                [o4[g * cq:(g + 1) * cq, :] for g in range(ATT_GROUP)], axis=1).astype(o_ref.dtype)


def _attention(h_main, kall, vall, bias, sinks, bt, s, cq, n_sub, mask_prefix):
    tq = cq * n_sub
    nq = s // tq
    kw = WINDOW + cq
    bias4 = bias.reshape(ATT_KV_HEADS, ATT_GROUP * cq, kw)
    sink4 = jnp.repeat(sinks, cq).reshape(ATT_KV_HEADS, ATT_GROUP * cq, 1)
    kern = functools.partial(_attn_kernel, cq=cq, n_sub=n_sub, mask_prefix=mask_prefix)
    return pl.pallas_call(
        kern,
        grid=(bt, nq),
        in_specs=[
            pl.BlockSpec((tq, D_MODEL), lambda b, i: (b * nq + i, COL_Q)),
            pl.BlockSpec((1, WINDOW + s, ATT_KV_DIM), lambda b, i: (b, 0, 0)),
            pl.BlockSpec((1, WINDOW + s, ATT_KV_DIM), lambda b, i: (b, 0, 0)),
            pl.BlockSpec((ATT_KV_HEADS, ATT_GROUP * cq, kw), lambda b, i: (0, 0, 0)),
            pl.BlockSpec((ATT_KV_HEADS, ATT_GROUP * cq, 1), lambda b, i: (0, 0, 0)),
        ],
        out_specs=pl.BlockSpec((tq, D_MODEL), lambda b, i: (b * nq + i, 0)),
        out_shape=jax.ShapeDtypeStruct((bt * s, D_MODEL), BF16),
        compiler_params=_cparams("arbitrary", "arbitrary"),
        name="swa_attention",
    )(h_main, kall, vall, bias4, sink4)


def _merge_kernel(x_ref, ssd_ref, att_ref, scb_ref, scc_ref, sch_ref, gssd_ref, gatt_ref, gsc_ref, hist_ref,
                  wssd_ref, watt_ref, wsc_ref, wo_ref, scw_ref, lng_ref, lnb_ref, rw_ref, rb_ref,
                  x1r_ref, ir_ref, gate_ref, cnt_ref, histout_ref,
                  ext_ref, base_ref, *, tm, tiles_per_batch):
    i = pl.program_id(0)

    @pl.when(i == 0)
    def _():
        base_ref[...] = jnp.zeros_like(base_ref)

    @pl.when(i % tiles_per_batch == 0)
    def _():
        ext_ref[0:SUBLANES, :] = hist_ref[0]

    u = scc_ref[...].astype(F32) * sch_ref[...].astype(F32)
    ext_ref[SUBLANES:SUBLANES + tm, :] = u
    off = SUBLANES - (SC_WIDTH - 1)
    conv = ext_ref[off:off + tm, :] * scw_ref[0:1, :]
    for j in range(1, SC_WIDTH):
        conv = conv + ext_ref[off + j:off + j + tm, :] * scw_ref[j:j + 1, :]
    tail = ext_ref[tm:tm + SUBLANES, :]
    ext_ref[0:SUBLANES, :] = tail
    histout_ref[0] = tail
    sc_pre = (scb_ref[...].astype(F32) * conv).astype(BF16)

    y_sc = jnp.dot(sc_pre, wsc_ref[...], preferred_element_type=F32)
    y_ssd = jnp.dot(ssd_ref[...], wssd_ref[...], preferred_element_type=F32)
    y_att = jnp.dot(att_ref[...], watt_ref[...], preferred_element_type=F32)
    merged = (jax.nn.sigmoid(gssd_ref[...].astype(F32)) * y_ssd
              + jax.nn.sigmoid(gatt_ref[...].astype(F32)) * y_att
              + jax.nn.sigmoid(gsc_ref[...].astype(F32)) * y_sc)
    v = ALPHA * x_ref[...] + jnp.dot(merged.astype(BF16), wo_ref[...], preferred_element_type=F32)
    mu = jnp.mean(v, axis=-1, keepdims=True)
    vc = v - mu
    var = jnp.mean(vc * vc, axis=-1, keepdims=True)
    x1 = vc * lax.rsqrt(var + LN_EPS) * lng_ref[...] + lnb_ref[...]
    for j in range(ROW_TILES):
        x1r_ref[pl.ds(j, tm, stride=ROW_TILES), :] = x1[:, j * LANES:(j + 1) * LANES]

    lane = lax.broadcasted_iota(jnp.int32, (tm, LANES), 1)
    logits = jnp.dot(x1.astype(BF16), rw_ref[...], preferred_element_type=F32) + rb_ref[...]
    work = jnp.where(lane < N_EXPERTS, logits, -jnp.inf)
    vals, idxs = [], []
    onehot = jnp.zeros((tm, LANES), F32)
    for k in range(TOP_K):
        mv = jnp.max(work, axis=-1, keepdims=True)
        mi = jnp.min(jnp.where(work == mv, lane, LANES), axis=-1, keepdims=True)
        vals.append(mv)
        idxs.append(mi)
        work = jnp.where(lane == mi, -jnp.inf, work)
        onehot = onehot + (lane == mi + k * N_EXPERTS).astype(F32)
    es = [jnp.exp(vk - vals[0]) for vk in vals]
    den = es[0] + es[1] + es[2] + es[3]
    ri = lax.broadcasted_iota(jnp.int32, (tm, tm), 0)
    ci = lax.broadcasted_iota(jnp.int32, (tm, tm), 1)
    ltri = (ri > ci).astype(BF16)
    prefix = jnp.dot(ltri, onehot.astype(BF16), preferred_element_type=F32)
    cnt = jnp.sum(onehot, axis=0, keepdims=True)
    lane1 = lax.broadcasted_iota(jnp.int32, (1, LANES), 1)
    base = base_ref[0:1, :]
    offs = base
    tot = cnt
    for sh in range(1, TOP_K):
        rolled = pltpu.roll(cnt, sh * N_EXPERTS, 1)
        offs = offs + jnp.where(lane1 >= sh * N_EXPERTS, rolled, 0.0)
        tot = tot + rolled
    rank_all = prefix + offs
    ir = jnp.zeros((tm, LANES), jnp.int32)
    gt = jnp.zeros((tm, LANES), F32)
    for k in range(TOP_K):
        sel = lane == idxs[k] + k * N_EXPERTS
        rk = jnp.sum(jnp.where(sel, rank_all, 0.0), axis=-1, keepdims=True).astype(jnp.int32)
        ir = jnp.where(lane == k, idxs[k], ir)
        ir = jnp.where(lane == TOP_K + k, rk, ir)
        gt = jnp.where(lane == k, es[k] / den, gt)
    ir_ref[...] = ir
    gate_ref[...] = gt
    new_base = base + tot
    base_ref[...] = jnp.broadcast_to(new_base, base_ref.shape)
    cnt_ref[...] = jnp.broadcast_to(new_base, cnt_ref.shape)


def _merge(x2d, h_main, ssd_pre, att, schist8, p, bt, s, tm):
    t = bt * s
    nt = t // tm
    tiles_per_batch = s // tm
    row = lambda i: (i, 0)
    const2 = lambda i: (0, 0)
    hcol = lambda cidx: pl.BlockSpec((tm, D_MODEL), lambda i: (i, cidx))
    kern = functools.partial(_merge_kernel, tm=tm, tiles_per_batch=tiles_per_batch)
    return pl.pallas_call(
        kern,
        grid=(nt,),
        in_specs=[
            pl.BlockSpec((tm, D_MODEL), row),
            pl.BlockSpec((tm, SSD_D_INNER), row),
            pl.BlockSpec((tm, D_MODEL), row),
            hcol(COL_SCB), hcol(COL_SCC), hcol(COL_SCH), hcol(COL_GSSD), hcol(COL_GATT), hcol(COL_GSC),
            pl.BlockSpec((1, SUBLANES, D_MODEL), lambda i: (i // tiles_per_batch, 0, 0)),
            pl.BlockSpec((SSD_D_INNER, D_MODEL), const2),
            pl.BlockSpec((D_MODEL, D_MODEL), const2),
            pl.BlockSpec((D_MODEL, D_MODEL), const2),
            pl.BlockSpec((D_MODEL, D_MODEL), const2),
            pl.BlockSpec((SC_WIDTH, D_MODEL), const2),
            pl.BlockSpec((1, D_MODEL), const2),
            pl.BlockSpec((1, D_MODEL), const2),
            pl.BlockSpec((D_MODEL, LANES), const2),
            pl.BlockSpec((1, LANES), const2),
        ],
        out_specs=[
            pl.BlockSpec((tm * ROW_TILES, LANES), row),
            pl.BlockSpec((tm, LANES), row),
            pl.BlockSpec((tm, LANES), row),
            pl.BlockSpec((SUBLANES, LANES), const2),
            pl.BlockSpec((1, SUBLANES, D_MODEL), lambda i: (i // tiles_per_batch, 0, 0)),
        ],
        out_shape=[
            jax.ShapeDtypeStruct((t * ROW_TILES, LANES), F32),
            jax.ShapeDtypeStruct((t, LANES), jnp.int32),
            jax.ShapeDtypeStruct((t, LANES), F32),
            jax.ShapeDtypeStruct((SUBLANES, LANES), F32),
            jax.ShapeDtypeStruct((bt, SUBLANES, D_MODEL), F32),
        ],
        scratch_shapes=[
            pltpu.VMEM((tm + SUBLANES, D_MODEL), F32),
            pltpu.VMEM((SUBLANES, LANES), F32),
        ],
        compiler_params=_cparams("arbitrary"),
        name="merge_ln_router",
    )(x2d, ssd_pre, att, h_main, h_main, h_main, h_main, h_main, h_main, schist8,
      p["w_ssd_out"], p["w_att_out"], p["w_sc_out"], p["w_o"], p["sc_w"], p["ln1_g"], p["ln1_b"],
      p["router_w"], p["router_b"])


def _row_copy_wait(src_hbm, dst, sem, n_rows):
    pltpu.make_async_copy(src_hbm.at[pl.ds(0, n_rows * ROW_TILES), :],
                          dst.at[pl.ds(0, n_rows * ROW_TILES), :], sem).wait()


def _dispatch_kernel(pstart_ref, pad_ref, ir_ref, x_ref, xs_hbm, zero_ref, sem, zsem, *, tm):
    @pl.when(pl.program_id(0) == 0)
    def _():
        zero_ref[...] = jnp.zeros_like(zero_ref)

        def zero_copy(row, n_rows):
            dst = pl.multiple_of(row * ROW_TILES, ROW_TILES)
            return pltpu.make_async_copy(zero_ref.at[pl.ds(0, n_rows * ROW_TILES), :],
                                         xs_hbm.at[pl.ds(dst, n_rows * ROW_TILES), :], zsem)

        def per_expert(start):
            def body(e, carry):
                first = pad_ref[e]
                n = pad_ref[N_EXPERTS + 1 + e]
                n_big = n // ZERO_ROWS

                def big(j, c):
                    cp = zero_copy(first + j * ZERO_ROWS, ZERO_ROWS)
                    cp.start() if start else cp.wait()
                    return c

                def small(j, c):
                    cp = zero_copy(first + j, 1)
                    cp.start() if start else cp.wait()
                    return c

                lax.fori_loop(0, n_big, big, 0)
                lax.fori_loop(n_big * ZERO_ROWS, n, small, 0)
                return carry

            lax.fori_loop(0, N_EXPERTS + 1, body, 0)

        per_expert(True)
        per_expert(False)

    def issue(tk, carry):
        src = pl.multiple_of(tk * ROW_TILES, ROW_TILES)
        for k in range(TOP_K):
            dest = pstart_ref[ir_ref[tk * SUBLANES + k]] + ir_ref[tk * SUBLANES + TOP_K + k]
            dst = pl.multiple_of(dest * ROW_TILES, ROW_TILES)
            pltpu.make_async_copy(x_ref.at[pl.ds(src, ROW_TILES), :],
                                  xs_hbm.at[pl.ds(dst, ROW_TILES), :], sem).start()
        return carry

    lax.fori_loop(0, tm, issue, 0)
    for _ in range(TOP_K):
        _row_copy_wait(x_ref, xs_hbm, sem, tm)


def _dispatch(x1r, ir_flat, pstart, pad_info, rows, t, tm):
    return pl.pallas_call(
        functools.partial(_dispatch_kernel, tm=tm),
        grid_spec=pltpu.PrefetchScalarGridSpec(
            num_scalar_prefetch=2,
            grid=(t // tm,),
            in_specs=[
                pl.BlockSpec((tm * SUBLANES,), lambda i, ps, pd: (i,), memory_space=pltpu.SMEM),
                pl.BlockSpec((tm * ROW_TILES, LANES), lambda i, ps, pd: (i, 0)),
            ],
            out_specs=pl.BlockSpec(memory_space=pl.ANY),
            scratch_shapes=[pltpu.VMEM((ZERO_ROWS * ROW_TILES, LANES), F32),
                            pltpu.SemaphoreType.DMA, pltpu.SemaphoreType.DMA],
        ),
        out_shape=jax.ShapeDtypeStruct((rows * ROW_TILES, LANES), F32),
        compiler_params=_cparams("arbitrary"),
        name="moe_dispatch",
    )(pstart, pad_info, ir_flat, x1r)


def _moe_kernel(be_ref, nv_ref, x_ref, wu_ref, bu_ref, wd_ref, bd_ref, o_ref, wub_ref, wdb_ref, *, tmo):
    i = pl.program_id(0)
    valid = i < nv_ref[0]
    new_expert = jnp.logical_or(i == 0, be_ref[i] != be_ref[jnp.maximum(i - 1, 0)])

    @pl.when(jnp.logical_and(valid, new_expert))
    def _():
        wub_ref[...] = wu_ref[0, 0].astype(BF16)
        wdb_ref[...] = wd_ref[0, 0].astype(BF16)

    @pl.when(valid)
    def _():
        x = jnp.concatenate([x_ref[pl.ds(j, tmo, stride=ROW_TILES), :] for j in range(ROW_TILES)], axis=1)
        h = jnp.dot(x.astype(BF16), wub_ref[...], preferred_element_type=F32) + bu_ref[0, 0]
        gate = jnp.minimum(h[:, :D_FF], SWIGLU_LIMIT)
        up = jnp.clip(h[:, D_FF:], -SWIGLU_LIMIT, SWIGLU_LIMIT)
        act = (up + 1.0) * gate * jax.nn.sigmoid(SWIGLU_ALPHA * gate)
        o = jnp.dot(act.astype(BF16), wdb_ref[...], preferred_element_type=F32) + bd_ref[0, 0]
        for j in range(ROW_TILES):
            o_ref[pl.ds(j, tmo, stride=ROW_TILES), :] = o[:, j * LANES:(j + 1) * LANES]

    @pl.when(jnp.logical_not(valid))
    def _():
        o_ref[...] = jnp.zeros_like(o_ref)


def _moe_experts(xs, block_e, nvalid, p, n_blocks, tmo):
    l = p["layer"]
    last_valid = lambda i, nv: jnp.minimum(i, nv[0] - 1)
    return pl.pallas_call(
        functools.partial(_moe_kernel, tmo=tmo),
        grid_spec=pltpu.PrefetchScalarGridSpec(
            num_scalar_prefetch=2,
            grid=(n_blocks,),
            in_specs=[
                pl.BlockSpec((tmo * ROW_TILES, LANES), lambda i, be, nv: (last_valid(i, nv), 0)),
                pl.BlockSpec((1, 1, D_MODEL, 2 * D_FF), lambda i, be, nv: (l, be[i], 0, 0)),
                pl.BlockSpec((1, 1, 1, 2 * D_FF), lambda i, be, nv: (l, be[i], 0, 0)),
                pl.BlockSpec((1, 1, D_FF, D_MODEL), lambda i, be, nv: (l, be[i], 0, 0)),
                pl.BlockSpec((1, 1, 1, D_MODEL), lambda i, be, nv: (l, be[i], 0, 0)),
            ],
            out_specs=pl.BlockSpec((tmo * ROW_TILES, LANES), lambda i, be, nv: (i, 0)),
            scratch_shapes=[pltpu.VMEM((D_MODEL, 2 * D_FF), BF16), pltpu.VMEM((D_FF, D_MODEL), BF16)],
        ),
        out_shape=jax.ShapeDtypeStruct(xs.shape, F32),
        compiler_params=_cparams("arbitrary"),
        name="moe_experts",
    )(block_e, nvalid, xs, p["w_up"], p["b_up"], p["w_down"], p["b_down"])


def _combine_kernel(pstart_ref, ir_ref, irn_ref, x1r_ref, gate_ref, lng_ref, lnb_ref, eo_hbm, o_ref, buf_ref, sem,
                    *, tm, nt):
    i = pl.program_id(0)
    slot = i % 2

    def issue(idx_ref, s):
        def body(tk, carry):
            dst = pl.multiple_of(tk * ROW_TILES, ROW_TILES)
            for k in range(TOP_K):
                dest = pstart_ref[idx_ref[tk * SUBLANES + k]] + idx_ref[tk * SUBLANES + TOP_K + k]
                src = pl.multiple_of(dest * ROW_TILES, ROW_TILES)
                pltpu.make_async_copy(eo_hbm.at[pl.ds(src, ROW_TILES), :],
                                      buf_ref.at[s, k, pl.ds(dst, ROW_TILES), :], sem.at[s]).start()
            return carry

        lax.fori_loop(0, tm, body, 0)

    @pl.when(i == 0)
    def _():
        issue(ir_ref, 0)

    @pl.when(i + 1 < nt)
    def _():
        issue(irn_ref, 1 - slot)

    for k in range(TOP_K):
        _row_copy_wait(eo_hbm, buf_ref.at[slot, k], sem.at[slot], tm)

    gates = gate_ref[...]
    vs = []
    tot = jnp.zeros((tm, 1), F32)
    for j in range(ROW_TILES):
        acc = ALPHA * x1r_ref[pl.ds(j, tm, stride=ROW_TILES), :]
        for k in range(TOP_K):
            acc = acc + buf_ref[slot, k, pl.ds(j, tm, stride=ROW_TILES), :] * gates[:, k:k + 1]
        vs.append(acc)
        tot = tot + jnp.sum(acc, axis=-1, keepdims=True)
    mu = tot * (1.0 / D_MODEL)
    sq = jnp.zeros((tm, 1), F32)
    for j in range(ROW_TILES):
        vs[j] = vs[j] - mu
        sq = sq + jnp.sum(vs[j] * vs[j], axis=-1, keepdims=True)
    inv = lax.rsqrt(sq * (1.0 / D_MODEL) + LN_EPS)
    for j in range(ROW_TILES):
        sl = slice(j * LANES, (j + 1) * LANES)
        o_ref[:, sl] = vs[j] * inv * lng_ref[:, sl] + lnb_ref[:, sl]


def _combine(x1r, ir_flat, gates, pstart, eo, p, t, tm):
    nt = t // tm
    return pl.pallas_call(
        functools.partial(_combine_kernel, tm=tm, nt=nt),
        grid_spec=pltpu.PrefetchScalarGridSpec(
            num_scalar_prefetch=1,
            grid=(nt,),
            in_specs=[
                pl.BlockSpec((tm * SUBLANES,), lambda i, ps: (i,), memory_space=pltpu.SMEM),
                pl.BlockSpec((tm * SUBLANES,), lambda i, ps: (jnp.minimum(i + 1, nt - 1),), memory_space=pltpu.SMEM),
                pl.BlockSpec((tm * ROW_TILES, LANES), lambda i, ps: (i, 0)),
                pl.BlockSpec((tm, LANES), lambda i, ps: (i, 0)),
                pl.BlockSpec((1, D_MODEL), lambda i, ps: (0, 0)),
                pl.BlockSpec((1, D_MODEL), lambda i, ps: (0, 0)),
                pl.BlockSpec(memory_space=pl.ANY),
            ],
            out_specs=pl.BlockSpec((tm, D_MODEL), lambda i, ps: (i, 0)),
            scratch_shapes=[pltpu.VMEM((2, TOP_K, tm * ROW_TILES, LANES), F32), pltpu.SemaphoreType.DMA((2,))],
        ),
        out_shape=jax.ShapeDtypeStruct((t, D_MODEL), F32),
        compiler_params=_cparams("arbitrary"),
        name="moe_combine_ln",
    )(pstart, ir_flat, ir_flat, x1r, gates, p["ln2_g"], p["ln2_b"], eo)


def _moe(x1r, ir, gates, cnt, p, t, tm, tmo):
    n_assign = t * TOP_K
    n_blocks = (n_assign + N_EXPERTS * (tmo - 1) + tmo - 1) // tmo
    rows = n_blocks * tmo
    counts = cnt[0, :N_EXPERTS].astype(jnp.int32)
    padded = (counts + tmo - 1) // tmo * tmo
    pend = jnp.cumsum(padded)
    pstart = (pend - padded).astype(jnp.int32)
    pad_info = jnp.concatenate([pstart + counts, pend[-1:], padded - counts, rows - pend[-1:]]).astype(jnp.int32)
    block_start = jnp.arange(n_blocks, dtype=jnp.int32) * tmo
    block_e = jnp.minimum(jnp.sum((pend[None, :] <= block_start[:, None]).astype(jnp.int32), axis=1),
                          N_EXPERTS - 1)
    nvalid = (pend[-1:] // tmo).astype(jnp.int32)
    ir_flat = ir[:, :SUBLANES].reshape(t * SUBLANES)
    xs = _dispatch(x1r, ir_flat, pstart, pad_info, rows, t, tm)
    eo = _moe_experts(xs, block_e, nvalid, p, n_blocks, tmo)
    return _combine(x1r, ir_flat, gates, pstart, eo, p, t, tm)


def _t5_bucket(rel):
    half = N_BUCKETS // 2
    max_exact = half // 2
    ret = jnp.where(rel > 0, half, 0)
    n = jnp.abs(rel)
    nf = jnp.maximum(n, 1).astype(F32)
    large = max_exact + (jnp.log(nf / max_exact) / math.log(MAX_DISTANCE / max_exact)
                         * (half - max_exact)).astype(jnp.int32)
    large = jnp.minimum(large, half - 1)
    return ret + jnp.where(n < max_exact, n, large)


def _rel_bias(table, cq):
    koff = jnp.arange(WINDOW + cq) - WINDOW
    rel = koff[None, :] - jnp.arange(cq)[:, None]
    return jnp.transpose(table[_t5_bucket(rel)].astype(F32), (2, 0, 1))


def _pad_rows_front(a, rows):
    pad = jnp.zeros(a.shape[:1] + (rows - a.shape[1],) + a.shape[2:], a.dtype)
    return jnp.concatenate([pad, a], axis=1)


def _layer_params(l, w_in, ssd_conv_w, ssd_conv_b, ssd_dt_bias, ssd_a_log, ssd_d, ssd_norm_w, w_ssd_out,
                  attn_sinks, w_attn_out, sc_conv_w, w_sc_out, w_o, ln1_g, ln1_b,
                  router_w, router_b, w_up, b_up, w_down, b_down, ln2_g, ln2_b):
    wi = w_in[l]
    w_dt = wi[:, OFF_DT:OFF_Q]
    return {
        "w_main": jnp.concatenate([wi[:, :OFF_DT], wi[:, OFF_Q:OFF_K], wi[:, OFF_SCB:]], axis=1).astype(BF16),
        "w_kv": wi[:, OFF_K:OFF_SCB].astype(BF16),
        "w_dt": jnp.pad(w_dt, ((0, 0), (0, LANES - SSD_N_HEADS))).astype(BF16),
        "w_dtT": w_dt.T.astype(BF16),
        "conv_w": ssd_conv_w[l],
        "conv_b": ssd_conv_b[l][None, :],
        "dt_b": ssd_dt_bias[l][None, :],
        "dt_bT": ssd_dt_bias[l][:, None],
        "a_log": ssd_a_log[l][None, :],
        "a_logT": ssd_a_log[l][:, None],
        "d_exp": jnp.repeat(ssd_d[l], SSD_HEAD_DIM)[None, :],
        "norm_w": ssd_norm_w[l][None, :],
        "w_ssd_out": w_ssd_out[l].astype(BF16),
        "sinks": attn_sinks[l],
        "w_att_out": w_attn_out[l].astype(BF16),
        "sc_w": sc_conv_w[l],
        "w_sc_out": w_sc_out[l].astype(BF16),
        "w_o": w_o[l].astype(BF16),
        "ln1_g": ln1_g[l][None, :],
        "ln1_b": ln1_b[l][None, :],
        "router_w": jnp.pad(router_w[l], ((0, 0), (0, LANES - N_EXPERTS))).astype(BF16),
        "router_b": jnp.pad(router_b[l], (0, LANES - N_EXPERTS))[None, :],
        "layer": l,
        "w_up": w_up,
        "b_up": b_up[:, :, None, :],
        "w_down": w_down,
        "b_down": b_down[:, :, None, :],
        "ln2_g": ln2_g[l][None, :],
        "ln2_b": ln2_b[l][None, :],
    }


def _trunk_layer(x2d, bt, s, p, bias, ssd_hist, ssd_state, sc_hist, kv_cache, cfg):
    t = bt * s
    h_main = _matmul(x2d, p["w_main"], BF16, cfg["tm_in"], cfg["tn_in"])
    h_kv = _matmul(x2d, p["w_kv"], F32, cfg["tm_in"], 2 * ATT_KV_DIM)
    k_new = h_kv[:, :ATT_KV_DIM].reshape(bt, s, ATT_KV_DIM)
    v_new = h_kv[:, ATT_KV_DIM:].reshape(bt, s, ATT_KV_DIM)

    if kv_cache is None:
        zpad = jnp.zeros((bt, WINDOW, ATT_KV_DIM), BF16)
        kall = jnp.concatenate([zpad, k_new.astype(BF16)], axis=1)
        vall = jnp.concatenate([zpad, v_new.astype(BF16)], axis=1)
    else:
        kall = jnp.concatenate([kv_cache[0].astype(BF16), k_new.astype(BF16)], axis=1)
        vall = jnp.concatenate([kv_cache[1].astype(BF16), v_new.astype(BF16)], axis=1)
    att = _attention(h_main, kall, vall, bias, p["sinks"], bt, s, cfg["cq"], cfg["n_sub"], kv_cache is None)

    if ssd_hist is None:
        hist8 = jnp.zeros((bt, SUBLANES, SSD_CONV_DIM), F32)
        st0 = jnp.zeros((bt, SSD_N_HEADS, SSD_D_STATE, SSD_HEAD_DIM), F32)
        schist8 = jnp.zeros((bt, SUBLANES, D_MODEL), F32)
    else:
        hist8 = _pad_rows_front(ssd_hist, SUBLANES)
        st0 = jnp.swapaxes(ssd_state, 2, 3)
        schist8 = _pad_rows_front(sc_hist, SUBLANES)
    ssd_pre, st_out, hist_out = _ssd(x2d, h_main, hist8, st0, p, bt, s, cfg["cq"])

    x1r, ir, gates, cnt, schist_out = _merge(x2d, h_main, ssd_pre, att, schist8, p, bt, s, cfg["tm_merge"])
    x2 = _moe(x1r, ir, gates, cnt, p, t, cfg["tm_moe"], cfg["tmo"])

    n_keep = min(s, WINDOW)
    new_k = k_new[:, s - n_keep:].reshape(bt, n_keep, ATT_KV_HEADS, ATT_HEAD_DIM)
    new_v = v_new[:, s - n_keep:].reshape(bt, n_keep, ATT_KV_HEADS, ATT_HEAD_DIM)
    new_state = jnp.swapaxes(st_out, 2, 3)
    new_hist = hist_out[:, SUBLANES - (SSD_CONV - 1):]
    new_sc_hist = schist_out[:, SUBLANES - (SC_WIDTH - 1):]
    return x2, new_k, new_v, new_state, new_hist, new_sc_hist


def _config(bt, s, prompt):
    t = bt * s
    if prompt:
        return {"tm_in": min(t, 1024), "tn_in": 1024, "cq": CHUNK, "n_sub": min(4, s // CHUNK),
                "tm_merge": min(s, 256), "tm_moe": min(t, 256), "tmo": min(512, max(64, t // 32))}
    return {"tm_in": min(t, 256), "tn_in": 1024, "cq": s, "n_sub": 1,
            "tm_merge": s, "tm_moe": min(t, 256), "tmo": 64}


def _forward(x_prompt, x_sample, cache_attn_k, cache_attn_v, state_ssd, state_ssd_conv, state_short_conv,
             rel_bias, layer_weights):
    bp, sp, _ = x_prompt.shape
    bd, sd, _ = x_sample.shape
    cfg_p = _config(bp, sp, True)
    cfg_d = _config(bd, sd, False)
    bias_p = _rel_bias(rel_bias, cfg_p["cq"])
    bias_d = _rel_bias(rel_bias, cfg_d["cq"])
    yp = x_prompt.reshape(bp * sp, D_MODEL)
    ys = x_sample.reshape(bd * sd, D_MODEL)
    outs_p, outs_d = [], []
    for l in range(DEPTH):
        p = _layer_params(l, *layer_weights)
        yp, *rest = _trunk_layer(yp, bp, sp, p, bias_p, None, None, None, None, cfg_p)
        outs_p.append(rest)
        cache = (cache_attn_k[l].reshape(bd, WINDOW, ATT_KV_DIM), cache_attn_v[l].reshape(bd, WINDOW, ATT_KV_DIM))
        ys, *rest = _trunk_layer(ys, bd, sd, p, bias_d, state_ssd_conv[l], state_ssd[l], state_short_conv[l],
                                 cache, cfg_d)
        outs_d.append(rest)
    stack = lambda outs, i: jnp.stack([o[i] for o in outs])
    return (yp.reshape(bp, sp, D_MODEL), ys.reshape(bd, sd, D_MODEL),
            stack(outs_p, 0), stack(outs_p, 1), stack(outs_p, 2), stack(outs_p, 3), stack(outs_p, 4),
            stack(outs_d, 0), stack(outs_d, 1), stack(outs_d, 2), stack(outs_d, 3), stack(outs_d, 4))


def kernel(x_prompt, x_sample, cache_attn_k, cache_attn_v, state_ssd, state_ssd_conv, state_short_conv, w_in, ssd_conv_w, ssd_conv_b, ssd_dt_bias, ssd_a_log, ssd_d, ssd_norm_w, w_ssd_out, attn_sinks, w_attn_out, rel_bias, sc_conv_w, w_sc_out, w_o, ln1_g, ln1_b, router_w, router_b, w_up, b_up, w_down, b_down, ln2_g, ln2_b):
    layer_weights = (w_in, ssd_conv_w, ssd_conv_b, ssd_dt_bias, ssd_a_log, ssd_d, ssd_norm_w, w_ssd_out,
                     attn_sinks, w_attn_out, sc_conv_w, w_sc_out, w_o, ln1_g, ln1_b,
                     router_w, router_b, w_up, b_up, w_down, b_down, ln2_g, ln2_b)
    return _forward(x_prompt, x_sample, cache_attn_k, cache_attn_v, state_ssd, state_ssd_conv, state_short_conv,
                    rel_bias, layer_weights)
```

```python
import functools
import math

import jax
import jax.numpy as jnp
import numpy as np
from jax import lax
from jax.experimental import pallas as pl
from jax.experimental.pallas import tpu as pltpu

F32 = jnp.float32
BF16 = jnp.bfloat16

D_MODEL = 1024
DEPTH = 2
CHUNK = 64
SSD_D_INNER = 2 * D_MODEL
SSD_HEAD_DIM = 64
SSD_N_HEADS = SSD_D_INNER // SSD_HEAD_DIM
SSD_N_GROUPS = 8
SSD_D_STATE = 128
SSD_CONV = 4
SSD_GN = SSD_N_GROUPS * SSD_D_STATE
SSD_CONV_DIM = SSD_D_INNER + 2 * SSD_GN
SSD_HEADS_PER_GROUP = SSD_N_HEADS // SSD_N_GROUPS
SSD_NORM_GROUP = SSD_D_INNER // SSD_N_GROUPS
ATT_HEAD_DIM = 64
ATT_HEADS = D_MODEL // ATT_HEAD_DIM
ATT_KV_HEADS = ATT_HEADS // 4
ATT_GROUP = ATT_HEADS // ATT_KV_HEADS
ATT_KV_DIM = ATT_KV_HEADS * ATT_HEAD_DIM
WINDOW = 128
N_BUCKETS = 32
MAX_DISTANCE = 128
SC_WIDTH = 3
N_EXPERTS = 32
TOP_K = 4
D_FF = D_MODEL
SWIGLU_LIMIT = 7.0
SWIGLU_ALPHA = 1.702
LN_EPS = 1e-5
ALPHA = (2.0 * DEPTH) ** 0.25

LANES = 128
SUBLANES = 8
ROW_TILES = D_MODEL // LANES
VMEM_LIMIT = 56 * 1024 * 1024
ZERO_ROWS = 16

COL_Z, COL_XS, COL_BC = 0, 1, 2
COL_Q, COL_SCB, COL_SCC, COL_SCH, COL_GSSD, COL_GATT, COL_GSC = 6, 7, 8, 9, 10, 11, 12
N_MAIN = 13 * D_MODEL
OFF_DT = SSD_D_INNER + SSD_CONV_DIM
OFF_Q = OFF_DT + SSD_N_HEADS
OFF_K = OFF_Q + D_MODEL
OFF_SCB = OFF_K + 2 * ATT_KV_DIM


def _sigmoid(v):
    return 0.5 * jnp.tanh(0.5 * v) + 0.5


def _cparams(*sem):
    return pltpu.CompilerParams(dimension_semantics=sem, vmem_limit_bytes=VMEM_LIMIT)


def _matmul_kernel(x_ref, w_ref, o_ref, xb_ref):
    @pl.when(pl.program_id(1) == 0)
    def _():
        xb_ref[...] = x_ref[...].astype(BF16)

    o_ref[...] = jnp.dot(xb_ref[...], w_ref[...], preferred_element_type=F32).astype(o_ref.dtype)


def _matmul(x, w, out_dtype, tm, tn):
    t, k = x.shape
    n = w.shape[1]
    return pl.pallas_call(
        _matmul_kernel,
        grid=(t // tm, n // tn),
        in_specs=[pl.BlockSpec((tm, k), lambda i, j: (i, 0)),
                  pl.BlockSpec((k, tn), lambda i, j: (0, j))],
        out_specs=pl.BlockSpec((tm, tn), lambda i, j: (i, j)),
        out_shape=jax.ShapeDtypeStruct((t, n), out_dtype),
        scratch_shapes=[pltpu.VMEM((tm, k), BF16)],
        compiler_params=_cparams("arbitrary", "arbitrary"),
        name="in_proj",
    )(x, w)


def _ssd_kernel(x_ref, z_ref, xs_ref, bc_ref, hist_ref, st0_ref,
                wdt_ref, wdtT_ref, convw_ref, convb_ref, dtb_ref, dtbT_ref, alog_ref, alogT_ref,
                dexp_ref, normw_ref,
                y_ref, stout_ref, histout_ref,
                ext_ref, st_ref, ybuf_ref, *, L, nc):
    c = pl.program_id(1)

    @pl.when(c == 0)
    def _():
        ext_ref[0:SUBLANES, :] = hist_ref[0]
        st_ref[...] = st0_ref[0]

    xbc = jnp.concatenate([xs_ref[...], bc_ref[...]], axis=1).astype(F32)
    ext_ref[SUBLANES:SUBLANES + L, :] = xbc
    base = SUBLANES - (SSD_CONV - 1)
    conv = convb_ref[...]
    for j in range(SSD_CONV):
        conv = conv + ext_ref[base + j:base + j + L, :] * convw_ref[j:j + 1, :]
    tail = ext_ref[L:L + SUBLANES, :]
    ext_ref[0:SUBLANES, :] = tail
    histout_ref[0] = tail
    xc = conv * _sigmoid(conv)
    xs_c = xc[:, :SSD_D_INNER]
    b_m = xc[:, SSD_D_INNER:SSD_D_INNER + SSD_GN]
    c_m = xc[:, SSD_D_INNER + SSD_GN:]

    xb = x_ref[...].astype(BF16)
    dtr = jnp.dot(xb, wdt_ref[...], preferred_element_type=F32)[:, :SSD_N_HEADS]
    dtrT = lax.dot_general(wdtT_ref[...], xb, (((1,), (1,)), ((), ())),
                           preferred_element_type=F32)

    def softplus(v):
        return jnp.maximum(v, 0.0) + jnp.log1p(jnp.exp(-jnp.abs(v)))

    dt = softplus(dtr + dtb_ref[...])
    dtT = softplus(dtrT + dtbT_ref[...])
    adt = dt * (-jnp.exp(alog_ref[...]))
    adtT = dtT * (-jnp.exp(alogT_ref[...]))
    ri = lax.broadcasted_iota(jnp.int32, (L, L), 0)
    ci = lax.broadcasted_iota(jnp.int32, (L, L), 1)
    tri = ri >= ci
    acs = jnp.dot(tri.astype(F32), adt, precision=lax.Precision.HIGHEST,
                  preferred_element_type=F32)
    acsT = jnp.dot(adtT, (ri <= ci).astype(F32), precision=lax.Precision.HIGHEST,
                   preferred_element_type=F32)
    totT = jnp.sum(adtT, axis=1, keepdims=True)
    eacs = jnp.exp(acs)
    wT = jnp.exp(totT - acsT) * dtT
    cdT = jnp.exp(totT)

    for g in range(SSD_N_GROUPS):
        bg = b_m[:, g * SSD_D_STATE:(g + 1) * SSD_D_STATE]
        cg_b = c_m[:, g * SSD_D_STATE:(g + 1) * SSD_D_STATE].astype(BF16)
        cb = lax.dot_general(cg_b, bg.astype(BF16), (((1,), (1,)), ((), ())),
                             preferred_element_type=F32)
        if L < LANES:
            bg = jnp.concatenate([bg, jnp.zeros((LANES - L, SSD_D_STATE), F32)], axis=0)
        bgT = bg.T[:, :L]
        for r in range(SSD_HEADS_PER_GROUP):
            h = g * SSD_HEADS_PER_GROUP + r
            col = acs[:, h:h + 1]
            row = acsT[h:h + 1, :]
            decay = jnp.exp(jnp.where(tri, col - row, -jnp.inf))
            m = (cb * decay * dtT[h:h + 1, :]).astype(BF16)
            xh_b = xs_c[:, h * SSD_HEAD_DIM:(h + 1) * SSD_HEAD_DIM].astype(BF16)
            st = st_ref[h]
            yd = jnp.dot(m, xh_b, preferred_element_type=F32)
            yo = jnp.dot(cg_b, st.astype(BF16), preferred_element_type=F32)
            ybuf_ref[:, h * SSD_HEAD_DIM:(h + 1) * SSD_HEAD_DIM] = yd + yo * eacs[:, h:h + 1]
            wh = (bgT * wT[h:h + 1, :]).astype(BF16)
            stc = jnp.dot(wh, xh_b, preferred_element_type=F32)
            st_ref[h] = st * cdT[h:h + 1, :] + stc

    y = ybuf_ref[...] + dexp_ref[...] * xs_c
    z = z_ref[...].astype(F32)
    y = y * (z * _sigmoid(z))
    for g in range(SSD_N_GROUPS):
        sl = slice(g * SSD_NORM_GROUP, (g + 1) * SSD_NORM_GROUP)
        blk = y[:, sl]
        ms = jnp.mean(blk * blk, axis=-1, keepdims=True)
        y_ref[:, sl] = (blk * lax.rsqrt(ms + LN_EPS) * normw_ref[:, sl]).astype(y_ref.dtype)

    @pl.when(c == nc - 1)
    def _():
        stout_ref[0] = st_ref[...]


def _ssd(x2d, h_main, hist8, st0, p, bt, s, L):
    nc = s // L
    t = bt * s
    w2 = 2 * D_MODEL
    row = lambda b, c: (b * nc + c, 0)
    const2 = lambda b, c: (0, 0)
    kern = functools.partial(_ssd_kernel, L=L, nc=nc)
    return pl.pallas_call(
        kern,
        grid=(bt, nc),
        in_specs=[
            pl.BlockSpec((L, D_MODEL), row),
            pl.BlockSpec((L, w2), lambda b, c: (b * nc + c, COL_Z)),
            pl.BlockSpec((L, w2), lambda b, c: (b * nc + c, COL_XS)),
            pl.BlockSpec((L, w2), lambda b, c: (b * nc + c, COL_BC)),
            pl.BlockSpec((1, SUBLANES, SSD_CONV_DIM), lambda b, c: (b, 0, 0)),
            pl.BlockSpec((1, SSD_N_HEADS, SSD_D_STATE, SSD_HEAD_DIM), lambda b, c: (b, 0, 0, 0)),
            pl.BlockSpec((D_MODEL, LANES), const2),
            pl.BlockSpec((SSD_N_HEADS, D_MODEL), const2),
            pl.BlockSpec((SSD_CONV, SSD_CONV_DIM), const2),
            pl.BlockSpec((1, SSD_CONV_DIM), const2),
            pl.BlockSpec((1, SSD_N_HEADS), const2),
            pl.BlockSpec((SSD_N_HEADS, 1), const2),
            pl.BlockSpec((1, SSD_N_HEADS), const2),
            pl.BlockSpec((SSD_N_HEADS, 1), const2),
            pl.BlockSpec((1, SSD_D_INNER), const2),
            pl.BlockSpec((1, SSD_D_INNER), const2),
        ],
        out_specs=[
            pl.BlockSpec((L, SSD_D_INNER), row),
            pl.BlockSpec((1, SSD_N_HEADS, SSD_D_STATE, SSD_HEAD_DIM), lambda b, c: (b, 0, 0, 0)),
            pl.BlockSpec((1, SUBLANES, SSD_CONV_DIM), lambda b, c: (b, 0, 0)),
        ],
        out_shape=[
            jax.ShapeDtypeStruct((t, SSD_D_INNER), BF16),
            jax.ShapeDtypeStruct((bt, SSD_N_HEADS, SSD_D_STATE, SSD_HEAD_DIM), F32),
            jax.ShapeDtypeStruct((bt, SUBLANES, SSD_CONV_DIM), F32),
        ],
        scratch_shapes=[
            pltpu.VMEM((L + SUBLANES, SSD_CONV_DIM), F32),
            pltpu.VMEM((SSD_N_HEADS, SSD_D_STATE, SSD_HEAD_DIM), F32),
            pltpu.VMEM((L, SSD_D_INNER), F32),
        ],
        compiler_params=_cparams("arbitrary", "arbitrary"),
        name="ssd_scan",
    )(x2d, h_main, h_main, h_main, hist8, st0,
      p["w_dt"], p["w_dtT"], p["conv_w"], p["conv_b"], p["dt_b"], p["dt_bT"], p["a_log"], p["a_logT"],
      p["d_exp"], p["norm_w"])


def _ssd_group_kernel(x_ref, z_ref, xs_ref, bc_ref, hist_ref, st0_ref,
                      wdt_ref, wdtT_ref, convw_ref, convb_ref, dtb_ref, dtbT_ref, alog_ref, alogT_ref,
                      dexp_ref, normw_ref, expand_ref,
                      y_ref, stout_ref, histout_ref,
                      ext_ref, xrow_ref, st_ref, ynat_ref, *, L, nc):
    c = pl.program_id(1)
    stride = L // SUBLANES
    gw = SSD_HEADS_PER_GROUP * SSD_HEAD_DIM

    n_cb = SSD_CONV_DIM // LANES
    lanes = lambda cb: slice(cb * LANES, (cb + 1) * LANES)

    def strided_rows(ref, start):
        return jnp.concatenate([ref[cb, pl.ds(start, SUBLANES, stride=stride), :] for cb in range(ref.shape[0])],
                               axis=1)

    @pl.when(c == 0)
    def _():
        for cb in range(n_cb):
            ext_ref[cb, 0:SUBLANES, :] = hist_ref[0, :, lanes(cb)]
        st_ref[...] = st0_ref[0]

    for cb in range(n_cb):
        src = xs_ref if cb < SSD_D_INNER // LANES else bc_ref
        ext_ref[cb, SUBLANES:SUBLANES + L, :] = src[:, lanes(cb % (SSD_D_INNER // LANES))].astype(F32)
    base = SUBLANES - (SSD_CONV - 1)
    taps = {o: strided_rows(ext_ref, o) for o in range(base, base + stride + SSD_CONV - 1)}
    pieces = []
    for r in range(stride):
        conv = convb_ref[...] + taps[base + r] * convw_ref[0:1, :]
        for j in range(1, SSD_CONV):
            conv = conv + taps[base + r + j] * convw_ref[j:j + 1, :]
        pieces.append(conv * _sigmoid(conv))
    xc = jnp.concatenate(pieces, axis=0)
    for cb in range(n_cb):
        tail = ext_ref[cb, L:L + SUBLANES, :]
        ext_ref[cb, 0:SUBLANES, :] = tail
        histout_ref[0, :, lanes(cb)] = tail
    xs_c = xc[:, :SSD_D_INNER]
    b_m = xc[:, SSD_D_INNER:SSD_D_INNER + SSD_GN]
    c_m = xc[:, SSD_D_INNER + SSD_GN:]

    for cb in range(D_MODEL // LANES):
        xrow_ref[cb] = x_ref[:, lanes(cb)]
    xb = jnp.concatenate([strided_rows(xrow_ref, r) for r in range(stride)], axis=0).astype(BF16)
    lane_h = lax.broadcasted_iota(jnp.int32, (1, LANES), 1) < SSD_N_HEADS
    dtr = jnp.dot(xb, wdt_ref[...], preferred_element_type=F32)
    dtrT = lax.dot_general(wdtT_ref[...], xb, (((1,), (1,)), ((), ())),
                           preferred_element_type=F32)

    def softplus(v):
        return jnp.maximum(v, 0.0) + jnp.log1p(jnp.exp(-jnp.abs(v)))

    dt = jnp.where(lane_h, softplus(dtr + dtb_ref[...]), 0.0)
    dtT = softplus(dtrT + dtbT_ref[...])
    adt = dt * (-jnp.exp(alog_ref[...]))
    adtT = dtT * (-jnp.exp(alogT_ref[...]))
    ri = lax.broadcasted_iota(jnp.int32, (L, L), 0)
    ci = lax.broadcasted_iota(jnp.int32, (L, L), 1)
    time_r = stride * (ri % SUBLANES) + ri // SUBLANES
    time_c = stride * (ci % SUBLANES) + ci // SUBLANES
    tri = time_r >= time_c
    acs = jnp.dot(tri.astype(F32), adt, precision=lax.Precision.HIGHEST,
                  preferred_element_type=F32)
    acsT = jnp.dot(adtT, (time_r <= time_c).astype(F32), precision=lax.Precision.HIGHEST,
                   preferred_element_type=F32)
    tot = jnp.sum(adt, axis=0, keepdims=True)

    cols = jnp.concatenate([acs, jnp.exp(acs), jnp.exp(tot - acs) * dt,
                            jnp.broadcast_to(jnp.exp(tot), (SUBLANES, LANES))], axis=0)
    cols = jnp.where(lane_h, cols, 0.0)
    hi = cols.astype(BF16).astype(F32)
    mid = (cols - hi).astype(BF16).astype(F32)
    lo = (cols - hi - mid).astype(BF16).astype(F32)
    packed = (hi + pltpu.roll(mid, SSD_N_HEADS, 1) + pltpu.roll(lo, 2 * SSD_N_HEADS, 1)).astype(BF16)
    wide = jnp.dot(packed, expand_ref[...], preferred_element_type=F32)
    acs_w = wide[0:L]
    eacs_w = wide[L:2 * L]
    w_w = wide[2 * L:3 * L]
    cd_w = wide[3 * L:3 * L + 1]

    for g in range(SSD_N_GROUPS):
        gs = slice(g * gw, (g + 1) * gw)
        bg = b_m[:, g * SSD_D_STATE:(g + 1) * SSD_D_STATE]
        cg_b = c_m[:, g * SSD_D_STATE:(g + 1) * SSD_D_STATE].astype(BF16)
        cb = lax.dot_general(cg_b, bg.astype(BF16), (((1,), (1,)), ((), ())),
                             preferred_element_type=F32)
        if L < LANES:
            bg = jnp.concatenate([bg, jnp.zeros((LANES - L, SSD_D_STATE), F32)], axis=0)
        bgT_b = bg.T[:, :L].astype(BF16)
        xs_g = xs_c[:, gs]
        st = st_ref[g]
        yo = jnp.dot(cg_b, st.astype(BF16), preferred_element_type=F32)
        stc = jnp.dot(bgT_b, (xs_g * w_w[:, gs]).astype(BF16), preferred_element_type=F32)
        st_ref[g] = st * cd_w[:, gs] + stc
        yds = []
        for r in range(SSD_HEADS_PER_GROUP):
            h = g * SSD_HEADS_PER_GROUP + r
            col = acs_w[:, h * SSD_HEAD_DIM:h * SSD_HEAD_DIM + L]
            decay = jnp.exp(jnp.where(tri, col - acsT[h:h + 1, :], -jnp.inf))
            m = (cb * decay * dtT[h:h + 1, :]).astype(BF16)
            yds.append(jnp.dot(m, xs_g[:, r * SSD_HEAD_DIM:(r + 1) * SSD_HEAD_DIM].astype(BF16),
                               preferred_element_type=F32))
        y_g = jnp.concatenate(yds, axis=1) + yo * eacs_w[:, gs] + dexp_ref[:, gs] * xs_g
        for r in range(stride):
            for cb in range(gw // LANES):
                ynat_ref[g * (gw // LANES) + cb, pl.ds(r, SUBLANES, stride=stride), :] = (
                    y_g[r * SUBLANES:(r + 1) * SUBLANES, lanes(cb)])

    z = z_ref[...].astype(F32)
    y = jnp.concatenate([ynat_ref[cb] for cb in range(SSD_D_INNER // LANES)], axis=1) * (z * _sigmoid(z))
    for g in range(SSD_N_GROUPS):
        sl = slice(g * SSD_NORM_GROUP, (g + 1) * SSD_NORM_GROUP)
        blk = y[:, sl]
        ms = jnp.mean(blk * blk, axis=-1, keepdims=True)
        y_ref[:, sl] = (blk * lax.rsqrt(ms + LN_EPS) * normw_ref[:, sl]).astype(y_ref.dtype)

    @pl.when(c == nc - 1)
    def _():
        stout_ref[0] = st_ref[...]


def _ssd_grouped(x2d, h_main, hist8, st0, p, bt, s, L):
    assert L % SUBLANES == 0 and L <= SSD_HEAD_DIM
    nc = s // L
    t = bt * s
    w2 = 2 * D_MODEL
    gw = SSD_HEADS_PER_GROUP * SSD_HEAD_DIM
    row = lambda b, c: (b * nc + c, 0)
    const2 = lambda b, c: (0, 0)
    state_spec = pl.BlockSpec((1, SSD_N_GROUPS, SSD_D_STATE, gw), lambda b, c: (b, 0, 0, 0))
    kern = functools.partial(_ssd_group_kernel, L=L, nc=nc)
    return pl.pallas_call(
        kern,
        grid=(bt, nc),
        in_specs=[
            pl.BlockSpec((L, D_MODEL), row),
            pl.BlockSpec((L, w2), lambda b, c: (b * nc + c, COL_Z)),
            pl.BlockSpec((L, w2), lambda b, c: (b * nc + c, COL_XS)),
            pl.BlockSpec((L, w2), lambda b, c: (b * nc + c, COL_BC)),
            pl.BlockSpec((1, SUBLANES, SSD_CONV_DIM), lambda b, c: (b, 0, 0)),
            state_spec,
            pl.BlockSpec((D_MODEL, LANES), const2),
            pl.BlockSpec((SSD_N_HEADS, D_MODEL), const2),
            pl.BlockSpec((SSD_CONV, SSD_CONV_DIM), const2),
            pl.BlockSpec((1, SSD_CONV_DIM), const2),
            pl.BlockSpec((1, LANES), const2),
            pl.BlockSpec((SSD_N_HEADS, 1), const2),
            pl.BlockSpec((1, LANES), const2),
            pl.BlockSpec((SSD_N_HEADS, 1), const2),
            pl.BlockSpec((1, SSD_D_INNER), const2),
            pl.BlockSpec((1, SSD_D_INNER), const2),
            pl.BlockSpec((LANES, SSD_D_INNER), const2),
        ],
        out_specs=[
            pl.BlockSpec((L, SSD_D_INNER), row),
            state_spec,
            pl.BlockSpec((1, SUBLANES, SSD_CONV_DIM), lambda b, c: (b, 0, 0)),
        ],
        out_shape=[
            jax.ShapeDtypeStruct((t, SSD_D_INNER), BF16),
            jax.ShapeDtypeStruct((bt, SSD_N_GROUPS, SSD_D_STATE, gw), F32),
            jax.ShapeDtypeStruct((bt, SUBLANES, SSD_CONV_DIM), F32),
        ],
        scratch_shapes=[
            pltpu.VMEM((SSD_CONV_DIM // LANES, L + SUBLANES, LANES), F32),
            pltpu.VMEM((D_MODEL // LANES, L, LANES), F32),
            pltpu.VMEM((SSD_N_GROUPS, SSD_D_STATE, gw), F32),
            pltpu.VMEM((SSD_D_INNER // LANES, L, LANES), F32),
        ],
        compiler_params=_cparams("arbitrary", "arbitrary"),
        name="ssd_scan",
    )(x2d, h_main, h_main, h_main, hist8, st0,
      p["w_dt"], p["w_dtT"], p["conv_w"], p["conv_b"], p["dt_b128"], p["dt_bT"], p["a_log128"], p["a_logT"],
      p["d_exp"], p["norm_w"], p["expand"])


def _head_expand_matrix():
    e = np.zeros((LANES, SSD_D_INNER), np.float32)
    for part in range(3):
        for h in range(SSD_N_HEADS):
            e[part * SSD_N_HEADS + h, h * SSD_HEAD_DIM:(h + 1) * SSD_HEAD_DIM] = 1.0
    return e


def _state_to_kernel(state):
    b = state.shape[0]
    s5 = state.reshape(b, SSD_N_GROUPS, SSD_HEADS_PER_GROUP, SSD_HEAD_DIM, SSD_D_STATE)
    return jnp.transpose(s5, (0, 1, 4, 2, 3)).reshape(b, SSD_N_GROUPS, SSD_D_STATE,
                                                       SSD_HEADS_PER_GROUP * SSD_HEAD_DIM)


def _state_from_kernel(st):
    b = st.shape[0]
    s5 = st.reshape(b, SSD_N_GROUPS, SSD_D_STATE, SSD_HEADS_PER_GROUP, SSD_HEAD_DIM)
    return jnp.transpose(s5, (0, 1, 3, 4, 2)).reshape(b, SSD_N_HEADS, SSD_HEAD_DIM, SSD_D_STATE)


def _attn_kernel(q_ref, k_ref, v_ref, bias_ref, sink_ref, o_ref, *, cq, n_sub, mask_prefix):
    step = pl.program_id(1)
    kw = WINDOW + cq
    scale = ATT_HEAD_DIM ** -0.5
    gw = ATT_GROUP * ATT_HEAD_DIM
    pairs = [(sub, hk) for sub in range(n_sub) for hk in range(ATT_KV_HEADS)]
    scs, vhs = {}, {}
    for sub in range(n_sub):
        r0 = pl.multiple_of((step * n_sub + sub) * cq, cq)
        kk = k_ref[0, pl.ds(r0, kw), :]
        vv = v_ref[0, pl.ds(r0, kw), :]
        qq = q_ref[sub * cq:(sub + 1) * cq, :]
        if mask_prefix:
            valid = (r0 + lax.broadcasted_iota(jnp.int32, (1, kw), 1)) >= WINDOW
        for hk in range(ATT_KV_HEADS):
            q4 = jnp.concatenate(
                [qq[:, hk * gw + g * ATT_HEAD_DIM:hk * gw + (g + 1) * ATT_HEAD_DIM] for g in range(ATT_GROUP)],
                axis=0)
            kh = kk[:, hk * ATT_HEAD_DIM:(hk + 1) * ATT_HEAD_DIM]
            sc = lax.dot_general(q4, kh, (((1,), (1,)), ((), ())), preferred_element_type=F32)
            sc = sc * scale + bias_ref[hk]
            if mask_prefix:
                sc = jnp.where(valid, sc, -1e30)
            scs[sub, hk] = sc
            vhs[sub, hk] = vv[:, hk * ATT_HEAD_DIM:(hk + 1) * ATT_HEAD_DIM]
    es, rs = {}, {}
    for sub, hk in pairs:
        sink = sink_ref[hk]
        mx = jnp.maximum(jnp.max(scs[sub, hk], axis=-1, keepdims=True), sink)
        e = jnp.exp(scs[sub, hk] - mx)
        den = jnp.sum(e, axis=-1, keepdims=True) + jnp.exp(sink - mx)
        es[sub, hk] = e.astype(BF16)
        rs[sub, hk] = 1.0 / den
    for sub, hk in pairs:
        o4 = jnp.dot(es[sub, hk], vhs[sub, hk], preferred_element_type=F32) * rs[sub, hk]
        o_ref[sub * cq:(sub + 1) * cq, hk * gw:(hk + 1) * gw] = jnp.concatenate(
            [o4[g * cq:(g + 1) * cq, :] for g in range(ATT_GROUP)], axis=1).astype(o_ref.dtype)


def _attention(h_main, kall, vall, bias, sinks, bt, s, cq, n_sub, mask_prefix):
    tq = cq * n_sub
    nq = s // tq
    kw = WINDOW + cq
    bias4 = bias.reshape(ATT_KV_HEADS, ATT_GROUP * cq, kw)
    sink4 = jnp.repeat(sinks, cq).reshape(ATT_KV_HEADS, ATT_GROUP * cq, 1)
    kern = functools.partial(_attn_kernel, cq=cq, n_sub=n_sub, mask_prefix=mask_prefix)
    return pl.pallas_call(
        kern,
        grid=(bt, nq),
        in_specs=[
            pl.BlockSpec((tq, D_MODEL), lambda b, i: (b * nq + i, COL_Q)),
            pl.BlockSpec((1, WINDOW + s, ATT_KV_DIM), lambda b, i: (b, 0, 0)),
            pl.BlockSpec((1, WINDOW + s, ATT_KV_DIM), lambda b, i: (b, 0, 0)),
            pl.BlockSpec((ATT_KV_HEADS, ATT_GROUP * cq, kw), lambda b, i: (0, 0, 0)),
            pl.BlockSpec((ATT_KV_HEADS, ATT_GROUP * cq, 1), lambda b, i: (0, 0, 0)),
        ],
        out_specs=pl.BlockSpec((tq, D_MODEL), lambda b, i: (b * nq + i, 0)),
        out_shape=jax.ShapeDtypeStruct((bt * s, D_MODEL), BF16),
        compiler_params=_cparams("arbitrary", "arbitrary"),
        name="swa_attention",
    )(h_main, kall, vall, bias4, sink4)


def _merge_kernel(x_ref, ssd_ref, att_ref, scb_ref, scc_ref, sch_ref, gssd_ref, gatt_ref, gsc_ref, hist_ref,
                  wssd_ref, watt_ref, wsc_ref, wo_ref, scw_ref, lng_ref, lnb_ref, rw_ref, rb_ref,
                  x1r_ref, ir_ref, gate_ref, cnt_ref, histout_ref,
                  ext_ref, base_ref, *, tm, tiles_per_batch):
    i = pl.program_id(0)

    @pl.when(i == 0)
    def _():
        base_ref[...] = jnp.zeros_like(base_ref)

    @pl.when(i % tiles_per_batch == 0)
    def _():
        ext_ref[0:SUBLANES, :] = hist_ref[0]

    u = scc_ref[...].astype(F32) * sch_ref[...].astype(F32)
    ext_ref[SUBLANES:SUBLANES + tm, :] = u
    off = SUBLANES - (SC_WIDTH - 1)
    conv = ext_ref[off:off + tm, :] * scw_ref[0:1, :]
    for j in range(1, SC_WIDTH):
        conv = conv + ext_ref[off + j:off + j + tm, :] * scw_ref[j:j + 1, :]
    tail = ext_ref[tm:tm + SUBLANES, :]
    ext_ref[0:SUBLANES, :] = tail
    histout_ref[0] = tail
    sc_pre = (scb_ref[...].astype(F32) * conv).astype(BF16)

    y_sc = jnp.dot(sc_pre, wsc_ref[...], preferred_element_type=F32)
    y_ssd = jnp.dot(ssd_ref[...], wssd_ref[...], preferred_element_type=F32)
    y_att = jnp.dot(att_ref[...], watt_ref[...], preferred_element_type=F32)
    merged = (_sigmoid(gssd_ref[...].astype(F32)) * y_ssd
              + _sigmoid(gatt_ref[...].astype(F32)) * y_att
              + _sigmoid(gsc_ref[...].astype(F32)) * y_sc)
    v = ALPHA * x_ref[...] + jnp.dot(merged.astype(BF16), wo_ref[...], preferred_element_type=F32)
    mu = jnp.mean(v, axis=-1, keepdims=True)
    vc = v - mu
    var = jnp.mean(vc * vc, axis=-1, keepdims=True)
    x1 = vc * lax.rsqrt(var + LN_EPS) * lng_ref[...] + lnb_ref[...]
    for j in range(ROW_TILES):
        x1r_ref[pl.ds(j, tm, stride=ROW_TILES), :] = x1[:, j * LANES:(j + 1) * LANES]

    lane = lax.broadcasted_iota(jnp.int32, (tm, LANES), 1)
    logits = jnp.dot(x1.astype(BF16), rw_ref[...], preferred_element_type=F32) + rb_ref[...]
    work = jnp.where(lane < N_EXPERTS, logits, -jnp.inf)
    vals, idxs = [], []
    onehot = jnp.zeros((tm, LANES), F32)
    for k in range(TOP_K):
        mv = jnp.max(work, axis=-1, keepdims=True)
        mi = jnp.min(jnp.where(work == mv, lane, LANES), axis=-1, keepdims=True)
        vals.append(mv)
        idxs.append(mi)
        work = jnp.where(lane == mi, -jnp.inf, work)
        onehot = onehot + (lane == mi + k * N_EXPERTS).astype(F32)
    es = [jnp.exp(vk - vals[0]) for vk in vals]
    den = es[0] + es[1] + es[2] + es[3]
    ri = lax.broadcasted_iota(jnp.int32, (tm, tm), 0)
    ci = lax.broadcasted_iota(jnp.int32, (tm, tm), 1)
    ltri = (ri > ci).astype(BF16)
    prefix = jnp.dot(ltri, onehot.astype(BF16), preferred_element_type=F32)
    cnt = jnp.sum(onehot, axis=0, keepdims=True)
    lane1 = lax.broadcasted_iota(jnp.int32, (1, LANES), 1)
    base = base_ref[0:1, :]
    offs = base
    tot = cnt
    for sh in range(1, TOP_K):
        rolled = pltpu.roll(cnt, sh * N_EXPERTS, 1)
        offs = offs + jnp.where(lane1 >= sh * N_EXPERTS, rolled, 0.0)
        tot = tot + rolled
    rank_all = prefix + offs
    ir = jnp.zeros((tm, LANES), jnp.int32)
    gt = jnp.zeros((tm, LANES), F32)
    for k in range(TOP_K):
        sel = lane == idxs[k] + k * N_EXPERTS
        rk = jnp.sum(jnp.where(sel, rank_all, 0.0), axis=-1, keepdims=True).astype(jnp.int32)
        ir = jnp.where(lane == k, idxs[k], ir)
        ir = jnp.where(lane == TOP_K + k, rk, ir)
        gt = jnp.where(lane == k, es[k] / den, gt)
    ir_ref[...] = ir
    gate_ref[...] = gt
    new_base = base + tot
    base_ref[...] = jnp.broadcast_to(new_base, base_ref.shape)
    cnt_ref[...] = jnp.broadcast_to(new_base, cnt_ref.shape)


def _merge(x2d, h_main, ssd_pre, att, schist8, p, bt, s, tm):
    t = bt * s
    nt = t // tm
    tiles_per_batch = s // tm
    row = lambda i: (i, 0)
    const2 = lambda i: (0, 0)
    hcol = lambda cidx: pl.BlockSpec((tm, D_MODEL), lambda i: (i, cidx))
    kern = functools.partial(_merge_kernel, tm=tm, tiles_per_batch=tiles_per_batch)
    return pl.pallas_call(
        kern,
        grid=(nt,),
        in_specs=[
            pl.BlockSpec((tm, D_MODEL), row),
            pl.BlockSpec((tm, SSD_D_INNER), row),
            pl.BlockSpec((tm, D_MODEL), row),
            hcol(COL_SCB), hcol(COL_SCC), hcol(COL_SCH), hcol(COL_GSSD), hcol(COL_GATT), hcol(COL_GSC),
            pl.BlockSpec((1, SUBLANES, D_MODEL), lambda i: (i // tiles_per_batch, 0, 0)),
            pl.BlockSpec((SSD_D_INNER, D_MODEL), const2, pipeline_mode=pl.Buffered(1)),
            pl.BlockSpec((D_MODEL, D_MODEL), const2, pipeline_mode=pl.Buffered(1)),
            pl.BlockSpec((D_MODEL, D_MODEL), const2, pipeline_mode=pl.Buffered(1)),
            pl.BlockSpec((D_MODEL, D_MODEL), const2, pipeline_mode=pl.Buffered(1)),
            pl.BlockSpec((SC_WIDTH, D_MODEL), const2),
            pl.BlockSpec((1, D_MODEL), const2),
            pl.BlockSpec((1, D_MODEL), const2),
            pl.BlockSpec((D_MODEL, LANES), const2),
            pl.BlockSpec((1, LANES), const2),
        ],
        out_specs=[
            pl.BlockSpec((tm * ROW_TILES, LANES), row),
            pl.BlockSpec((tm, LANES), row),
            pl.BlockSpec((tm, LANES), row),
            pl.BlockSpec((SUBLANES, LANES), const2),
            pl.BlockSpec((1, SUBLANES, D_MODEL), lambda i: (i // tiles_per_batch, 0, 0)),
        ],
        out_shape=[
            jax.ShapeDtypeStruct((t * ROW_TILES, LANES), F32),
            jax.ShapeDtypeStruct((t, LANES), jnp.int32),
            jax.ShapeDtypeStruct((t, LANES), F32),
            jax.ShapeDtypeStruct((SUBLANES, LANES), F32),
            jax.ShapeDtypeStruct((bt, SUBLANES, D_MODEL), F32),
        ],
        scratch_shapes=[
            pltpu.VMEM((tm + SUBLANES, D_MODEL), F32),
            pltpu.VMEM((SUBLANES, LANES), F32),
        ],
        compiler_params=_cparams("arbitrary"),
        name="merge_ln_router",
    )(x2d, ssd_pre, att, h_main, h_main, h_main, h_main, h_main, h_main, schist8,
      p["w_ssd_out"], p["w_att_out"], p["w_sc_out"], p["w_o"], p["sc_w"], p["ln1_g"], p["ln1_b"],
      p["router_w"], p["router_b"])


def _row_copy_wait(src_hbm, dst, sem, n_rows):
    pltpu.make_async_copy(src_hbm.at[pl.ds(0, n_rows * ROW_TILES), :],
                          dst.at[pl.ds(0, n_rows * ROW_TILES), :], sem).wait()


def _dispatch_kernel(pstart_ref, pad_ref, ir_ref, x_ref, xs_hbm, zero_ref, sem, zsem, *, tm):
    @pl.when(pl.program_id(0) == 0)
    def _():
        zero_ref[...] = jnp.zeros_like(zero_ref)

        def zero_copy(row, n_rows):
            dst = pl.multiple_of(row * ROW_TILES, ROW_TILES)
            return pltpu.make_async_copy(zero_ref.at[pl.ds(0, n_rows * ROW_TILES), :],
                                         xs_hbm.at[pl.ds(dst, n_rows * ROW_TILES), :], zsem)

        def per_expert(start):
            def body(e, carry):
                first = pad_ref[e]
                n = pad_ref[N_EXPERTS + 1 + e]
                n_big = n // ZERO_ROWS

                def big(j, c):
                    cp = zero_copy(first + j * ZERO_ROWS, ZERO_ROWS)
                    cp.start() if start else cp.wait()
                    return c

                def small(j, c):
                    cp = zero_copy(first + j, 1)
                    cp.start() if start else cp.wait()
                    return c

                lax.fori_loop(0, n_big, big, 0)
                lax.fori_loop(n_big * ZERO_ROWS, n, small, 0)
                return carry

            lax.fori_loop(0, N_EXPERTS + 1, body, 0)

        per_expert(True)
        per_expert(False)

    def issue(tk, carry):
        src = pl.multiple_of(tk * ROW_TILES, ROW_TILES)
        for k in range(TOP_K):
            dest = pstart_ref[ir_ref[tk * SUBLANES + k]] + ir_ref[tk * SUBLANES + TOP_K + k]
            dst = pl.multiple_of(dest * ROW_TILES, ROW_TILES)
            pltpu.make_async_copy(x_ref.at[pl.ds(src, ROW_TILES), :],
                                  xs_hbm.at[pl.ds(dst, ROW_TILES), :], sem).start()
        return carry

    lax.fori_loop(0, tm, issue, 0)
    for _ in range(TOP_K):
        _row_copy_wait(x_ref, xs_hbm, sem, tm)


def _dispatch(x1r, ir_flat, pstart, pad_info, rows, t, tm):
    return pl.pallas_call(
        functools.partial(_dispatch_kernel, tm=tm),
        grid_spec=pltpu.PrefetchScalarGridSpec(
            num_scalar_prefetch=2,
            grid=(t // tm,),
            in_specs=[
                pl.BlockSpec((tm * SUBLANES,), lambda i, ps, pd: (i,), memory_space=pltpu.SMEM),
                pl.BlockSpec((tm * ROW_TILES, LANES), lambda i, ps, pd: (i, 0)),
            ],
            out_specs=pl.BlockSpec(memory_space=pl.ANY),
            scratch_shapes=[pltpu.VMEM((ZERO_ROWS * ROW_TILES, LANES), F32),
                            pltpu.SemaphoreType.DMA, pltpu.SemaphoreType.DMA],
        ),
        out_shape=jax.ShapeDtypeStruct((rows * ROW_TILES, LANES), F32),
        compiler_params=_cparams("arbitrary"),
        name="moe_dispatch",
    )(pstart, pad_info, ir_flat, x1r)


def _moe_kernel(be_ref, nv_ref, x_ref, wu_ref, bu_ref, wd_ref, bd_ref, o_ref, wub_ref, wdb_ref, *, tmo):
    i = pl.program_id(0)
    valid = i < nv_ref[0]
    new_expert = jnp.logical_or(i == 0, be_ref[i] != be_ref[jnp.maximum(i - 1, 0)])

    @pl.when(jnp.logical_and(valid, new_expert))
    def _():
        wub_ref[...] = wu_ref[0, 0].astype(BF16)
        wdb_ref[...] = wd_ref[0, 0].astype(BF16)

    @pl.when(valid)
    def _():
        x = jnp.concatenate([x_ref[pl.ds(j, tmo, stride=ROW_TILES), :] for j in range(ROW_TILES)], axis=1)
        h = jnp.dot(x.astype(BF16), wub_ref[...], preferred_element_type=F32) + bu_ref[0, 0]
        gate = jnp.minimum(h[:, :D_FF], SWIGLU_LIMIT)
        up = jnp.clip(h[:, D_FF:], -SWIGLU_LIMIT, SWIGLU_LIMIT)
        act = (up + 1.0) * gate * _sigmoid(SWIGLU_ALPHA * gate)
        o = jnp.dot(act.astype(BF16), wdb_ref[...], preferred_element_type=F32) + bd_ref[0, 0]
        for j in range(ROW_TILES):
            o_ref[pl.ds(j, tmo, stride=ROW_TILES), :] = o[:, j * LANES:(j + 1) * LANES]

    @pl.when(jnp.logical_not(valid))
    def _():
        o_ref[...] = jnp.zeros_like(o_ref)


def _moe_experts(xs, block_e, nvalid, p, n_blocks, tmo):
    l = p["layer"]
    last_valid = lambda i, nv: jnp.minimum(i, nv[0] - 1)
    return pl.pallas_call(
        functools.partial(_moe_kernel, tmo=tmo),
        grid_spec=pltpu.PrefetchScalarGridSpec(
            num_scalar_prefetch=2,
            grid=(n_blocks,),
            in_specs=[
                pl.BlockSpec((tmo * ROW_TILES, LANES), lambda i, be, nv: (last_valid(i, nv), 0)),
                pl.BlockSpec((1, 1, D_MODEL, 2 * D_FF), lambda i, be, nv: (l, be[i], 0, 0)),
                pl.BlockSpec((1, 1, 1, 2 * D_FF), lambda i, be, nv: (l, be[i], 0, 0)),
                pl.BlockSpec((1, 1, D_FF, D_MODEL), lambda i, be, nv: (l, be[i], 0, 0)),
                pl.BlockSpec((1, 1, 1, D_MODEL), lambda i, be, nv: (l, be[i], 0, 0)),
            ],
            out_specs=pl.BlockSpec((tmo * ROW_TILES, LANES), lambda i, be, nv: (i, 0)),
            scratch_shapes=[pltpu.VMEM((D_MODEL, 2 * D_FF), BF16), pltpu.VMEM((D_FF, D_MODEL), BF16)],
        ),
        out_shape=jax.ShapeDtypeStruct(xs.shape, F32),
        compiler_params=_cparams("arbitrary"),
        name="moe_experts",
    )(block_e, nvalid, xs, p["w_up"], p["b_up"], p["w_down"], p["b_down"])


def _combine_kernel(pstart_ref, ir_ref, irn_ref, x1r_ref, gate_ref, lng_ref, lnb_ref, eo_hbm, o_ref, buf_ref, sem,
                    *, tm, nt):
    i = pl.program_id(0)
    slot = i % 2

    def issue(idx_ref, s):
        def body(tk, carry):
            dst = pl.multiple_of(tk * ROW_TILES, ROW_TILES)
            for k in range(TOP_K):
                dest = pstart_ref[idx_ref[tk * SUBLANES + k]] + idx_ref[tk * SUBLANES + TOP_K + k]
                src = pl.multiple_of(dest * ROW_TILES, ROW_TILES)
                pltpu.make_async_copy(eo_hbm.at[pl.ds(src, ROW_TILES), :],
                                      buf_ref.at[s, k, pl.ds(dst, ROW_TILES), :], sem.at[s]).start()
            return carry

        lax.fori_loop(0, tm, body, 0)

    @pl.when(i == 0)
    def _():
        issue(ir_ref, 0)

    @pl.when(i + 1 < nt)
    def _():
        issue(irn_ref, 1 - slot)

    for k in range(TOP_K):
        _row_copy_wait(eo_hbm, buf_ref.at[slot, k], sem.at[slot], tm)

    gates = gate_ref[...]
    vs = []
    tot = jnp.zeros((tm, 1), F32)
    for j in range(ROW_TILES):
        acc = ALPHA * x1r_ref[pl.ds(j, tm, stride=ROW_TILES), :]
        for k in range(TOP_K):
            acc = acc + buf_ref[slot, k, pl.ds(j, tm, stride=ROW_TILES), :] * gates[:, k:k + 1]
        vs.append(acc)
        tot = tot + jnp.sum(acc, axis=-1, keepdims=True)
    mu = tot * (1.0 / D_MODEL)
    sq = jnp.zeros((tm, 1), F32)
    for j in range(ROW_TILES):
        vs[j] = vs[j] - mu
        sq = sq + jnp.sum(vs[j] * vs[j], axis=-1, keepdims=True)
    inv = lax.rsqrt(sq * (1.0 / D_MODEL) + LN_EPS)
    for j in range(ROW_TILES):
        sl = slice(j * LANES, (j + 1) * LANES)
        o_ref[:, sl] = vs[j] * inv * lng_ref[:, sl] + lnb_ref[:, sl]


def _combine(x1r, ir_flat, gates, pstart, eo, p, t, tm):
    nt = t // tm
    return pl.pallas_call(
        functools.partial(_combine_kernel, tm=tm, nt=nt),
        grid_spec=pltpu.PrefetchScalarGridSpec(
            num_scalar_prefetch=1,
            grid=(nt,),
            in_specs=[
                pl.BlockSpec((tm * SUBLANES,), lambda i, ps: (i,), memory_space=pltpu.SMEM),
                pl.BlockSpec((tm * SUBLANES,), lambda i, ps: (jnp.minimum(i + 1, nt - 1),), memory_space=pltpu.SMEM),
                pl.BlockSpec((tm * ROW_TILES, LANES), lambda i, ps: (i, 0)),
                pl.BlockSpec((tm, LANES), lambda i, ps: (i, 0)),
                pl.BlockSpec((1, D_MODEL), lambda i, ps: (0, 0)),
                pl.BlockSpec((1, D_MODEL), lambda i, ps: (0, 0)),
                pl.BlockSpec(memory_space=pl.ANY),
            ],
            out_specs=pl.BlockSpec((tm, D_MODEL), lambda i, ps: (i, 0)),
            scratch_shapes=[pltpu.VMEM((2, TOP_K, tm * ROW_TILES, LANES), F32), pltpu.SemaphoreType.DMA((2,))],
        ),
        out_shape=jax.ShapeDtypeStruct((t, D_MODEL), F32),
        compiler_params=_cparams("arbitrary"),
        name="moe_combine_ln",
    )(pstart, ir_flat, ir_flat, x1r, gates, p["ln2_g"], p["ln2_b"], eo)


def _moe(x1r, ir, gates, cnt, p, t, tm, tmo):
    n_assign = t * TOP_K
    n_blocks = (n_assign + N_EXPERTS * (tmo - 1) + tmo - 1) // tmo
    rows = n_blocks * tmo
    counts = cnt[0, :N_EXPERTS].astype(jnp.int32)
    padded = (counts + tmo - 1) // tmo * tmo
    pend = jnp.cumsum(padded)
    pstart = (pend - padded).astype(jnp.int32)
    pad_info = jnp.concatenate([pstart + counts, pend[-1:], padded - counts, rows - pend[-1:]]).astype(jnp.int32)
    block_start = jnp.arange(n_blocks, dtype=jnp.int32) * tmo
    block_e = jnp.minimum(jnp.sum((pend[None, :] <= block_start[:, None]).astype(jnp.int32), axis=1),
                          N_EXPERTS - 1)
    nvalid = (pend[-1:] // tmo).astype(jnp.int32)
    ir_flat = ir[:, :SUBLANES].reshape(t * SUBLANES)
    xs = _dispatch(x1r, ir_flat, pstart, pad_info, rows, t, tm)
    eo = _moe_experts(xs, block_e, nvalid, p, n_blocks, tmo)
    return _combine(x1r, ir_flat, gates, pstart, eo, p, t, tm)


def _t5_bucket(rel):
    half = N_BUCKETS // 2
    max_exact = half // 2
    ret = jnp.where(rel > 0, half, 0)
    n = jnp.abs(rel)
    nf = jnp.maximum(n, 1).astype(F32)
    large = max_exact + (jnp.log(nf / max_exact) / math.log(MAX_DISTANCE / max_exact)
                         * (half - max_exact)).astype(jnp.int32)
    large = jnp.minimum(large, half - 1)
    return ret + jnp.where(n < max_exact, n, large)


def _rel_bias(table, cq):
    koff = jnp.arange(WINDOW + cq) - WINDOW
    rel = koff[None, :] - jnp.arange(cq)[:, None]
    return jnp.transpose(table[_t5_bucket(rel)].astype(F32), (2, 0, 1))


def _pad_rows_front(a, rows):
    pad = jnp.zeros(a.shape[:1] + (rows - a.shape[1],) + a.shape[2:], a.dtype)
    return jnp.concatenate([pad, a], axis=1)


def _layer_params(l, w_in, ssd_conv_w, ssd_conv_b, ssd_dt_bias, ssd_a_log, ssd_d, ssd_norm_w, w_ssd_out,
                  attn_sinks, w_attn_out, sc_conv_w, w_sc_out, w_o, ln1_g, ln1_b,
                  router_w, router_b, w_up, b_up, w_down, b_down, ln2_g, ln2_b):
    wi = w_in[l]
    w_dt = wi[:, OFF_DT:OFF_Q]
    return {
        "w_main": jnp.concatenate([wi[:, :OFF_DT], wi[:, OFF_Q:OFF_K], wi[:, OFF_SCB:]], axis=1).astype(BF16),
        "w_kv": wi[:, OFF_K:OFF_SCB].astype(BF16),
        "w_dt": jnp.pad(w_dt, ((0, 0), (0, LANES - SSD_N_HEADS))).astype(BF16),
        "w_dtT": w_dt.T.astype(BF16),
        "conv_w": ssd_conv_w[l],
        "conv_b": ssd_conv_b[l][None, :],
        "dt_b128": jnp.pad(ssd_dt_bias[l], (0, LANES - SSD_N_HEADS))[None, :],
        "dt_bT": ssd_dt_bias[l][:, None],
        "a_log128": jnp.pad(ssd_a_log[l], (0, LANES - SSD_N_HEADS))[None, :],
        "expand": jnp.asarray(_head_expand_matrix(), BF16),
        "a_logT": ssd_a_log[l][:, None],
        "d_exp": jnp.repeat(ssd_d[l], SSD_HEAD_DIM)[None, :],
        "norm_w": ssd_norm_w[l][None, :],
        "w_ssd_out": w_ssd_out[l].astype(BF16),
        "sinks": attn_sinks[l],
        "w_att_out": w_attn_out[l].astype(BF16),
        "sc_w": sc_conv_w[l],
        "w_sc_out": w_sc_out[l].astype(BF16),
        "w_o": w_o[l].astype(BF16),
        "ln1_g": ln1_g[l][None, :],
        "ln1_b": ln1_b[l][None, :],
        "router_w": jnp.pad(router_w[l], ((0, 0), (0, LANES - N_EXPERTS))).astype(BF16),
        "router_b": jnp.pad(router_b[l], (0, LANES - N_EXPERTS))[None, :],
        "layer": l,
        "w_up": w_up,
        "b_up": b_up[:, :, None, :],
        "w_down": w_down,
        "b_down": b_down[:, :, None, :],
        "ln2_g": ln2_g[l][None, :],
        "ln2_b": ln2_b[l][None, :],
    }


def _trunk_layer(x2d, bt, s, p, bias, ssd_hist, ssd_state, sc_hist, kv_cache, cfg):
    t = bt * s
    h_main = _matmul(x2d, p["w_main"], BF16, cfg["tm_in"], cfg["tn_in"])
    h_kv = _matmul(x2d, p["w_kv"], F32, cfg["tm_in"], 2 * ATT_KV_DIM)
    k_new = h_kv[:, :ATT_KV_DIM].reshape(bt, s, ATT_KV_DIM)
    v_new = h_kv[:, ATT_KV_DIM:].reshape(bt, s, ATT_KV_DIM)

    if kv_cache is None:
        zpad = jnp.zeros((bt, WINDOW, ATT_KV_DIM), BF16)
        kall = jnp.concatenate([zpad, k_new.astype(BF16)], axis=1)
        vall = jnp.concatenate([zpad, v_new.astype(BF16)], axis=1)
    else:
        kall = jnp.concatenate([kv_cache[0].astype(BF16), k_new.astype(BF16)], axis=1)
        vall = jnp.concatenate([kv_cache[1].astype(BF16), v_new.astype(BF16)], axis=1)
    att = _attention(h_main, kall, vall, bias, p["sinks"], bt, s, cfg["cq"], cfg["n_sub"], kv_cache is None)

    if ssd_hist is None:
        hist8 = jnp.zeros((bt, SUBLANES, SSD_CONV_DIM), F32)
        st0 = jnp.zeros((bt, SSD_N_GROUPS, SSD_D_STATE, SSD_HEADS_PER_GROUP * SSD_HEAD_DIM), F32)
        schist8 = jnp.zeros((bt, SUBLANES, D_MODEL), F32)
    else:
        hist8 = _pad_rows_front(ssd_hist, SUBLANES)
        st0 = _state_to_kernel(ssd_state)
        schist8 = _pad_rows_front(sc_hist, SUBLANES)
    ssd_pre, st_out, hist_out = _ssd_grouped(x2d, h_main, hist8, st0, p, bt, s, cfg["cq"])

    x1r, ir, gates, cnt, schist_out = _merge(x2d, h_main, ssd_pre, att, schist8, p, bt, s, cfg["tm_merge"])
    x2 = _moe(x1r, ir, gates, cnt, p, t, cfg["tm_moe"], cfg["tmo"])

    n_keep = min(s, WINDOW)
    new_k = k_new[:, s - n_keep:].reshape(bt, n_keep, ATT_KV_HEADS, ATT_HEAD_DIM)
    new_v = v_new[:, s - n_keep:].reshape(bt, n_keep, ATT_KV_HEADS, ATT_HEAD_DIM)
    new_state = _state_from_kernel(st_out)
    new_hist = hist_out[:, SUBLANES - (SSD_CONV - 1):]
    new_sc_hist = schist_out[:, SUBLANES - (SC_WIDTH - 1):]
    return x2, new_k, new_v, new_state, new_hist, new_sc_hist


def _config(bt, s, prompt):
    t = bt * s
    if prompt:
        return {"tm_in": min(t, 2048), "tn_in": 1024, "cq": CHUNK, "n_sub": min(4, s // CHUNK),
                "tm_merge": min(s, 512), "tm_moe": min(t, 256), "tmo": min(512, max(64, t // 32))}
    return {"tm_in": min(t, 256), "tn_in": 1024, "cq": s, "n_sub": 1,
            "tm_merge": s, "tm_moe": min(t, 256), "tmo": 64}


def _forward(x_prompt, x_sample, cache_attn_k, cache_attn_v, state_ssd, state_ssd_conv, state_short_conv,
             rel_bias, layer_weights):
    bp, sp, _ = x_prompt.shape
    bd, sd, _ = x_sample.shape
    cfg_p = _config(bp, sp, True)
    cfg_d = _config(bd, sd, False)
    bias_p = _rel_bias(rel_bias, cfg_p["cq"])
    bias_d = _rel_bias(rel_bias, cfg_d["cq"])
    yp = x_prompt.reshape(bp * sp, D_MODEL)
    ys = x_sample.reshape(bd * sd, D_MODEL)
    outs_p, outs_d = [], []
    for l in range(DEPTH):
        p = _layer_params(l, *layer_weights)
        yp, *rest = _trunk_layer(yp, bp, sp, p, bias_p, None, None, None, None, cfg_p)
        outs_p.append(rest)
        cache = (cache_attn_k[l].reshape(bd, WINDOW, ATT_KV_DIM), cache_attn_v[l].reshape(bd, WINDOW, ATT_KV_DIM))
        ys, *rest = _trunk_layer(ys, bd, sd, p, bias_d, state_ssd_conv[l], state_ssd[l], state_short_conv[l],
                                 cache, cfg_d)
        outs_d.append(rest)
    stack = lambda outs, i: jnp.stack([o[i] for o in outs])
    return (yp.reshape(bp, sp, D_MODEL), ys.reshape(bd, sd, D_MODEL),
            stack(outs_p, 0), stack(outs_p, 1), stack(outs_p, 2), stack(outs_p, 3), stack(outs_p, 4),
            stack(outs_d, 0), stack(outs_d, 1), stack(outs_d, 2), stack(outs_d, 3), stack(outs_d, 4))


def kernel(x_prompt, x_sample, cache_attn_k, cache_attn_v, state_ssd, state_ssd_conv, state_short_conv, w_in, ssd_conv_w, ssd_conv_b, ssd_dt_bias, ssd_a_log, ssd_d, ssd_norm_w, w_ssd_out, attn_sinks, w_attn_out, rel_bias, sc_conv_w, w_sc_out, w_o, ln1_g, ln1_b, router_w, router_b, w_up, b_up, w_down, b_down, ln2_g, ln2_b):
    layer_weights = (w_in, ssd_conv_w, ssd_conv_b, ssd_dt_bias, ssd_a_log, ssd_d, ssd_norm_w, w_ssd_out,
                     attn_sinks, w_attn_out, sc_conv_w, w_sc_out, w_o, ln1_g, ln1_b,
                     router_w, router_b, w_up, b_up, w_down, b_down, ln2_g, ln2_b)
    return _forward(x_prompt, x_sample, cache_attn_k, cache_attn_v, state_ssd, state_ssd_conv, state_short_conv,
                    rel_bias, layer_weights)
```

```python
import functools
import math

import jax
import jax.numpy as jnp
import numpy as np
from jax import lax
from jax.experimental import pallas as pl
from jax.experimental.pallas import tpu as pltpu

F32 = jnp.float32
BF16 = jnp.bfloat16

D_MODEL = 1024
DEPTH = 2
CHUNK = 64
SSD_D_INNER = 2 * D_MODEL
SSD_HEAD_DIM = 64
SSD_N_HEADS = SSD_D_INNER // SSD_HEAD_DIM
SSD_N_GROUPS = 8
SSD_D_STATE = 128
SSD_CONV = 4
SSD_GN = SSD_N_GROUPS * SSD_D_STATE
SSD_CONV_DIM = SSD_D_INNER + 2 * SSD_GN
SSD_HEADS_PER_GROUP = SSD_N_HEADS // SSD_N_GROUPS
SSD_NORM_GROUP = SSD_D_INNER // SSD_N_GROUPS
ATT_HEAD_DIM = 64
ATT_HEADS = D_MODEL // ATT_HEAD_DIM
ATT_KV_HEADS = ATT_HEADS // 4
ATT_GROUP = ATT_HEADS // ATT_KV_HEADS
ATT_KV_DIM = ATT_KV_HEADS * ATT_HEAD_DIM
WINDOW = 128
N_BUCKETS = 32
MAX_DISTANCE = 128
SC_WIDTH = 3
N_EXPERTS = 32
TOP_K = 4
D_FF = D_MODEL
SWIGLU_LIMIT = 7.0
SWIGLU_ALPHA = 1.702
LN_EPS = 1e-5
ALPHA = (2.0 * DEPTH) ** 0.25

LANES = 128
SUBLANES = 8
ROW_TILES = D_MODEL // LANES
VMEM_LIMIT = 56 * 1024 * 1024
ZERO_ROWS = 16
ROUTE_TILE = 256
SEG_ALIGN = 8

COL_Z, COL_XS, COL_BC = 0, 1, 2
COL_Q, COL_SCB, COL_SCC, COL_SCH, COL_GSSD, COL_GATT, COL_GSC = 6, 7, 8, 9, 10, 11, 12
N_MAIN = 13 * D_MODEL
OFF_DT = SSD_D_INNER + SSD_CONV_DIM
OFF_Q = OFF_DT + SSD_N_HEADS
OFF_K = OFF_Q + D_MODEL
OFF_SCB = OFF_K + 2 * ATT_KV_DIM


def _sigmoid(v):
    return 0.5 * jnp.tanh(0.5 * v) + 0.5


def _cparams(*sem):
    return pltpu.CompilerParams(dimension_semantics=sem, vmem_limit_bytes=VMEM_LIMIT)


def _matmul_kernel(x_ref, w_ref, o_ref, xb_ref):
    @pl.when(pl.program_id(1) == 0)
    def _():
        xb_ref[...] = x_ref[...].astype(BF16)

    o_ref[...] = jnp.dot(xb_ref[...], w_ref[...], preferred_element_type=F32).astype(o_ref.dtype)


def _matmul(x, w, out_dtype, tm, tn):
    t, k = x.shape
    n = w.shape[1]
    return pl.pallas_call(
        _matmul_kernel,
        grid=(t // tm, n // tn),
        in_specs=[pl.BlockSpec((tm, k), lambda i, j: (i, 0)),
                  pl.BlockSpec((k, tn), lambda i, j: (0, j))],
        out_specs=pl.BlockSpec((tm, tn), lambda i, j: (i, j)),
        out_shape=jax.ShapeDtypeStruct((t, n), out_dtype),
        scratch_shapes=[pltpu.VMEM((tm, k), BF16)],
        compiler_params=_cparams("arbitrary", "arbitrary"),
        name="in_proj",
    )(x, w)


def _ssd_kernel(x_ref, z_ref, xs_ref, bc_ref, hist_ref, st0_ref,
                wdt_ref, wdtT_ref, convw_ref, convb_ref, dtb_ref, dtbT_ref, alog_ref, alogT_ref,
                dexp_ref, normw_ref,
                y_ref, stout_ref, histout_ref,
                ext_ref, st_ref, ybuf_ref, *, L, nc):
    c = pl.program_id(1)

    @pl.when(c == 0)
    def _():
        ext_ref[0:SUBLANES, :] = hist_ref[0]
        st_ref[...] = st0_ref[0]

    xbc = jnp.concatenate([xs_ref[...], bc_ref[...]], axis=1).astype(F32)
    ext_ref[SUBLANES:SUBLANES + L, :] = xbc
    base = SUBLANES - (SSD_CONV - 1)
    conv = convb_ref[...]
    for j in range(SSD_CONV):
        conv = conv + ext_ref[base + j:base + j + L, :] * convw_ref[j:j + 1, :]
    tail = ext_ref[L:L + SUBLANES, :]
    ext_ref[0:SUBLANES, :] = tail
    histout_ref[0] = tail
    xc = conv * _sigmoid(conv)
    xs_c = xc[:, :SSD_D_INNER]
    b_m = xc[:, SSD_D_INNER:SSD_D_INNER + SSD_GN]
    c_m = xc[:, SSD_D_INNER + SSD_GN:]

    xb = x_ref[...].astype(BF16)
    dtr = jnp.dot(xb, wdt_ref[...], preferred_element_type=F32)[:, :SSD_N_HEADS]
    dtrT = lax.dot_general(wdtT_ref[...], xb, (((1,), (1,)), ((), ())),
                           preferred_element_type=F32)

    def softplus(v):
        return jnp.maximum(v, 0.0) + jnp.log1p(jnp.exp(-jnp.abs(v)))

    dt = softplus(dtr + dtb_ref[...])
    dtT = softplus(dtrT + dtbT_ref[...])
    adt = dt * (-jnp.exp(alog_ref[...]))
    adtT = dtT * (-jnp.exp(alogT_ref[...]))
    ri = lax.broadcasted_iota(jnp.int32, (L, L), 0)
    ci = lax.broadcasted_iota(jnp.int32, (L, L), 1)
    tri = ri >= ci
    acs = jnp.dot(tri.astype(F32), adt, precision=lax.Precision.HIGHEST,
                  preferred_element_type=F32)
    acsT = jnp.dot(adtT, (ri <= ci).astype(F32), precision=lax.Precision.HIGHEST,
                   preferred_element_type=F32)
    totT = jnp.sum(adtT, axis=1, keepdims=True)
    eacs = jnp.exp(acs)
    wT = jnp.exp(totT - acsT) * dtT
    cdT = jnp.exp(totT)

    for g in range(SSD_N_GROUPS):
        bg = b_m[:, g * SSD_D_STATE:(g + 1) * SSD_D_STATE]
        cg_b = c_m[:, g * SSD_D_STATE:(g + 1) * SSD_D_STATE].astype(BF16)
        cb = lax.dot_general(cg_b, bg.astype(BF16), (((1,), (1,)), ((), ())),
                             preferred_element_type=F32)
        if L < LANES:
            bg = jnp.concatenate([bg, jnp.zeros((LANES - L, SSD_D_STATE), F32)], axis=0)
        bgT = bg.T[:, :L]
        for r in range(SSD_HEADS_PER_GROUP):
            h = g * SSD_HEADS_PER_GROUP + r
            col = acs[:, h:h + 1]
            row = acsT[h:h + 1, :]
            decay = jnp.exp(jnp.where(tri, col - row, -jnp.inf))
            m = (cb * decay * dtT[h:h + 1, :]).astype(BF16)
            xh_b = xs_c[:, h * SSD_HEAD_DIM:(h + 1) * SSD_HEAD_DIM].astype(BF16)
            st = st_ref[h]
            yd = jnp.dot(m, xh_b, preferred_element_type=F32)
            yo = jnp.dot(cg_b, st.astype(BF16), preferred_element_type=F32)
            ybuf_ref[:, h * SSD_HEAD_DIM:(h + 1) * SSD_HEAD_DIM] = yd + yo * eacs[:, h:h + 1]
            wh = (bgT * wT[h:h + 1, :]).astype(BF16)
            stc = jnp.dot(wh, xh_b, preferred_element_type=F32)
            st_ref[h] = st * cdT[h:h + 1, :] + stc

    y = ybuf_ref[...] + dexp_ref[...] * xs_c
    z = z_ref[...].astype(F32)
    y = y * (z * _sigmoid(z))
    for g in range(SSD_N_GROUPS):
        sl = slice(g * SSD_NORM_GROUP, (g + 1) * SSD_NORM_GROUP)
        blk = y[:, sl]
        ms = jnp.mean(blk * blk, axis=-1, keepdims=True)
        y_ref[:, sl] = (blk * lax.rsqrt(ms + LN_EPS) * normw_ref[:, sl]).astype(y_ref.dtype)

    @pl.when(c == nc - 1)
    def _():
        stout_ref[0] = st_ref[...]


def _ssd(x2d, h_main, hist8, st0, p, bt, s, L):
    nc = s // L
    t = bt * s
    w2 = 2 * D_MODEL
    row = lambda b, c: (b * nc + c, 0)
    const2 = lambda b, c: (0, 0)
    kern = functools.partial(_ssd_kernel, L=L, nc=nc)
    return pl.pallas_call(
        kern,
        grid=(bt, nc),
        in_specs=[
            pl.BlockSpec((L, D_MODEL), row),
            pl.BlockSpec((L, w2), lambda b, c: (b * nc + c, COL_Z)),
            pl.BlockSpec((L, w2), lambda b, c: (b * nc + c, COL_XS)),
            pl.BlockSpec((L, w2), lambda b, c: (b * nc + c, COL_BC)),
            pl.BlockSpec((1, SUBLANES, SSD_CONV_DIM), lambda b, c: (b, 0, 0)),
            pl.BlockSpec((1, SSD_N_HEADS, SSD_D_STATE, SSD_HEAD_DIM), lambda b, c: (b, 0, 0, 0)),
            pl.BlockSpec((D_MODEL, LANES), const2),
            pl.BlockSpec((SSD_N_HEADS, D_MODEL), const2),
            pl.BlockSpec((SSD_CONV, SSD_CONV_DIM), const2),
            pl.BlockSpec((1, SSD_CONV_DIM), const2),
            pl.BlockSpec((1, SSD_N_HEADS), const2),
            pl.BlockSpec((SSD_N_HEADS, 1), const2),
            pl.BlockSpec((1, SSD_N_HEADS), const2),
            pl.BlockSpec((SSD_N_HEADS, 1), const2),
            pl.BlockSpec((1, SSD_D_INNER), const2),
            pl.BlockSpec((1, SSD_D_INNER), const2),
        ],
        out_specs=[
            pl.BlockSpec((L, SSD_D_INNER), row),
            pl.BlockSpec((1, SSD_N_HEADS, SSD_D_STATE, SSD_HEAD_DIM), lambda b, c: (b, 0, 0, 0)),
            pl.BlockSpec((1, SUBLANES, SSD_CONV_DIM), lambda b, c: (b, 0, 0)),
        ],
        out_shape=[
            jax.ShapeDtypeStruct((t, SSD_D_INNER), BF16),
            jax.ShapeDtypeStruct((bt, SSD_N_HEADS, SSD_D_STATE, SSD_HEAD_DIM), F32),
            jax.ShapeDtypeStruct((bt, SUBLANES, SSD_CONV_DIM), F32),
        ],
        scratch_shapes=[
            pltpu.VMEM((L + SUBLANES, SSD_CONV_DIM), F32),
            pltpu.VMEM((SSD_N_HEADS, SSD_D_STATE, SSD_HEAD_DIM), F32),
            pltpu.VMEM((L, SSD_D_INNER), F32),
        ],
        compiler_params=_cparams("arbitrary", "arbitrary"),
        name="ssd_scan",
    )(x2d, h_main, h_main, h_main, hist8, st0,
      p["w_dt"], p["w_dtT"], p["conv_w"], p["conv_b"], p["dt_b"], p["dt_bT"], p["a_log"], p["a_logT"],
      p["d_exp"], p["norm_w"])


def _ssd_group_kernel(x_ref, z_ref, xs_ref, bc_ref, hist_ref, st0_ref,
                      wdt_ref, wdtT_ref, convw_ref, convb_ref, dtb_ref, dtbT_ref, alog_ref, alogT_ref,
                      dexp_ref, normw_ref, expand_ref,
                      y_ref, stout_ref, histout_ref,
                      ext_ref, xrow_ref, st_ref, ynat_ref, *, L, nc):
    c = pl.program_id(1)
    stride = L // SUBLANES
    gw = SSD_HEADS_PER_GROUP * SSD_HEAD_DIM

    n_cb = SSD_CONV_DIM // LANES
    lanes = lambda cb: slice(cb * LANES, (cb + 1) * LANES)

    def strided_rows(ref, start):
        return jnp.concatenate([ref[cb, pl.ds(start, SUBLANES, stride=stride), :] for cb in range(ref.shape[0])],
                               axis=1)

    @pl.when(c == 0)
    def _():
        for cb in range(n_cb):
            ext_ref[cb, 0:SUBLANES, :] = hist_ref[0, :, lanes(cb)]
        st_ref[...] = st0_ref[0]

    for cb in range(n_cb):
        src = xs_ref if cb < SSD_D_INNER // LANES else bc_ref
        ext_ref[cb, SUBLANES:SUBLANES + L, :] = src[:, lanes(cb % (SSD_D_INNER // LANES))].astype(F32)
    base = SUBLANES - (SSD_CONV - 1)
    taps = {o: strided_rows(ext_ref, o) for o in range(base, base + stride + SSD_CONV - 1)}
    pieces = []
    for r in range(stride):
        conv = convb_ref[...] + taps[base + r] * convw_ref[0:1, :]
        for j in range(1, SSD_CONV):
            conv = conv + taps[base + r + j] * convw_ref[j:j + 1, :]
        pieces.append(conv * _sigmoid(conv))
    xc = jnp.concatenate(pieces, axis=0)
    for cb in range(n_cb):
        tail = ext_ref[cb, L:L + SUBLANES, :]
        ext_ref[cb, 0:SUBLANES, :] = tail
        histout_ref[0, :, lanes(cb)] = tail
    xs_c = xc[:, :SSD_D_INNER]
    b_m = xc[:, SSD_D_INNER:SSD_D_INNER + SSD_GN]
    c_m = xc[:, SSD_D_INNER + SSD_GN:]

    for cb in range(D_MODEL // LANES):
        xrow_ref[cb] = x_ref[:, lanes(cb)]
    xb = jnp.concatenate([strided_rows(xrow_ref, r) for r in range(stride)], axis=0).astype(BF16)
    lane_h = lax.broadcasted_iota(jnp.int32, (1, LANES), 1) < SSD_N_HEADS
    dtr = jnp.dot(xb, wdt_ref[...], preferred_element_type=F32)
    dtrT = lax.dot_general(wdtT_ref[...], xb, (((1,), (1,)), ((), ())),
                           preferred_element_type=F32)

    def softplus(v):
        return jnp.maximum(v, 0.0) + jnp.log1p(jnp.exp(-jnp.abs(v)))

    dt = jnp.where(lane_h, softplus(dtr + dtb_ref[...]), 0.0)
    dtT = softplus(dtrT + dtbT_ref[...])
    adt = dt * (-jnp.exp(alog_ref[...]))
    adtT = dtT * (-jnp.exp(alogT_ref[...]))
    ri = lax.broadcasted_iota(jnp.int32, (L, L), 0)
    ci = lax.broadcasted_iota(jnp.int32, (L, L), 1)
    time_r = stride * (ri % SUBLANES) + ri // SUBLANES
    time_c = stride * (ci % SUBLANES) + ci // SUBLANES
    tri = time_r >= time_c
    acs = jnp.dot(tri.astype(F32), adt, precision=lax.Precision.HIGHEST,
                  preferred_element_type=F32)
    acsT = jnp.dot(adtT, (time_r <= time_c).astype(F32), precision=lax.Precision.HIGHEST,
                   preferred_element_type=F32)
    tot = jnp.sum(adt, axis=0, keepdims=True)

    cols = jnp.concatenate([acs, jnp.exp(acs), jnp.exp(tot - acs) * dt,
                            jnp.broadcast_to(jnp.exp(tot), (SUBLANES, LANES))], axis=0)
    cols = jnp.where(lane_h, cols, 0.0)
    hi = cols.astype(BF16).astype(F32)
    mid = (cols - hi).astype(BF16).astype(F32)
    lo = (cols - hi - mid).astype(BF16).astype(F32)
    packed = (hi + pltpu.roll(mid, SSD_N_HEADS, 1) + pltpu.roll(lo, 2 * SSD_N_HEADS, 1)).astype(BF16)
    wide = jnp.dot(packed, expand_ref[...], preferred_element_type=F32)
    acs_w = wide[0:L]
    eacs_w = wide[L:2 * L]
    w_w = wide[2 * L:3 * L]
    cd_w = wide[3 * L:3 * L + 1]

    for g in range(SSD_N_GROUPS):
        gs = slice(g * gw, (g + 1) * gw)
        bg = b_m[:, g * SSD_D_STATE:(g + 1) * SSD_D_STATE]
        cg_b = c_m[:, g * SSD_D_STATE:(g + 1) * SSD_D_STATE].astype(BF16)
        cb = lax.dot_general(cg_b, bg.astype(BF16), (((1,), (1,)), ((), ())),
                             preferred_element_type=F32)
        if L < LANES:
            bg = jnp.concatenate([bg, jnp.zeros((LANES - L, SSD_D_STATE), F32)], axis=0)
        bgT_b = bg.T[:, :L].astype(BF16)
        xs_g = xs_c[:, gs]
        st = st_ref[g]
        yo = jnp.dot(cg_b, st.astype(BF16), preferred_element_type=F32)
        stc = jnp.dot(bgT_b, (xs_g * w_w[:, gs]).astype(BF16), preferred_element_type=F32)
        st_ref[g] = st * cd_w[:, gs] + stc
        yds = []
        for r in range(SSD_HEADS_PER_GROUP):
            h = g * SSD_HEADS_PER_GROUP + r
            col = acs_w[:, h * SSD_HEAD_DIM:h * SSD_HEAD_DIM + L]
            decay = jnp.exp(jnp.where(tri, col - acsT[h:h + 1, :], -jnp.inf))
            m = (cb * decay * dtT[h:h + 1, :]).astype(BF16)
            yds.append(jnp.dot(m, xs_g[:, r * SSD_HEAD_DIM:(r + 1) * SSD_HEAD_DIM].astype(BF16),
                               preferred_element_type=F32))
        y_g = jnp.concatenate(yds, axis=1) + yo * eacs_w[:, gs] + dexp_ref[:, gs] * xs_g
        for r in range(stride):
            for cb in range(gw // LANES):
                ynat_ref[g * (gw // LANES) + cb, pl.ds(r, SUBLANES, stride=stride), :] = (
                    y_g[r * SUBLANES:(r + 1) * SUBLANES, lanes(cb)])

    z = z_ref[...].astype(F32)
    y = jnp.concatenate([ynat_ref[cb] for cb in range(SSD_D_INNER // LANES)], axis=1) * (z * _sigmoid(z))
    for g in range(SSD_N_GROUPS):
        sl = slice(g * SSD_NORM_GROUP, (g + 1) * SSD_NORM_GROUP)
        blk = y[:, sl]
        ms = jnp.mean(blk * blk, axis=-1, keepdims=True)
        y_ref[:, sl] = (blk * lax.rsqrt(ms + LN_EPS) * normw_ref[:, sl]).astype(y_ref.dtype)

    @pl.when(c == nc - 1)
    def _():
        stout_ref[0] = st_ref[...]


def _ssd_grouped(x2d, h_main, hist8, st0, p, bt, s, L):
    assert L % SUBLANES == 0 and L <= SSD_HEAD_DIM
    nc = s // L
    t = bt * s
    w2 = 2 * D_MODEL
    gw = SSD_HEADS_PER_GROUP * SSD_HEAD_DIM
    row = lambda b, c: (b * nc + c, 0)
    const2 = lambda b, c: (0, 0)
    state_spec = pl.BlockSpec((1, SSD_N_GROUPS, SSD_D_STATE, gw), lambda b, c: (b, 0, 0, 0))
    kern = functools.partial(_ssd_group_kernel, L=L, nc=nc)
    return pl.pallas_call(
        kern,
        grid=(bt, nc),
        in_specs=[
            pl.BlockSpec((L, D_MODEL), row),
            pl.BlockSpec((L, w2), lambda b, c: (b * nc + c, COL_Z)),
            pl.BlockSpec((L, w2), lambda b, c: (b * nc + c, COL_XS)),
            pl.BlockSpec((L, w2), lambda b, c: (b * nc + c, COL_BC)),
            pl.BlockSpec((1, SUBLANES, SSD_CONV_DIM), lambda b, c: (b, 0, 0)),
            state_spec,
            pl.BlockSpec((D_MODEL, LANES), const2),
            pl.BlockSpec((SSD_N_HEADS, D_MODEL), const2),
            pl.BlockSpec((SSD_CONV, SSD_CONV_DIM), const2),
            pl.BlockSpec((1, SSD_CONV_DIM), const2),
            pl.BlockSpec((1, LANES), const2),
            pl.BlockSpec((SSD_N_HEADS, 1), const2),
            pl.BlockSpec((1, LANES), const2),
            pl.BlockSpec((SSD_N_HEADS, 1), const2),
            pl.BlockSpec((1, SSD_D_INNER), const2),
            pl.BlockSpec((1, SSD_D_INNER), const2),
            pl.BlockSpec((LANES, SSD_D_INNER), const2),
        ],
        out_specs=[
            pl.BlockSpec((L, SSD_D_INNER), row),
            state_spec,
            pl.BlockSpec((1, SUBLANES, SSD_CONV_DIM), lambda b, c: (b, 0, 0)),
        ],
        out_shape=[
            jax.ShapeDtypeStruct((t, SSD_D_INNER), BF16),
            jax.ShapeDtypeStruct((bt, SSD_N_GROUPS, SSD_D_STATE, gw), F32),
            jax.ShapeDtypeStruct((bt, SUBLANES, SSD_CONV_DIM), F32),
        ],
        scratch_shapes=[
            pltpu.VMEM((SSD_CONV_DIM // LANES, L + SUBLANES, LANES), F32),
            pltpu.VMEM((D_MODEL // LANES, L, LANES), F32),
            pltpu.VMEM((SSD_N_GROUPS, SSD_D_STATE, gw), F32),
            pltpu.VMEM((SSD_D_INNER // LANES, L, LANES), F32),
        ],
        compiler_params=_cparams("arbitrary", "arbitrary"),
        name="ssd_scan",
    )(x2d, h_main, h_main, h_main, hist8, st0,
      p["w_dt"], p["w_dtT"], p["conv_w"], p["conv_b"], p["dt_b128"], p["dt_bT"], p["a_log128"], p["a_logT"],
      p["d_exp"], p["norm_w"], p["expand"])


def _head_expand_matrix():
    e = np.zeros((LANES, SSD_D_INNER), np.float32)
    for part in range(3):
        for h in range(SSD_N_HEADS):
            e[part * SSD_N_HEADS + h, h * SSD_HEAD_DIM:(h + 1) * SSD_HEAD_DIM] = 1.0
    return e


def _state_to_kernel(state):
    b = state.shape[0]
    s5 = state.reshape(b, SSD_N_GROUPS, SSD_HEADS_PER_GROUP, SSD_HEAD_DIM, SSD_D_STATE)
    return jnp.transpose(s5, (0, 1, 4, 2, 3)).reshape(b, SSD_N_GROUPS, SSD_D_STATE,
                                                       SSD_HEADS_PER_GROUP * SSD_HEAD_DIM)


def _state_from_kernel(st):
    b = st.shape[0]
    s5 = st.reshape(b, SSD_N_GROUPS, SSD_D_STATE, SSD_HEADS_PER_GROUP, SSD_HEAD_DIM)
    return jnp.transpose(s5, (0, 1, 3, 4, 2)).reshape(b, SSD_N_HEADS, SSD_HEAD_DIM, SSD_D_STATE)


def _attn_kernel(q_ref, k_ref, v_ref, bias_ref, sink_ref, o_ref, *, cq, n_sub, mask_prefix):
    step = pl.program_id(1)
    kw = WINDOW + cq
    scale = ATT_HEAD_DIM ** -0.5
    gw = ATT_GROUP * ATT_HEAD_DIM
    pairs = [(sub, hk) for sub in range(n_sub) for hk in range(ATT_KV_HEADS)]
    scs, vhs = {}, {}
    for sub in range(n_sub):
        r0 = pl.multiple_of((step * n_sub + sub) * cq, cq)
        kk = k_ref[0, pl.ds(r0, kw), :]
        vv = v_ref[0, pl.ds(r0, kw), :]
        qq = q_ref[sub * cq:(sub + 1) * cq, :]
        if mask_prefix:
            valid = (r0 + lax.broadcasted_iota(jnp.int32, (1, kw), 1)) >= WINDOW
        for hk in range(ATT_KV_HEADS):
            q4 = jnp.concatenate(
                [qq[:, hk * gw + g * ATT_HEAD_DIM:hk * gw + (g + 1) * ATT_HEAD_DIM] for g in range(ATT_GROUP)],
                axis=0)
            kh = kk[:, hk * ATT_HEAD_DIM:(hk + 1) * ATT_HEAD_DIM]
            sc = lax.dot_general(q4, kh, (((1,), (1,)), ((), ())), preferred_element_type=F32)
            sc = sc * scale + bias_ref[hk]
            if mask_prefix:
                sc = jnp.where(valid, sc, -1e30)
            scs[sub, hk] = sc
            vhs[sub, hk] = vv[:, hk * ATT_HEAD_DIM:(hk + 1) * ATT_HEAD_DIM]
    es, rs = {}, {}
    for sub, hk in pairs:
        sink = sink_ref[hk]
        mx = jnp.maximum(jnp.max(scs[sub, hk], axis=-1, keepdims=True), sink)
        e = jnp.exp(scs[sub, hk] - mx)
        den = jnp.sum(e, axis=-1, keepdims=True) + jnp.exp(sink - mx)
        es[sub, hk] = e.astype(BF16)
        rs[sub, hk] = 1.0 / den
    for sub, hk in pairs:
        o4 = jnp.dot(es[sub, hk], vhs[sub, hk], preferred_element_type=F32) * rs[sub, hk]
        o_ref[sub * cq:(sub + 1) * cq, hk * gw:(hk + 1) * gw] = jnp.concatenate(
            [o4[g * cq:(g + 1) * cq, :] for g in range(ATT_GROUP)], axis=1).astype(o_ref.dtype)


def _attention(h_main, kall, vall, bias, sinks, bt, s, cq, n_sub, mask_prefix):
    tq = cq * n_sub
    nq = s // tq
    kw = WINDOW + cq
    bias4 = bias.reshape(ATT_KV_HEADS, ATT_GROUP * cq, kw)
    sink4 = jnp.repeat(sinks, cq).reshape(ATT_KV_HEADS, ATT_GROUP * cq, 1)
    kern = functools.partial(_attn_kernel, cq=cq, n_sub=n_sub, mask_prefix=mask_prefix)
    return pl.pallas_call(
        kern,
        grid=(bt, nq),
        in_specs=[
            pl.BlockSpec((tq, D_MODEL), lambda b, i: (b * nq + i, COL_Q)),
            pl.BlockSpec((1, WINDOW + s, ATT_KV_DIM), lambda b, i: (b, 0, 0)),
            pl.BlockSpec((1, WINDOW + s, ATT_KV_DIM), lambda b, i: (b, 0, 0)),
            pl.BlockSpec((ATT_KV_HEADS, ATT_GROUP * cq, kw), lambda b, i: (0, 0, 0)),
            pl.BlockSpec((ATT_KV_HEADS, ATT_GROUP * cq, 1), lambda b, i: (0, 0, 0)),
        ],
        out_specs=pl.BlockSpec((tq, D_MODEL), lambda b, i: (b * nq + i, 0)),
        out_shape=jax.ShapeDtypeStruct((bt * s, D_MODEL), BF16),
        compiler_params=_cparams("arbitrary", "arbitrary"),
        name="swa_attention",
    )(h_main, kall, vall, bias4, sink4)


def _merge_kernel(x_ref, ssd_ref, att_ref, scb_ref, scc_ref, sch_ref, gssd_ref, gatt_ref, gsc_ref, hist_ref,
                  wssd_ref, watt_ref, wsc_ref, wo_ref, scw_ref, lng_ref, lnb_ref, rw_ref, rb_ref,
                  x1r_ref, ir_ref, gate_ref, cnt_ref, histout_ref, stat_ref,
                  ext_ref, base_ref, *, tm, tiles_per_batch):
    i = pl.program_id(0)

    @pl.when(i == 0)
    def _():
        base_ref[...] = jnp.zeros_like(base_ref)

    @pl.when(i % tiles_per_batch == 0)
    def _():
        ext_ref[0:SUBLANES, :] = hist_ref[0]

    u = scc_ref[...].astype(F32) * sch_ref[...].astype(F32)
    ext_ref[SUBLANES:SUBLANES + tm, :] = u
    off = SUBLANES - (SC_WIDTH - 1)
    conv = ext_ref[off:off + tm, :] * scw_ref[0:1, :]
    for j in range(1, SC_WIDTH):
        conv = conv + ext_ref[off + j:off + j + tm, :] * scw_ref[j:j + 1, :]
    tail = ext_ref[tm:tm + SUBLANES, :]
    ext_ref[0:SUBLANES, :] = tail
    histout_ref[0] = tail
    sc_pre = (scb_ref[...].astype(F32) * conv).astype(BF16)

    y_sc = jnp.dot(sc_pre, wsc_ref[...], preferred_element_type=F32)
    y_ssd = jnp.dot(ssd_ref[...], wssd_ref[...], preferred_element_type=F32)
    y_att = jnp.dot(att_ref[...], watt_ref[...], preferred_element_type=F32)
    merged = (_sigmoid(gssd_ref[...].astype(F32)) * y_ssd
              + _sigmoid(gatt_ref[...].astype(F32)) * y_att
              + _sigmoid(gsc_ref[...].astype(F32)) * y_sc)
    v = ALPHA * x_ref[...] + jnp.dot(merged.astype(BF16), wo_ref[...], preferred_element_type=F32)
    mu = jnp.mean(v, axis=-1, keepdims=True)
    vc = v - mu
    var = jnp.mean(vc * vc, axis=-1, keepdims=True)
    x1 = vc * lax.rsqrt(var + LN_EPS) * lng_ref[...] + lnb_ref[...]
    for j in range(ROW_TILES):
        x1r_ref[pl.ds(j, tm, stride=ROW_TILES), :] = x1[:, j * LANES:(j + 1) * LANES]

    rt = min(tm, ROUTE_TILE)
    lane = lax.broadcasted_iota(jnp.int32, (rt, LANES), 1)
    lane1 = lax.broadcasted_iota(jnp.int32, (1, LANES), 1)
    row8 = lax.broadcasted_iota(jnp.int32, (SUBLANES, LANES), 0)
    ri = lax.broadcasted_iota(jnp.int32, (rt, rt), 0)
    ci = lax.broadcasted_iota(jnp.int32, (rt, rt), 1)
    ltri = (ri > ci).astype(BF16)
    gi = lax.broadcasted_iota(jnp.int32, (LANES, LANES), 0)
    gj = lax.broadcasted_iota(jnp.int32, (LANES, LANES), 1)
    earlier_expert = jnp.logical_and(gi < gj, gi // N_EXPERTS == gj // N_EXPERTS).astype(BF16)
    logits_all = jnp.dot(x1.astype(BF16), rw_ref[...], preferred_element_type=F32) + rb_ref[...]
    for sub in range(tm // rt):
        rows = slice(sub * rt, (sub + 1) * rt)
        work = jnp.where(lane < N_EXPERTS, logits_all[rows], -jnp.inf)
        vals, idxs = [], []
        onehot = jnp.zeros((rt, LANES), F32)
        for k in range(TOP_K):
            mv = jnp.max(work, axis=-1, keepdims=True)
            mi = jnp.min(jnp.where(work == mv, lane, LANES), axis=-1, keepdims=True)
            vals.append(mv)
            idxs.append(mi)
            work = jnp.where(lane == mi, -jnp.inf, work)
            onehot = onehot + (lane == mi + k * N_EXPERTS).astype(F32)
        es = [jnp.exp(vk - vals[0]) for vk in vals]
        den = es[0] + es[1] + es[2] + es[3]
        prefix = jnp.dot(ltri, onehot.astype(BF16), preferred_element_type=F32)
        cnt = jnp.sum(onehot, axis=0, keepdims=True)
        base = base_ref[0:1, :]
        within = jnp.zeros((1, LANES), F32)
        tot = cnt
        for sh in range(1, TOP_K):
            rolled = pltpu.roll(cnt, sh * N_EXPERTS, 1)
            within = within + jnp.where(lane1 >= sh * N_EXPERTS, rolled, 0.0)
            tot = tot + rolled
        chunks = jnp.floor((tot + (SEG_ALIGN - 1)) * (1.0 / SEG_ALIGN))
        seg_start = SEG_ALIGN * jnp.dot(jnp.broadcast_to(chunks, (SUBLANES, LANES)).astype(BF16), earlier_expert,
                                        preferred_element_type=F32)[0:1, :]
        rank_all = prefix + (within + base)
        loc_all = prefix + (within + seg_start)
        ir = jnp.zeros((rt, LANES), jnp.int32)
        gt = jnp.zeros((rt, LANES), F32)
        for k in range(TOP_K):
            sel = lane == idxs[k] + k * N_EXPERTS
            rk = jnp.sum(jnp.where(sel, rank_all, 0.0), axis=-1, keepdims=True).astype(jnp.int32)
            lc = jnp.sum(jnp.where(sel, loc_all, 0.0), axis=-1, keepdims=True).astype(jnp.int32)
            ir = jnp.where(lane == k, idxs[k], ir)
            ir = jnp.where(lane == TOP_K + k, rk, ir)
            ir = jnp.where(lane == 2 * TOP_K + k, lc, ir)
            gt = jnp.where(lane == k, es[k] / den, gt)
        ir_ref[rows, :] = ir
        gate_ref[rows, :] = gt
        stat_ref[sub * SUBLANES:(sub + 1) * SUBLANES, :] = jnp.where(
            row8 == 0, tot, jnp.where(row8 == 1, seg_start, jnp.where(row8 == 2, base, 0.0)))
        base_ref[...] = jnp.broadcast_to(base + tot, base_ref.shape)
    cnt_ref[...] = base_ref[...]


def _merge(x2d, h_main, ssd_pre, att, schist8, p, bt, s, tm):
    t = bt * s
    nt = t // tm
    tiles_per_batch = s // tm
    row = lambda i: (i, 0)
    const2 = lambda i: (0, 0)
    hcol = lambda cidx: pl.BlockSpec((tm, D_MODEL), lambda i: (i, cidx))
    kern = functools.partial(_merge_kernel, tm=tm, tiles_per_batch=tiles_per_batch)
    return pl.pallas_call(
        kern,
        grid=(nt,),
        in_specs=[
            pl.BlockSpec((tm, D_MODEL), row),
            pl.BlockSpec((tm, SSD_D_INNER), row),
            pl.BlockSpec((tm, D_MODEL), row),
            hcol(COL_SCB), hcol(COL_SCC), hcol(COL_SCH), hcol(COL_GSSD), hcol(COL_GATT), hcol(COL_GSC),
            pl.BlockSpec((1, SUBLANES, D_MODEL), lambda i: (i // tiles_per_batch, 0, 0)),
            pl.BlockSpec((SSD_D_INNER, D_MODEL), const2, pipeline_mode=pl.Buffered(1)),
            pl.BlockSpec((D_MODEL, D_MODEL), const2, pipeline_mode=pl.Buffered(1)),
            pl.BlockSpec((D_MODEL, D_MODEL), const2, pipeline_mode=pl.Buffered(1)),
            pl.BlockSpec((D_MODEL, D_MODEL), const2, pipeline_mode=pl.Buffered(1)),
            pl.BlockSpec((SC_WIDTH, D_MODEL), const2),
            pl.BlockSpec((1, D_MODEL), const2),
            pl.BlockSpec((1, D_MODEL), const2),
            pl.BlockSpec((D_MODEL, LANES), const2),
            pl.BlockSpec((1, LANES), const2),
        ],
        out_specs=[
            pl.BlockSpec((tm * ROW_TILES, LANES), row),
            pl.BlockSpec((tm, LANES), row),
            pl.BlockSpec((tm, LANES), row),
            pl.BlockSpec((SUBLANES, LANES), const2),
            pl.BlockSpec((1, SUBLANES, D_MODEL), lambda i: (i // tiles_per_batch, 0, 0)),
            pl.BlockSpec((tm // min(tm, ROUTE_TILE) * SUBLANES, LANES), row),
        ],
        out_shape=[
            jax.ShapeDtypeStruct((t * ROW_TILES, LANES), F32),
            jax.ShapeDtypeStruct((t, LANES), jnp.int32),
            jax.ShapeDtypeStruct((t, LANES), F32),
            jax.ShapeDtypeStruct((SUBLANES, LANES), F32),
            jax.ShapeDtypeStruct((bt, SUBLANES, D_MODEL), F32),
            jax.ShapeDtypeStruct((t // min(tm, ROUTE_TILE) * SUBLANES, LANES), F32),
        ],
        scratch_shapes=[
            pltpu.VMEM((tm + SUBLANES, D_MODEL), F32),
            pltpu.VMEM((SUBLANES, LANES), F32),
        ],
        compiler_params=_cparams("arbitrary"),
        name="merge_ln_router",
    )(x2d, ssd_pre, att, h_main, h_main, h_main, h_main, h_main, h_main, schist8,
      p["w_ssd_out"], p["w_att_out"], p["w_sc_out"], p["w_o"], p["sc_w"], p["ln1_g"], p["ln1_b"],
      p["router_w"], p["router_b"])


def _row_copy_wait(src_hbm, dst, sem, n_rows):
    pltpu.make_async_copy(src_hbm.at[pl.ds(0, n_rows * ROW_TILES), :],
                          dst.at[pl.ds(0, n_rows * ROW_TILES), :], sem).wait()


def _dispatch_kernel(pstart_ref, pad_ref, ir_ref, x_ref, xs_hbm, zero_ref, sem, zsem, *, tm):
    @pl.when(pl.program_id(0) == 0)
    def _():
        zero_ref[...] = jnp.zeros_like(zero_ref)

        def zero_copy(row, n_rows):
            dst = pl.multiple_of(row * ROW_TILES, ROW_TILES)
            return pltpu.make_async_copy(zero_ref.at[pl.ds(0, n_rows * ROW_TILES), :],
                                         xs_hbm.at[pl.ds(dst, n_rows * ROW_TILES), :], zsem)

        def per_expert(start):
            def body(e, carry):
                first = pad_ref[e]
                n = pad_ref[N_EXPERTS + 1 + e]
                n_big = n // ZERO_ROWS

                def big(j, c):
                    cp = zero_copy(first + j * ZERO_ROWS, ZERO_ROWS)
                    cp.start() if start else cp.wait()
                    return c

                def small(j, c):
                    cp = zero_copy(first + j, 1)
                    cp.start() if start else cp.wait()
                    return c

                lax.fori_loop(0, n_big, big, 0)
                lax.fori_loop(n_big * ZERO_ROWS, n, small, 0)
                return carry

            lax.fori_loop(0, N_EXPERTS + 1, body, 0)

        per_expert(True)
        per_expert(False)

    def issue(tk, carry):
        src = pl.multiple_of(tk * ROW_TILES, ROW_TILES)
        for k in range(TOP_K):
            dest = pstart_ref[ir_ref[tk * SUBLANES + k]] + ir_ref[tk * SUBLANES + TOP_K + k]
            dst = pl.multiple_of(dest * ROW_TILES, ROW_TILES)
            pltpu.make_async_copy(x_ref.at[pl.ds(src, ROW_TILES), :],
                                  xs_hbm.at[pl.ds(dst, ROW_TILES), :], sem).start()
        return carry

    lax.fori_loop(0, tm, issue, 0)
    for _ in range(TOP_K):
        _row_copy_wait(x_ref, xs_hbm, sem, tm)


def _dispatch(x1r, ir_flat, pstart, pad_info, rows, t, tm):
    return pl.pallas_call(
        functools.partial(_dispatch_kernel, tm=tm),
        grid_spec=pltpu.PrefetchScalarGridSpec(
            num_scalar_prefetch=2,
            grid=(t // tm,),
            in_specs=[
                pl.BlockSpec((tm * SUBLANES,), lambda i, ps, pd: (i,), memory_space=pltpu.SMEM),
                pl.BlockSpec((tm * ROW_TILES, LANES), lambda i, ps, pd: (i, 0)),
            ],
            out_specs=pl.BlockSpec(memory_space=pl.ANY),
            scratch_shapes=[pltpu.VMEM((ZERO_ROWS * ROW_TILES, LANES), F32),
                            pltpu.SemaphoreType.DMA, pltpu.SemaphoreType.DMA],
        ),
        out_shape=jax.ShapeDtypeStruct((rows * ROW_TILES, LANES), F32),
        compiler_params=_cparams("arbitrary"),
        name="moe_dispatch",
    )(pstart, pad_info, ir_flat, x1r)


def _moe_kernel(be_ref, nv_ref, x_ref, wu_ref, bu_ref, wd_ref, bd_ref, o_ref, wub_ref, wdb_ref, *, tmo):
    i = pl.program_id(0)
    valid = i < nv_ref[0]
    new_expert = jnp.logical_or(i == 0, be_ref[i] != be_ref[jnp.maximum(i - 1, 0)])

    @pl.when(jnp.logical_and(valid, new_expert))
    def _():
        wub_ref[...] = wu_ref[0, 0].astype(BF16)
        wdb_ref[...] = wd_ref[0, 0].astype(BF16)

    @pl.when(valid)
    def _():
        x = jnp.concatenate([x_ref[pl.ds(j, tmo, stride=ROW_TILES), :] for j in range(ROW_TILES)], axis=1)
        h = jnp.dot(x.astype(BF16), wub_ref[...], preferred_element_type=F32) + bu_ref[0, 0]
        gate = jnp.minimum(h[:, :D_FF], SWIGLU_LIMIT)
        up = jnp.clip(h[:, D_FF:], -SWIGLU_LIMIT, SWIGLU_LIMIT)
        act = (up + 1.0) * gate * _sigmoid(SWIGLU_ALPHA * gate)
        o = jnp.dot(act.astype(BF16), wdb_ref[...], preferred_element_type=F32) + bd_ref[0, 0]
        for j in range(ROW_TILES):
            o_ref[pl.ds(j, tmo, stride=ROW_TILES), :] = o[:, j * LANES:(j + 1) * LANES]

    @pl.when(jnp.logical_not(valid))
    def _():
        o_ref[...] = jnp.zeros_like(o_ref)


def _moe_experts(xs, block_e, nvalid, p, n_blocks, tmo):
    l = p["layer"]
    last_valid = lambda i, nv: jnp.minimum(i, nv[0] - 1)
    return pl.pallas_call(
        functools.partial(_moe_kernel, tmo=tmo),
        grid_spec=pltpu.PrefetchScalarGridSpec(
            num_scalar_prefetch=2,
            grid=(n_blocks,),
            in_specs=[
                pl.BlockSpec((tmo * ROW_TILES, LANES), lambda i, be, nv: (last_valid(i, nv), 0)),
                pl.BlockSpec((1, 1, D_MODEL, 2 * D_FF), lambda i, be, nv: (l, be[i], 0, 0)),
                pl.BlockSpec((1, 1, 1, 2 * D_FF), lambda i, be, nv: (l, be[i], 0, 0)),
                pl.BlockSpec((1, 1, D_FF, D_MODEL), lambda i, be, nv: (l, be[i], 0, 0)),
                pl.BlockSpec((1, 1, 1, D_MODEL), lambda i, be, nv: (l, be[i], 0, 0)),
            ],
            out_specs=pl.BlockSpec((tmo * ROW_TILES, LANES), lambda i, be, nv: (i, 0)),
            scratch_shapes=[pltpu.VMEM((D_MODEL, 2 * D_FF), BF16), pltpu.VMEM((D_FF, D_MODEL), BF16)],
        ),
        out_shape=jax.ShapeDtypeStruct(xs.shape, F32),
        compiler_params=_cparams("arbitrary"),
        name="moe_experts",
    )(block_e, nvalid, xs, p["w_up"], p["b_up"], p["w_down"], p["b_down"])


def _combine_kernel(pstart_ref, ir_ref, irn_ref, x1r_ref, gate_ref, lng_ref, lnb_ref, eo_hbm, o_ref, buf_ref, sem,
                    *, tm, nt):
    i = pl.program_id(0)
    slot = i % 2

    def issue(idx_ref, s):
        def body(tk, carry):
            dst = pl.multiple_of(tk * ROW_TILES, ROW_TILES)
            for k in range(TOP_K):
                dest = pstart_ref[idx_ref[tk * SUBLANES + k]] + idx_ref[tk * SUBLANES + TOP_K + k]
                src = pl.multiple_of(dest * ROW_TILES, ROW_TILES)
                pltpu.make_async_copy(eo_hbm.at[pl.ds(src, ROW_TILES), :],
                                      buf_ref.at[s, k, pl.ds(dst, ROW_TILES), :], sem.at[s]).start()
            return carry

        lax.fori_loop(0, tm, body, 0)

    @pl.when(i == 0)
    def _():
        issue(ir_ref, 0)

    @pl.when(i + 1 < nt)
    def _():
        issue(irn_ref, 1 - slot)

    for k in range(TOP_K):
        _row_copy_wait(eo_hbm, buf_ref.at[slot, k], sem.at[slot], tm)

    gates = gate_ref[...]
    vs = []
    tot = jnp.zeros((tm, 1), F32)
    for j in range(ROW_TILES):
        acc = ALPHA * x1r_ref[pl.ds(j, tm, stride=ROW_TILES), :]
        for k in range(TOP_K):
            acc = acc + buf_ref[slot, k, pl.ds(j, tm, stride=ROW_TILES), :] * gates[:, k:k + 1]
        vs.append(acc)
        tot = tot + jnp.sum(acc, axis=-1, keepdims=True)
    mu = tot * (1.0 / D_MODEL)
    sq = jnp.zeros((tm, 1), F32)
    for j in range(ROW_TILES):
        vs[j] = vs[j] - mu
        sq = sq + jnp.sum(vs[j] * vs[j], axis=-1, keepdims=True)
    inv = lax.rsqrt(sq * (1.0 / D_MODEL) + LN_EPS)
    for j in range(ROW_TILES):
        sl = slice(j * LANES, (j + 1) * LANES)
        o_ref[:, sl] = vs[j] * inv * lng_ref[:, sl] + lnb_ref[:, sl]


def _combine(x1r, ir_flat, gates, pstart, eo, p, t, tm):
    nt = t // tm
    return pl.pallas_call(
        functools.partial(_combine_kernel, tm=tm, nt=nt),
        grid_spec=pltpu.PrefetchScalarGridSpec(
            num_scalar_prefetch=1,
            grid=(nt,),
            in_specs=[
                pl.BlockSpec((tm * SUBLANES,), lambda i, ps: (i,), memory_space=pltpu.SMEM),
                pl.BlockSpec((tm * SUBLANES,), lambda i, ps: (jnp.minimum(i + 1, nt - 1),), memory_space=pltpu.SMEM),
                pl.BlockSpec((tm * ROW_TILES, LANES), lambda i, ps: (i, 0)),
                pl.BlockSpec((tm, LANES), lambda i, ps: (i, 0)),
                pl.BlockSpec((1, D_MODEL), lambda i, ps: (0, 0)),
                pl.BlockSpec((1, D_MODEL), lambda i, ps: (0, 0)),
                pl.BlockSpec(memory_space=pl.ANY),
            ],
            out_specs=pl.BlockSpec((tm, D_MODEL), lambda i, ps: (i, 0)),
            scratch_shapes=[pltpu.VMEM((2, TOP_K, tm * ROW_TILES, LANES), F32), pltpu.SemaphoreType.DMA((2,))],
        ),
        out_shape=jax.ShapeDtypeStruct((t, D_MODEL), F32),
        compiler_params=_cparams("arbitrary"),
        name="moe_combine_ln",
    )(pstart, ir_flat, ir_flat, x1r, gates, p["ln2_g"], p["ln2_b"], eo)


def _combine_seg_kernel(seg_ref, segn_ref, ir_ref, x1r_ref, gate_ref, lng_ref, lnb_ref, eo_hbm, o_ref,
                        stage_ref, sem, *, rt, nt):
    i = pl.program_id(0)
    slot = i % 2
    sr = stage_ref.shape[1] // ROW_TILES

    @pl.when(i == 0)
    def _():
        stage_ref[...] = jnp.zeros_like(stage_ref)

    def copies(s_ref, s, start):
        def per_expert(e, carry):
            n_copies = (s_ref[e] + (SEG_ALIGN - 1)) // SEG_ALIGN
            dst0 = s_ref[N_EXPERTS + e]
            src0 = s_ref[2 * N_EXPERTS + e]

            def one(j, c):
                src = pl.multiple_of((src0 + j * SEG_ALIGN) * ROW_TILES, ROW_TILES)
                dst = pl.multiple_of((dst0 + j * SEG_ALIGN) * ROW_TILES, SEG_ALIGN * ROW_TILES)
                cp = pltpu.make_async_copy(eo_hbm.at[pl.ds(src, SEG_ALIGN * ROW_TILES), :],
                                           stage_ref.at[s, pl.ds(dst, SEG_ALIGN * ROW_TILES), :], sem.at[s])
                cp.start() if start else cp.wait()
                return c

            lax.fori_loop(0, n_copies, one, 0)
            return carry

        lax.fori_loop(0, N_EXPERTS, per_expert, 0)

    @pl.when(i == 0)
    def _():
        copies(seg_ref, 0, True)

    @pl.when(i + 1 < nt)
    def _():
        copies(segn_ref, 1 - slot, True)

    copies(seg_ref, slot, False)

    rows = jnp.concatenate([stage_ref[slot, pl.ds(j, sr, stride=ROW_TILES), :] for j in range(ROW_TILES)], axis=1)
    rows_hi = rows.astype(BF16)
    rows_lo = (rows - rows_hi.astype(F32)).astype(BF16)
    pos = lax.broadcasted_iota(jnp.int32, (rt, sr), 1)
    ir = ir_ref[...]
    gates = gate_ref[...]
    weight = jnp.zeros((rt, sr), F32)
    for k in range(TOP_K):
        weight = jnp.where(pos == ir[:, 2 * TOP_K + k:2 * TOP_K + k + 1], gates[:, k:k + 1], weight)
    w_hi = weight.astype(BF16)
    w_lo = (weight - w_hi.astype(F32)).astype(BF16)
    moe = (jnp.dot(w_hi, rows_hi, preferred_element_type=F32)
           + jnp.dot(w_hi, rows_lo, preferred_element_type=F32)
           + jnp.dot(w_lo, rows_hi, preferred_element_type=F32))
    x1 = jnp.concatenate([x1r_ref[pl.ds(j, rt, stride=ROW_TILES), :] for j in range(ROW_TILES)], axis=1)
    v = ALPHA * x1 + moe
    mu = jnp.mean(v, axis=-1, keepdims=True)
    vc = v - mu
    var = jnp.mean(vc * vc, axis=-1, keepdims=True)
    o_ref[...] = vc * lax.rsqrt(var + LN_EPS) * lng_ref[...] + lnb_ref[...]


def _combine_seg(x1r, ir, gates, seg, eo, p, t, rt):
    nt = t // rt
    sr = rt * TOP_K + N_EXPERTS * SEG_ALIGN
    seg_len = 4 * N_EXPERTS
    return pl.pallas_call(
        functools.partial(_combine_seg_kernel, rt=rt, nt=nt),
        grid=(nt,),
        in_specs=[
            pl.BlockSpec((seg_len,), lambda i: (i,), memory_space=pltpu.SMEM),
            pl.BlockSpec((seg_len,), lambda i: (jnp.minimum(i + 1, nt - 1),), memory_space=pltpu.SMEM),
            pl.BlockSpec((rt, LANES), lambda i: (i, 0)),
            pl.BlockSpec((rt * ROW_TILES, LANES), lambda i: (i, 0)),
            pl.BlockSpec((rt, LANES), lambda i: (i, 0)),
            pl.BlockSpec((1, D_MODEL), lambda i: (0, 0)),
            pl.BlockSpec((1, D_MODEL), lambda i: (0, 0)),
            pl.BlockSpec(memory_space=pl.ANY),
        ],
        out_specs=pl.BlockSpec((rt, D_MODEL), lambda i: (i, 0)),
        out_shape=jax.ShapeDtypeStruct((t, D_MODEL), F32),
        scratch_shapes=[pltpu.VMEM((2, sr * ROW_TILES, LANES), F32), pltpu.SemaphoreType.DMA((2,))],
        compiler_params=_cparams("arbitrary"),
        name="moe_combine_ln",
    )(seg, seg, ir, x1r, gates, p["ln2_g"], p["ln2_b"], eo)


def _moe(x1r, ir, gates, cnt, stat, p, t, tm, tmo, rt):
    n_assign = t * TOP_K
    n_blocks = (n_assign + N_EXPERTS * (tmo - 1) + tmo - 1) // tmo
    rows = n_blocks * tmo
    counts = cnt[0, :N_EXPERTS].astype(jnp.int32)
    padded = (counts + tmo - 1) // tmo * tmo
    pend = jnp.cumsum(padded)
    pstart = (pend - padded).astype(jnp.int32)
    pad_info = jnp.concatenate([pstart + counts, pend[-1:], padded - counts, rows - pend[-1:]]).astype(jnp.int32)
    block_start = jnp.arange(n_blocks, dtype=jnp.int32) * tmo
    block_e = jnp.minimum(jnp.sum((pend[None, :] <= block_start[:, None]).astype(jnp.int32), axis=1),
                          N_EXPERTS - 1)
    nvalid = (pend[-1:] // tmo).astype(jnp.int32)
    ir_flat = ir[:, :SUBLANES].reshape(t * SUBLANES)
    xs = _dispatch(x1r, ir_flat, pstart, pad_info, rows, t, tm)
    eo = _moe_experts(xs, block_e, nvalid, p, n_blocks, tmo)
    st3 = stat.reshape(t // rt, SUBLANES, LANES)[:, :3, :N_EXPERTS].astype(jnp.int32)
    seg = jnp.concatenate([st3[:, 0], st3[:, 1], st3[:, 2] + pstart[None, :], jnp.zeros_like(st3[:, 0])],
                          axis=1).reshape(-1)
    return _combine_seg(x1r, ir, gates, seg, eo, p, t, rt)


def _t5_bucket(rel):
    half = N_BUCKETS // 2
    max_exact = half // 2
    ret = jnp.where(rel > 0, half, 0)
    n = jnp.abs(rel)
    nf = jnp.maximum(n, 1).astype(F32)
    large = max_exact + (jnp.log(nf / max_exact) / math.log(MAX_DISTANCE / max_exact)
                         * (half - max_exact)).astype(jnp.int32)
    large = jnp.minimum(large, half - 1)
    return ret + jnp.where(n < max_exact, n, large)


def _rel_bias(table, cq):
    koff = jnp.arange(WINDOW + cq) - WINDOW
    rel = koff[None, :] - jnp.arange(cq)[:, None]
    return jnp.transpose(table[_t5_bucket(rel)].astype(F32), (2, 0, 1))


def _pad_rows_front(a, rows):
    pad = jnp.zeros(a.shape[:1] + (rows - a.shape[1],) + a.shape[2:], a.dtype)
    return jnp.concatenate([pad, a], axis=1)


def _layer_params(l, w_in, ssd_conv_w, ssd_conv_b, ssd_dt_bias, ssd_a_log, ssd_d, ssd_norm_w, w_ssd_out,
                  attn_sinks, w_attn_out, sc_conv_w, w_sc_out, w_o, ln1_g, ln1_b,
                  router_w, router_b, w_up, b_up, w_down, b_down, ln2_g, ln2_b):
    wi = w_in[l]
    w_dt = wi[:, OFF_DT:OFF_Q]
    return {
        "w_main": jnp.concatenate([wi[:, :OFF_DT], wi[:, OFF_Q:OFF_K], wi[:, OFF_SCB:]], axis=1).astype(BF16),
        "w_kv": wi[:, OFF_K:OFF_SCB].astype(BF16),
        "w_dt": jnp.pad(w_dt, ((0, 0), (0, LANES - SSD_N_HEADS))).astype(BF16),
        "w_dtT": w_dt.T.astype(BF16),
        "conv_w": ssd_conv_w[l],
        "conv_b": ssd_conv_b[l][None, :],
        "dt_b128": jnp.pad(ssd_dt_bias[l], (0, LANES - SSD_N_HEADS))[None, :],
        "dt_bT": ssd_dt_bias[l][:, None],
        "a_log128": jnp.pad(ssd_a_log[l], (0, LANES - SSD_N_HEADS))[None, :],
        "expand": jnp.asarray(_head_expand_matrix(), BF16),
        "a_logT": ssd_a_log[l][:, None],
        "d_exp": jnp.repeat(ssd_d[l], SSD_HEAD_DIM)[None, :],
        "norm_w": ssd_norm_w[l][None, :],
        "w_ssd_out": w_ssd_out[l].astype(BF16),
        "sinks": attn_sinks[l],
        "w_att_out": w_attn_out[l].astype(BF16),
        "sc_w": sc_conv_w[l],
        "w_sc_out": w_sc_out[l].astype(BF16),
        "w_o": w_o[l].astype(BF16),
        "ln1_g": ln1_g[l][None, :],
        "ln1_b": ln1_b[l][None, :],
        "router_w": jnp.pad(router_w[l], ((0, 0), (0, LANES - N_EXPERTS))).astype(BF16),
        "router_b": jnp.pad(router_b[l], (0, LANES - N_EXPERTS))[None, :],
        "layer": l,
        "w_up": w_up,
        "b_up": b_up[:, :, None, :],
        "w_down": w_down,
        "b_down": b_down[:, :, None, :],
        "ln2_g": ln2_g[l][None, :],
        "ln2_b": ln2_b[l][None, :],
    }


def _trunk_layer(x2d, bt, s, p, bias, ssd_hist, ssd_state, sc_hist, kv_cache, cfg):
    t = bt * s
    h_main = _matmul(x2d, p["w_main"], BF16, cfg["tm_in"], cfg["tn_in"])
    h_kv = _matmul(x2d, p["w_kv"], F32, cfg["tm_in"], 2 * ATT_KV_DIM)
    k_new = h_kv[:, :ATT_KV_DIM].reshape(bt, s, ATT_KV_DIM)
    v_new = h_kv[:, ATT_KV_DIM:].reshape(bt, s, ATT_KV_DIM)

    if kv_cache is None:
        zpad = jnp.zeros((bt, WINDOW, ATT_KV_DIM), BF16)
        kall = jnp.concatenate([zpad, k_new.astype(BF16)], axis=1)
        vall = jnp.concatenate([zpad, v_new.astype(BF16)], axis=1)
    else:
        kall = jnp.concatenate([kv_cache[0].astype(BF16), k_new.astype(BF16)], axis=1)
        vall = jnp.concatenate([kv_cache[1].astype(BF16), v_new.astype(BF16)], axis=1)
    att = _attention(h_main, kall, vall, bias, p["sinks"], bt, s, cfg["cq"], cfg["n_sub"], kv_cache is None)

    if ssd_hist is None:
        hist8 = jnp.zeros((bt, SUBLANES, SSD_CONV_DIM), F32)
        st0 = jnp.zeros((bt, SSD_N_GROUPS, SSD_D_STATE, SSD_HEADS_PER_GROUP * SSD_HEAD_DIM), F32)
        schist8 = jnp.zeros((bt, SUBLANES, D_MODEL), F32)
    else:
        hist8 = _pad_rows_front(ssd_hist, SUBLANES)
        st0 = _state_to_kernel(ssd_state)
        schist8 = _pad_rows_front(sc_hist, SUBLANES)
    ssd_pre, st_out, hist_out = _ssd_grouped(x2d, h_main, hist8, st0, p, bt, s, cfg["cq"])

    x1r, ir, gates, cnt, schist_out, stat = _merge(x2d, h_main, ssd_pre, att, schist8, p, bt, s, cfg["tm_merge"])
    x2 = _moe(x1r, ir, gates, cnt, stat, p, t, cfg["tm_moe"], cfg["tmo"], min(cfg["tm_merge"], ROUTE_TILE))

    n_keep = min(s, WINDOW)
    new_k = k_new[:, s - n_keep:].reshape(bt, n_keep, ATT_KV_HEADS, ATT_HEAD_DIM)
    new_v = v_new[:, s - n_keep:].reshape(bt, n_keep, ATT_KV_HEADS, ATT_HEAD_DIM)
    new_state = _state_from_kernel(st_out)
    new_hist = hist_out[:, SUBLANES - (SSD_CONV - 1):]
    new_sc_hist = schist_out[:, SUBLANES - (SC_WIDTH - 1):]
    return x2, new_k, new_v, new_state, new_hist, new_sc_hist


def _config(bt, s, prompt):
    t = bt * s
    if prompt:
        return {"tm_in": min(t, 2048), "tn_in": 1024, "cq": CHUNK, "n_sub": min(4, s // CHUNK),
                "tm_merge": min(s, 512), "tm_moe": min(t, 256), "tmo": min(512, max(64, t // 32))}
    return {"tm_in": min(t, 256), "tn_in": 1024, "cq": s, "n_sub": 1,
            "tm_merge": s, "tm_moe": min(t, 256), "tmo": 64}


def _forward(x_prompt, x_sample, cache_attn_k, cache_attn_v, state_ssd, state_ssd_conv, state_short_conv,
             rel_bias, layer_weights):
    bp, sp, _ = x_prompt.shape
    bd, sd, _ = x_sample.shape
    cfg_p = _config(bp, sp, True)
    cfg_d = _config(bd, sd, False)
    bias_p = _rel_bias(rel_bias, cfg_p["cq"])
    bias_d = _rel_bias(rel_bias, cfg_d["cq"])
    yp = x_prompt.reshape(bp * sp, D_MODEL)
    ys = x_sample.reshape(bd * sd, D_MODEL)
    outs_p, outs_d = [], []
    for l in range(DEPTH):
        p = _layer_params(l, *layer_weights)
        yp, *rest = _trunk_layer(yp, bp, sp, p, bias_p, None, None, None, None, cfg_p)
        outs_p.append(rest)
        cache = (cache_attn_k[l].reshape(bd, WINDOW, ATT_KV_DIM), cache_attn_v[l].reshape(bd, WINDOW, ATT_KV_DIM))
        ys, *rest = _trunk_layer(ys, bd, sd, p, bias_d, state_ssd_conv[l], state_ssd[l], state_short_conv[l],
                                 cache, cfg_d)
        outs_d.append(rest)
    stack = lambda outs, i: jnp.stack([o[i] for o in outs])
    return (yp.reshape(bp, sp, D_MODEL), ys.reshape(bd, sd, D_MODEL),
            stack(outs_p, 0), stack(outs_p, 1), stack(outs_p, 2), stack(outs_p, 3), stack(outs_p, 4),
            stack(outs_d, 0), stack(outs_d, 1), stack(outs_d, 2), stack(outs_d, 3), stack(outs_d, 4))


def kernel(x_prompt, x_sample, cache_attn_k, cache_attn_v, state_ssd, state_ssd_conv, state_short_conv, w_in, ssd_conv_w, ssd_conv_b, ssd_dt_bias, ssd_a_log, ssd_d, ssd_norm_w, w_ssd_out, attn_sinks, w_attn_out, rel_bias, sc_conv_w, w_sc_out, w_o, ln1_g, ln1_b, router_w, router_b, w_up, b_up, w_down, b_down, ln2_g, ln2_b):
    layer_weights = (w_in, ssd_conv_w, ssd_conv_b, ssd_dt_bias, ssd_a_log, ssd_d, ssd_norm_w, w_ssd_out,
                     attn_sinks, w_attn_out, sc_conv_w, w_sc_out, w_o, ln1_g, ln1_b,
                     router_w, router_b, w_up, b_up, w_down, b_down, ln2_g, ln2_b)
    return _forward(x_prompt, x_sample, cache_attn_k, cache_attn_v, state_ssd, state_ssd_conv, state_short_conv,
                    rel_bias, layer_weights)
```

```python
import functools
import math

import jax
import jax.numpy as jnp
import numpy as np
from jax import lax
from jax.experimental import pallas as pl
from jax.experimental.pallas import tpu as pltpu

F32 = jnp.float32
BF16 = jnp.bfloat16

D_MODEL = 1024
DEPTH = 2
CHUNK = 64
SSD_D_INNER = 2 * D_MODEL
SSD_HEAD_DIM = 64
SSD_N_HEADS = SSD_D_INNER // SSD_HEAD_DIM
SSD_N_GROUPS = 8
SSD_D_STATE = 128
SSD_CONV = 4
SSD_GN = SSD_N_GROUPS * SSD_D_STATE
SSD_CONV_DIM = SSD_D_INNER + 2 * SSD_GN
SSD_HEADS_PER_GROUP = SSD_N_HEADS // SSD_N_GROUPS
SSD_NORM_GROUP = SSD_D_INNER // SSD_N_GROUPS
ATT_HEAD_DIM = 64
ATT_HEADS = D_MODEL // ATT_HEAD_DIM
ATT_KV_HEADS = ATT_HEADS // 4
ATT_GROUP = ATT_HEADS // ATT_KV_HEADS
ATT_KV_DIM = ATT_KV_HEADS * ATT_HEAD_DIM
WINDOW = 128
N_BUCKETS = 32
MAX_DISTANCE = 128
SC_WIDTH = 3
N_EXPERTS = 32
TOP_K = 4
D_FF = D_MODEL
SWIGLU_LIMIT = 7.0
SWIGLU_ALPHA = 1.702
LN_EPS = 1e-5
ALPHA = (2.0 * DEPTH) ** 0.25

LANES = 128
SUBLANES = 8
ROW_TILES = D_MODEL // LANES
VMEM_LIMIT = 56 * 1024 * 1024
ZERO_ROWS = 16
ROUTE_TILE = 256
SEG_ALIGN = 8

COL_Z, COL_XS, COL_BC = 0, 1, 2
COL_Q, COL_SCB, COL_SCC, COL_SCH, COL_GSSD, COL_GATT, COL_GSC = 6, 7, 8, 9, 10, 11, 12
N_MAIN = 13 * D_MODEL
OFF_DT = SSD_D_INNER + SSD_CONV_DIM
OFF_Q = OFF_DT + SSD_N_HEADS
OFF_K = OFF_Q + D_MODEL
OFF_SCB = OFF_K + 2 * ATT_KV_DIM


def _sigmoid(v):
    return 0.5 * jnp.tanh(0.5 * v) + 0.5


def _cparams(*sem):
    return pltpu.CompilerParams(dimension_semantics=sem, vmem_limit_bytes=VMEM_LIMIT)


def _matmul_kernel(x_ref, w_ref, o_ref, xb_ref):
    @pl.when(pl.program_id(1) == 0)
    def _():
        xb_ref[...] = x_ref[...].astype(BF16)

    o_ref[...] = jnp.dot(xb_ref[...], w_ref[...], preferred_element_type=F32).astype(o_ref.dtype)


def _matmul(x, w, out_dtype, tm, tn):
    t, k = x.shape
    n = w.shape[1]
    return pl.pallas_call(
        _matmul_kernel,
        grid=(t // tm, n // tn),
        in_specs=[pl.BlockSpec((tm, k), lambda i, j: (i, 0)),
                  pl.BlockSpec((k, tn), lambda i, j: (0, j))],
        out_specs=pl.BlockSpec((tm, tn), lambda i, j: (i, j)),
        out_shape=jax.ShapeDtypeStruct((t, n), out_dtype),
        scratch_shapes=[pltpu.VMEM((tm, k), BF16)],
        compiler_params=_cparams("arbitrary", "arbitrary"),
        name="in_proj",
    )(x, w)


def _ssd_kernel(x_ref, z_ref, xs_ref, bc_ref, hist_ref, st0_ref,
                wdt_ref, wdtT_ref, convw_ref, convb_ref, dtb_ref, dtbT_ref, alog_ref, alogT_ref,
                dexp_ref, normw_ref,
                y_ref, stout_ref, histout_ref,
                ext_ref, st_ref, ybuf_ref, *, L, nc):
    c = pl.program_id(1)

    @pl.when(c == 0)
    def _():
        ext_ref[0:SUBLANES, :] = hist_ref[0]
        st_ref[...] = st0_ref[0]

    xbc = jnp.concatenate([xs_ref[...], bc_ref[...]], axis=1).astype(F32)
    ext_ref[SUBLANES:SUBLANES + L, :] = xbc
    base = SUBLANES - (SSD_CONV - 1)
    conv = convb_ref[...]
    for j in range(SSD_CONV):
        conv = conv + ext_ref[base + j:base + j + L, :] * convw_ref[j:j + 1, :]
    tail = ext_ref[L:L + SUBLANES, :]
    ext_ref[0:SUBLANES, :] = tail
    histout_ref[0] = tail
    xc = conv * _sigmoid(conv)
    xs_c = xc[:, :SSD_D_INNER]
    b_m = xc[:, SSD_D_INNER:SSD_D_INNER + SSD_GN]
    c_m = xc[:, SSD_D_INNER + SSD_GN:]

    xb = x_ref[...].astype(BF16)
    dtr = jnp.dot(xb, wdt_ref[...], preferred_element_type=F32)[:, :SSD_N_HEADS]
    dtrT = lax.dot_general(wdtT_ref[...], xb, (((1,), (1,)), ((), ())),
                           preferred_element_type=F32)

    def softplus(v):
        return jnp.maximum(v, 0.0) + jnp.log1p(jnp.exp(-jnp.abs(v)))

    dt = softplus(dtr + dtb_ref[...])
    dtT = softplus(dtrT + dtbT_ref[...])
    adt = dt * (-jnp.exp(alog_ref[...]))
    adtT = dtT * (-jnp.exp(alogT_ref[...]))
    ri = lax.broadcasted_iota(jnp.int32, (L, L), 0)
    ci = lax.broadcasted_iota(jnp.int32, (L, L), 1)
    tri = ri >= ci
    acs = jnp.dot(tri.astype(F32), adt, precision=lax.Precision.HIGHEST,
                  preferred_element_type=F32)
    acsT = jnp.dot(adtT, (ri <= ci).astype(F32), precision=lax.Precision.HIGHEST,
                   preferred_element_type=F32)
    totT = jnp.sum(adtT, axis=1, keepdims=True)
    eacs = jnp.exp(acs)
    wT = jnp.exp(totT - acsT) * dtT
    cdT = jnp.exp(totT)

    for g in range(SSD_N_GROUPS):
        bg = b_m[:, g * SSD_D_STATE:(g + 1) * SSD_D_STATE]
        cg_b = c_m[:, g * SSD_D_STATE:(g + 1) * SSD_D_STATE].astype(BF16)
        cb = lax.dot_general(cg_b, bg.astype(BF16), (((1,), (1,)), ((), ())),
                             preferred_element_type=F32)
        if L < LANES:
            bg = jnp.concatenate([bg, jnp.zeros((LANES - L, SSD_D_STATE), F32)], axis=0)
        bgT = bg.T[:, :L]
        for r in range(SSD_HEADS_PER_GROUP):
            h = g * SSD_HEADS_PER_GROUP + r
            col = acs[:, h:h + 1]
            row = acsT[h:h + 1, :]
            decay = jnp.exp(jnp.where(tri, col - row, -jnp.inf))
            m = (cb * decay * dtT[h:h + 1, :]).astype(BF16)
            xh_b = xs_c[:, h * SSD_HEAD_DIM:(h + 1) * SSD_HEAD_DIM].astype(BF16)
            st = st_ref[h]
            yd = jnp.dot(m, xh_b, preferred_element_type=F32)
            yo = jnp.dot(cg_b, st.astype(BF16), preferred_element_type=F32)
            ybuf_ref[:, h * SSD_HEAD_DIM:(h + 1) * SSD_HEAD_DIM] = yd + yo * eacs[:, h:h + 1]
            wh = (bgT * wT[h:h + 1, :]).astype(BF16)
            stc = jnp.dot(wh, xh_b, preferred_element_type=F32)
            st_ref[h] = st * cdT[h:h + 1, :] + stc

    y = ybuf_ref[...] + dexp_ref[...] * xs_c
    z = z_ref[...].astype(F32)
    y = y * (z * _sigmoid(z))
    for g in range(SSD_N_GROUPS):
        sl = slice(g * SSD_NORM_GROUP, (g + 1) * SSD_NORM_GROUP)
        blk = y[:, sl]
        ms = jnp.mean(blk * blk, axis=-1, keepdims=True)
        y_ref[:, sl] = (blk * lax.rsqrt(ms + LN_EPS) * normw_ref[:, sl]).astype(y_ref.dtype)

    @pl.when(c == nc - 1)
    def _():
        stout_ref[0] = st_ref[...]


def _ssd(x2d, h_main, hist8, st0, p, bt, s, L):
    nc = s // L
    t = bt * s
    w2 = 2 * D_MODEL
    row = lambda b, c: (b * nc + c, 0)
    const2 = lambda b, c: (0, 0)
    kern = functools.partial(_ssd_kernel, L=L, nc=nc)
    return pl.pallas_call(
        kern,
        grid=(bt, nc),
        in_specs=[
            pl.BlockSpec((L, D_MODEL), row),
            pl.BlockSpec((L, w2), lambda b, c: (b * nc + c, COL_Z)),
            pl.BlockSpec((L, w2), lambda b, c: (b * nc + c, COL_XS)),
            pl.BlockSpec((L, w2), lambda b, c: (b * nc + c, COL_BC)),
            pl.BlockSpec((1, SUBLANES, SSD_CONV_DIM), lambda b, c: (b, 0, 0)),
            pl.BlockSpec((1, SSD_N_HEADS, SSD_D_STATE, SSD_HEAD_DIM), lambda b, c: (b, 0, 0, 0)),
            pl.BlockSpec((D_MODEL, LANES), const2),
            pl.BlockSpec((SSD_N_HEADS, D_MODEL), const2),
            pl.BlockSpec((SSD_CONV, SSD_CONV_DIM), const2),
            pl.BlockSpec((1, SSD_CONV_DIM), const2),
            pl.BlockSpec((1, SSD_N_HEADS), const2),
            pl.BlockSpec((SSD_N_HEADS, 1), const2),
            pl.BlockSpec((1, SSD_N_HEADS), const2),
            pl.BlockSpec((SSD_N_HEADS, 1), const2),
            pl.BlockSpec((1, SSD_D_INNER), const2),
            pl.BlockSpec((1, SSD_D_INNER), const2),
        ],
        out_specs=[
            pl.BlockSpec((L, SSD_D_INNER), row),
            pl.BlockSpec((1, SSD_N_HEADS, SSD_D_STATE, SSD_HEAD_DIM), lambda b, c: (b, 0, 0, 0)),
            pl.BlockSpec((1, SUBLANES, SSD_CONV_DIM), lambda b, c: (b, 0, 0)),
        ],
        out_shape=[
            jax.ShapeDtypeStruct((t, SSD_D_INNER), BF16),
            jax.ShapeDtypeStruct((bt, SSD_N_HEADS, SSD_D_STATE, SSD_HEAD_DIM), F32),
            jax.ShapeDtypeStruct((bt, SUBLANES, SSD_CONV_DIM), F32),
        ],
        scratch_shapes=[
            pltpu.VMEM((L + SUBLANES, SSD_CONV_DIM), F32),
            pltpu.VMEM((SSD_N_HEADS, SSD_D_STATE, SSD_HEAD_DIM), F32),
            pltpu.VMEM((L, SSD_D_INNER), F32),
        ],
        compiler_params=_cparams("arbitrary", "arbitrary"),
        name="ssd_scan",
    )(x2d, h_main, h_main, h_main, hist8, st0,
      p["w_dt"], p["w_dtT"], p["conv_w"], p["conv_b"], p["dt_b"], p["dt_bT"], p["a_log"], p["a_logT"],
      p["d_exp"], p["norm_w"])


def _ssd_group_kernel(x_ref, z_ref, xs_ref, bc_ref, hist_ref, st0_ref,
                      wdt_ref, wdtT_ref, convw_ref, convb_ref, dtb_ref, dtbT_ref, alog_ref, alogT_ref,
                      dexp_ref, normw_ref, expand_ref,
                      y_ref, stout_ref, histout_ref,
                      ext_ref, xrow_ref, st_ref, ynat_ref, *, L, nc):
    c = pl.program_id(1)
    stride = L // SUBLANES
    gw = SSD_HEADS_PER_GROUP * SSD_HEAD_DIM

    n_cb = SSD_CONV_DIM // LANES
    lanes = lambda cb: slice(cb * LANES, (cb + 1) * LANES)

    def strided_rows(ref, start):
        return jnp.concatenate([ref[cb, pl.ds(start, SUBLANES, stride=stride), :] for cb in range(ref.shape[0])],
                               axis=1)

    @pl.when(c == 0)
    def _():
        for cb in range(n_cb):
            ext_ref[cb, 0:SUBLANES, :] = hist_ref[0, :, lanes(cb)]
        st_ref[...] = st0_ref[0]

    for cb in range(n_cb):
        src = xs_ref if cb < SSD_D_INNER // LANES else bc_ref
        ext_ref[cb, SUBLANES:SUBLANES + L, :] = src[:, lanes(cb % (SSD_D_INNER // LANES))].astype(F32)
    base = SUBLANES - (SSD_CONV - 1)
    taps = {o: strided_rows(ext_ref, o) for o in range(base, base + stride + SSD_CONV - 1)}
    pieces = []
    for r in range(stride):
        conv = convb_ref[...] + taps[base + r] * convw_ref[0:1, :]
        for j in range(1, SSD_CONV):
            conv = conv + taps[base + r + j] * convw_ref[j:j + 1, :]
        pieces.append(conv * _sigmoid(conv))
    xc = jnp.concatenate(pieces, axis=0)
    for cb in range(n_cb):
        tail = ext_ref[cb, L:L + SUBLANES, :]
        ext_ref[cb, 0:SUBLANES, :] = tail
        histout_ref[0, :, lanes(cb)] = tail
    xs_c = xc[:, :SSD_D_INNER]
    b_m = xc[:, SSD_D_INNER:SSD_D_INNER + SSD_GN]
    c_m = xc[:, SSD_D_INNER + SSD_GN:]

    for cb in range(D_MODEL // LANES):
        xrow_ref[cb] = x_ref[:, lanes(cb)]
    xb = jnp.concatenate([strided_rows(xrow_ref, r) for r in range(stride)], axis=0).astype(BF16)
    lane_h = lax.broadcasted_iota(jnp.int32, (1, LANES), 1) < SSD_N_HEADS
    dtr = jnp.dot(xb, wdt_ref[...], preferred_element_type=F32)
    dtrT = lax.dot_general(wdtT_ref[...], xb, (((1,), (1,)), ((), ())),
                           preferred_element_type=F32)

    def softplus(v):
        return jnp.maximum(v, 0.0) + jnp.log1p(jnp.exp(-jnp.abs(v)))

    dt = jnp.where(lane_h, softplus(dtr + dtb_ref[...]), 0.0)
    dtT = softplus(dtrT + dtbT_ref[...])
    adt = dt * (-jnp.exp(alog_ref[...]))
    adtT = dtT * (-jnp.exp(alogT_ref[...]))
    ri = lax.broadcasted_iota(jnp.int32, (L, L), 0)
    ci = lax.broadcasted_iota(jnp.int32, (L, L), 1)
    time_r = stride * (ri % SUBLANES) + ri // SUBLANES
    time_c = stride * (ci % SUBLANES) + ci // SUBLANES
    tri = time_r >= time_c
    acs = jnp.dot(tri.astype(F32), adt, precision=lax.Precision.HIGHEST,
                  preferred_element_type=F32)
    acsT = jnp.dot(adtT, (time_r <= time_c).astype(F32), precision=lax.Precision.HIGHEST,
                   preferred_element_type=F32)
    tot = jnp.sum(adt, axis=0, keepdims=True)

    cols = jnp.concatenate([acs, jnp.exp(acs), jnp.exp(tot - acs) * dt,
                            jnp.broadcast_to(jnp.exp(tot), (SUBLANES, LANES))], axis=0)
    cols = jnp.where(lane_h, cols, 0.0)
    hi = cols.astype(BF16).astype(F32)
    mid = (cols - hi).astype(BF16).astype(F32)
    lo = (cols - hi - mid).astype(BF16).astype(F32)
    packed = (hi + pltpu.roll(mid, SSD_N_HEADS, 1) + pltpu.roll(lo, 2 * SSD_N_HEADS, 1)).astype(BF16)
    wide = jnp.dot(packed, expand_ref[...], preferred_element_type=F32)
    acs_w = wide[0:L]
    eacs_w = wide[L:2 * L]
    w_w = wide[2 * L:3 * L]
    cd_w = wide[3 * L:3 * L + 1]

    for g in range(SSD_N_GROUPS):
        gs = slice(g * gw, (g + 1) * gw)
        bg = b_m[:, g * SSD_D_STATE:(g + 1) * SSD_D_STATE]
        cg_b = c_m[:, g * SSD_D_STATE:(g + 1) * SSD_D_STATE].astype(BF16)
        cb = lax.dot_general(cg_b, bg.astype(BF16), (((1,), (1,)), ((), ())),
                             preferred_element_type=F32)
        if L < LANES:
            bg = jnp.concatenate([bg, jnp.zeros((LANES - L, SSD_D_STATE), F32)], axis=0)
        bgT_b = bg.T[:, :L].astype(BF16)
        xs_g = xs_c[:, gs]
        st = st_ref[g]
        yo = jnp.dot(cg_b, st.astype(BF16), preferred_element_type=F32)
        stc = jnp.dot(bgT_b, (xs_g * w_w[:, gs]).astype(BF16), preferred_element_type=F32)
        st_ref[g] = st * cd_w[:, gs] + stc
        yds = []
        for r in range(SSD_HEADS_PER_GROUP):
            h = g * SSD_HEADS_PER_GROUP + r
            col = acs_w[:, h * SSD_HEAD_DIM:h * SSD_HEAD_DIM + L]
            decay = jnp.exp(jnp.where(tri, col - acsT[h:h + 1, :], -jnp.inf))
            m = (cb * decay * dtT[h:h + 1, :]).astype(BF16)
            yds.append(jnp.dot(m, xs_g[:, r * SSD_HEAD_DIM:(r + 1) * SSD_HEAD_DIM].astype(BF16),
                               preferred_element_type=F32))
        y_g = jnp.concatenate(yds, axis=1) + yo * eacs_w[:, gs] + dexp_ref[:, gs] * xs_g
        for r in range(stride):
            for cb in range(gw // LANES):
                ynat_ref[g * (gw // LANES) + cb, pl.ds(r, SUBLANES, stride=stride), :] = (
                    y_g[r * SUBLANES:(r + 1) * SUBLANES, lanes(cb)])

    z = z_ref[...].astype(F32)
    y = jnp.concatenate([ynat_ref[cb] for cb in range(SSD_D_INNER // LANES)], axis=1) * (z * _sigmoid(z))
    for g in range(SSD_N_GROUPS):
        sl = slice(g * SSD_NORM_GROUP, (g + 1) * SSD_NORM_GROUP)
        blk = y[:, sl]
        ms = jnp.mean(blk * blk, axis=-1, keepdims=True)
        y_ref[:, sl] = (blk * lax.rsqrt(ms + LN_EPS) * normw_ref[:, sl]).astype(y_ref.dtype)

    @pl.when(c == nc - 1)
    def _():
        stout_ref[0] = st_ref[...]


def _ssd_grouped(x2d, h_main, hist8, st0, p, bt, s, L):
    assert L % SUBLANES == 0 and L <= SSD_HEAD_DIM
    nc = s // L
    t = bt * s
    w2 = 2 * D_MODEL
    gw = SSD_HEADS_PER_GROUP * SSD_HEAD_DIM
    row = lambda b, c: (b * nc + c, 0)
    const2 = lambda b, c: (0, 0)
    state_spec = pl.BlockSpec((1, SSD_N_GROUPS, SSD_D_STATE, gw), lambda b, c: (b, 0, 0, 0))
    kern = functools.partial(_ssd_group_kernel, L=L, nc=nc)
    return pl.pallas_call(
        kern,
        grid=(bt, nc),
        in_specs=[
            pl.BlockSpec((L, D_MODEL), row),
            pl.BlockSpec((L, w2), lambda b, c: (b * nc + c, COL_Z)),
            pl.BlockSpec((L, w2), lambda b, c: (b * nc + c, COL_XS)),
            pl.BlockSpec((L, w2), lambda b, c: (b * nc + c, COL_BC)),
            pl.BlockSpec((1, SUBLANES, SSD_CONV_DIM), lambda b, c: (b, 0, 0)),
            state_spec,
            pl.BlockSpec((D_MODEL, LANES), const2),
            pl.BlockSpec((SSD_N_HEADS, D_MODEL), const2),
            pl.BlockSpec((SSD_CONV, SSD_CONV_DIM), const2),
            pl.BlockSpec((1, SSD_CONV_DIM), const2),
            pl.BlockSpec((1, LANES), const2),
            pl.BlockSpec((SSD_N_HEADS, 1), const2),
            pl.BlockSpec((1, LANES), const2),
            pl.BlockSpec((SSD_N_HEADS, 1), const2),
            pl.BlockSpec((1, SSD_D_INNER), const2),
            pl.BlockSpec((1, SSD_D_INNER), const2),
            pl.BlockSpec((LANES, SSD_D_INNER), const2),
        ],
        out_specs=[
            pl.BlockSpec((L, SSD_D_INNER), row),
            state_spec,
            pl.BlockSpec((1, SUBLANES, SSD_CONV_DIM), lambda b, c: (b, 0, 0)),
        ],
        out_shape=[
            jax.ShapeDtypeStruct((t, SSD_D_INNER), BF16),
            jax.ShapeDtypeStruct((bt, SSD_N_GROUPS, SSD_D_STATE, gw), F32),
            jax.ShapeDtypeStruct((bt, SUBLANES, SSD_CONV_DIM), F32),
        ],
        scratch_shapes=[
            pltpu.VMEM((SSD_CONV_DIM // LANES, L + SUBLANES, LANES), F32),
            pltpu.VMEM((D_MODEL // LANES, L, LANES), F32),
            pltpu.VMEM((SSD_N_GROUPS, SSD_D_STATE, gw), F32),
            pltpu.VMEM((SSD_D_INNER // LANES, L, LANES), F32),
        ],
        compiler_params=_cparams("arbitrary", "arbitrary"),
        name="ssd_scan",
    )(x2d, h_main, h_main, h_main, hist8, st0,
      p["w_dt"], p["w_dtT"], p["conv_w"], p["conv_b"], p["dt_b128"], p["dt_bT"], p["a_log128"], p["a_logT"],
      p["d_exp"], p["norm_w"], p["expand"])


def _head_expand_matrix():
    e = np.zeros((LANES, SSD_D_INNER), np.float32)
    for part in range(3):
        for h in range(SSD_N_HEADS):
            e[part * SSD_N_HEADS + h, h * SSD_HEAD_DIM:(h + 1) * SSD_HEAD_DIM] = 1.0
    return e


def _state_to_kernel(state):
    b = state.shape[0]
    s5 = state.reshape(b, SSD_N_GROUPS, SSD_HEADS_PER_GROUP, SSD_HEAD_DIM, SSD_D_STATE)
    return jnp.transpose(s5, (0, 1, 4, 2, 3)).reshape(b, SSD_N_GROUPS, SSD_D_STATE,
                                                       SSD_HEADS_PER_GROUP * SSD_HEAD_DIM)


def _state_from_kernel(st):
    b = st.shape[0]
    s5 = st.reshape(b, SSD_N_GROUPS, SSD_D_STATE, SSD_HEADS_PER_GROUP, SSD_HEAD_DIM)
    return jnp.transpose(s5, (0, 1, 3, 4, 2)).reshape(b, SSD_N_HEADS, SSD_HEAD_DIM, SSD_D_STATE)


def _attn_kernel(q_ref, k_ref, v_ref, bias_ref, sink_ref, o_ref, *, cq, n_sub, mask_prefix):
    step = pl.program_id(1)
    kw = WINDOW + cq
    scale = ATT_HEAD_DIM ** -0.5
    gw = ATT_GROUP * ATT_HEAD_DIM
    pairs = [(sub, hk) for sub in range(n_sub) for hk in range(ATT_KV_HEADS)]
    scs, vhs = {}, {}
    for sub in range(n_sub):
        r0 = pl.multiple_of((step * n_sub + sub) * cq, cq)
        qq = q_ref[sub * cq:(sub + 1) * cq, :]
        if mask_prefix:
            valid = (r0 + lax.broadcasted_iota(jnp.int32, (1, kw), 1)) >= WINDOW
        for hk in range(ATT_KV_HEADS):
            q4 = jnp.concatenate(
                [qq[:, hk * gw + g * ATT_HEAD_DIM:hk * gw + (g + 1) * ATT_HEAD_DIM] for g in range(ATT_GROUP)],
                axis=0)
            kh = k_ref[0, hk, pl.ds(r0, kw), :]
            sc = lax.dot_general(q4, kh, (((1,), (1,)), ((), ())), preferred_element_type=F32)
            sc = sc * scale + bias_ref[hk]
            if mask_prefix:
                sc = jnp.where(valid, sc, -1e30)
            scs[sub, hk] = sc
            vhs[sub, hk] = v_ref[0, hk, pl.ds(r0, kw), :]
    es, rs = {}, {}
    for sub, hk in pairs:
        sink = sink_ref[hk]
        mx = jnp.maximum(jnp.max(scs[sub, hk], axis=-1, keepdims=True), sink)
        e = jnp.exp(scs[sub, hk] - mx)
        den = jnp.sum(e, axis=-1, keepdims=True) + jnp.exp(sink - mx)
        es[sub, hk] = e.astype(BF16)
        rs[sub, hk] = 1.0 / den
    for sub, hk in pairs:
        o4 = jnp.dot(es[sub, hk], vhs[sub, hk], preferred_element_type=F32) * rs[sub, hk]
        o_ref[sub * cq:(sub + 1) * cq, hk * gw:(hk + 1) * gw] = jnp.concatenate(
            [o4[g * cq:(g + 1) * cq, :] for g in range(ATT_GROUP)], axis=1).astype(o_ref.dtype)


def _attention(h_main, kall, vall, bias, sinks, bt, s, cq, n_sub, mask_prefix):
    tq = cq * n_sub
    nq = s // tq
    kw = WINDOW + cq
    bias4 = bias.reshape(ATT_KV_HEADS, ATT_GROUP * cq, kw)
    sink4 = jnp.repeat(sinks, cq).reshape(ATT_KV_HEADS, ATT_GROUP * cq, 1)
    per_head = lambda a: jnp.swapaxes(a.reshape(bt, WINDOW + s, ATT_KV_HEADS, ATT_HEAD_DIM), 1, 2)
    kern = functools.partial(_attn_kernel, cq=cq, n_sub=n_sub, mask_prefix=mask_prefix)
    return pl.pallas_call(
        kern,
        grid=(bt, nq),
        in_specs=[
            pl.BlockSpec((tq, D_MODEL), lambda b, i: (b * nq + i, COL_Q)),
            pl.BlockSpec((1, ATT_KV_HEADS, WINDOW + s, ATT_HEAD_DIM), lambda b, i: (b, 0, 0, 0)),
            pl.BlockSpec((1, ATT_KV_HEADS, WINDOW + s, ATT_HEAD_DIM), lambda b, i: (b, 0, 0, 0)),
            pl.BlockSpec((ATT_KV_HEADS, ATT_GROUP * cq, kw), lambda b, i: (0, 0, 0)),
            pl.BlockSpec((ATT_KV_HEADS, ATT_GROUP * cq, 1), lambda b, i: (0, 0, 0)),
        ],
        out_specs=pl.BlockSpec((tq, D_MODEL), lambda b, i: (b * nq + i, 0)),
        out_shape=jax.ShapeDtypeStruct((bt * s, D_MODEL), BF16),
        compiler_params=_cparams("arbitrary", "arbitrary"),
        name="swa_attention",
    )(h_main, per_head(kall), per_head(vall), bias4, sink4)


def _merge_kernel(x_ref, ssd_ref, att_ref, scb_ref, scc_ref, sch_ref, gssd_ref, gatt_ref, gsc_ref, hist_ref,
                  wssd_ref, watt_ref, wsc_ref, wo_ref, scw_ref, lng_ref, lnb_ref, rw_ref, rb_ref,
                  x1r_ref, ir_ref, gate_ref, cnt_ref, histout_ref, stat_ref,
                  ext_ref, base_ref, *, tm, tiles_per_batch):
    i = pl.program_id(0)

    @pl.when(i == 0)
    def _():
        base_ref[...] = jnp.zeros_like(base_ref)

    @pl.when(i % tiles_per_batch == 0)
    def _():
        ext_ref[0:SUBLANES, :] = hist_ref[0]

    u = scc_ref[...].astype(F32) * sch_ref[...].astype(F32)
    ext_ref[SUBLANES:SUBLANES + tm, :] = u
    off = SUBLANES - (SC_WIDTH - 1)
    conv = ext_ref[off:off + tm, :] * scw_ref[0:1, :]
    for j in range(1, SC_WIDTH):
        conv = conv + ext_ref[off + j:off + j + tm, :] * scw_ref[j:j + 1, :]
    tail = ext_ref[tm:tm + SUBLANES, :]
    ext_ref[0:SUBLANES, :] = tail
    histout_ref[0] = tail
    sc_pre = (scb_ref[...].astype(F32) * conv).astype(BF16)

    y_sc = jnp.dot(sc_pre, wsc_ref[...], preferred_element_type=F32)
    y_ssd = jnp.dot(ssd_ref[...], wssd_ref[...], preferred_element_type=F32)
    y_att = jnp.dot(att_ref[...], watt_ref[...], preferred_element_type=F32)
    merged = (_sigmoid(gssd_ref[...].astype(F32)) * y_ssd
              + _sigmoid(gatt_ref[...].astype(F32)) * y_att
              + _sigmoid(gsc_ref[...].astype(F32)) * y_sc)
    v = ALPHA * x_ref[...] + jnp.dot(merged.astype(BF16), wo_ref[...], preferred_element_type=F32)
    mu = jnp.mean(v, axis=-1, keepdims=True)
    vc = v - mu
    var = jnp.mean(vc * vc, axis=-1, keepdims=True)
    x1 = vc * lax.rsqrt(var + LN_EPS) * lng_ref[...] + lnb_ref[...]
    for j in range(ROW_TILES):
        x1r_ref[pl.ds(j, tm, stride=ROW_TILES), :] = x1[:, j * LANES:(j + 1) * LANES]

    rt = min(tm, ROUTE_TILE)
    lane = lax.broadcasted_iota(jnp.int32, (rt, LANES), 1)
    lane1 = lax.broadcasted_iota(jnp.int32, (1, LANES), 1)
    row8 = lax.broadcasted_iota(jnp.int32, (SUBLANES, LANES), 0)
    ri = lax.broadcasted_iota(jnp.int32, (rt, rt), 0)
    ci = lax.broadcasted_iota(jnp.int32, (rt, rt), 1)
    ltri = (ri > ci).astype(BF16)
    gi = lax.broadcasted_iota(jnp.int32, (LANES, LANES), 0)
    gj = lax.broadcasted_iota(jnp.int32, (LANES, LANES), 1)
    earlier_expert = jnp.logical_and(gi < gj, gi // N_EXPERTS == gj // N_EXPERTS).astype(BF16)
    logits_all = jnp.dot(x1.astype(BF16), rw_ref[...], preferred_element_type=F32) + rb_ref[...]
    for sub in range(tm // rt):
        rows = slice(sub * rt, (sub + 1) * rt)
        work = jnp.where(lane < N_EXPERTS, logits_all[rows], -jnp.inf)
        vals, idxs = [], []
        onehot = jnp.zeros((rt, LANES), F32)
        for k in range(TOP_K):
            mv = jnp.max(work, axis=-1, keepdims=True)
            mi = jnp.min(jnp.where(work == mv, lane, LANES), axis=-1, keepdims=True)
            vals.append(mv)
            idxs.append(mi)
            work = jnp.where(lane == mi, -jnp.inf, work)
            onehot = onehot + (lane == mi + k * N_EXPERTS).astype(F32)
        es = [jnp.exp(vk - vals[0]) for vk in vals]
        den = es[0] + es[1] + es[2] + es[3]
        prefix = jnp.dot(ltri, onehot.astype(BF16), preferred_element_type=F32)
        cnt = jnp.sum(onehot, axis=0, keepdims=True)
        base = base_ref[0:1, :]
        within = jnp.zeros((1, LANES), F32)
        tot = cnt
        for sh in range(1, TOP_K):
            rolled = pltpu.roll(cnt, sh * N_EXPERTS, 1)
            within = within + jnp.where(lane1 >= sh * N_EXPERTS, rolled, 0.0)
            tot = tot + rolled
        chunks = jnp.floor((tot + (SEG_ALIGN - 1)) * (1.0 / SEG_ALIGN))
        seg_start = SEG_ALIGN * jnp.dot(jnp.broadcast_to(chunks, (SUBLANES, LANES)).astype(BF16), earlier_expert,
                                        preferred_element_type=F32)[0:1, :]
        rank_all = prefix + (within + base)
        loc_all = prefix + (within + seg_start)
        ir = jnp.zeros((rt, LANES), jnp.int32)
        gt = jnp.zeros((rt, LANES), F32)
        for k in range(TOP_K):
            sel = lane == idxs[k] + k * N_EXPERTS
            rk = jnp.sum(jnp.where(sel, rank_all, 0.0), axis=-1, keepdims=True).astype(jnp.int32)
            lc = jnp.sum(jnp.where(sel, loc_all, 0.0), axis=-1, keepdims=True).astype(jnp.int32)
            ir = jnp.where(lane == k, idxs[k], ir)
            ir = jnp.where(lane == TOP_K + k, rk, ir)
            ir = jnp.where(lane == 2 * TOP_K + k, lc, ir)
            gt = jnp.where(lane == k, es[k] / den, gt)
        ir_ref[rows, :] = ir
        gate_ref[rows, :] = gt
        stat_ref[sub * SUBLANES:(sub + 1) * SUBLANES, :] = jnp.where(
            row8 == 0, tot, jnp.where(row8 == 1, seg_start, jnp.where(row8 == 2, base, 0.0)))
        base_ref[...] = jnp.broadcast_to(base + tot, base_ref.shape)
    cnt_ref[...] = base_ref[...]


def _merge(x2d, h_main, ssd_pre, att, schist8, p, bt, s, tm):
    t = bt * s
    nt = t // tm
    tiles_per_batch = s // tm
    row = lambda i: (i, 0)
    const2 = lambda i: (0, 0)
    hcol = lambda cidx: pl.BlockSpec((tm, D_MODEL), lambda i: (i, cidx))
    kern = functools.partial(_merge_kernel, tm=tm, tiles_per_batch=tiles_per_batch)
    return pl.pallas_call(
        kern,
        grid=(nt,),
        in_specs=[
            pl.BlockSpec((tm, D_MODEL), row),
            pl.BlockSpec((tm, SSD_D_INNER), row),
            pl.BlockSpec((tm, D_MODEL), row),
            hcol(COL_SCB), hcol(COL_SCC), hcol(COL_SCH), hcol(COL_GSSD), hcol(COL_GATT), hcol(COL_GSC),
            pl.BlockSpec((1, SUBLANES, D_MODEL), lambda i: (i // tiles_per_batch, 0, 0)),
            pl.BlockSpec((SSD_D_INNER, D_MODEL), const2, pipeline_mode=pl.Buffered(1)),
            pl.BlockSpec((D_MODEL, D_MODEL), const2, pipeline_mode=pl.Buffered(1)),
            pl.BlockSpec((D_MODEL, D_MODEL), const2, pipeline_mode=pl.Buffered(1)),
            pl.BlockSpec((D_MODEL, D_MODEL), const2, pipeline_mode=pl.Buffered(1)),
            pl.BlockSpec((SC_WIDTH, D_MODEL), const2),
            pl.BlockSpec((1, D_MODEL), const2),
            pl.BlockSpec((1, D_MODEL), const2),
            pl.BlockSpec((D_MODEL, LANES), const2),
            pl.BlockSpec((1, LANES), const2),
        ],
        out_specs=[
            pl.BlockSpec((tm * ROW_TILES, LANES), row),
            pl.BlockSpec((tm, LANES), row),
            pl.BlockSpec((tm, LANES), row),
            pl.BlockSpec((SUBLANES, LANES), const2),
            pl.BlockSpec((1, SUBLANES, D_MODEL), lambda i: (i // tiles_per_batch, 0, 0)),
            pl.BlockSpec((tm // min(tm, ROUTE_TILE) * SUBLANES, LANES), row),
        ],
        out_shape=[
            jax.ShapeDtypeStruct((t * ROW_TILES, LANES), F32),
            jax.ShapeDtypeStruct((t, LANES), jnp.int32),
            jax.ShapeDtypeStruct((t, LANES), F32),
            jax.ShapeDtypeStruct((SUBLANES, LANES), F32),
            jax.ShapeDtypeStruct((bt, SUBLANES, D_MODEL), F32),
            jax.ShapeDtypeStruct((t // min(tm, ROUTE_TILE) * SUBLANES, LANES), F32),
        ],
        scratch_shapes=[
            pltpu.VMEM((tm + SUBLANES, D_MODEL), F32),
            pltpu.VMEM((SUBLANES, LANES), F32),
        ],
        compiler_params=_cparams("arbitrary"),
        name="merge_ln_router",
    )(x2d, ssd_pre, att, h_main, h_main, h_main, h_main, h_main, h_main, schist8,
      p["w_ssd_out"], p["w_att_out"], p["w_sc_out"], p["w_o"], p["sc_w"], p["ln1_g"], p["ln1_b"],
      p["router_w"], p["router_b"])


def _row_copy_wait(src_hbm, dst, sem, n_rows):
    pltpu.make_async_copy(src_hbm.at[pl.ds(0, n_rows * ROW_TILES), :],
                          dst.at[pl.ds(0, n_rows * ROW_TILES), :], sem).wait()


def _dispatch_kernel(pstart_ref, pad_ref, ir_ref, x_ref, xs_hbm, zero_ref, sem, zsem, *, tm):
    @pl.when(pl.program_id(0) == 0)
    def _():
        zero_ref[...] = jnp.zeros_like(zero_ref)

        def zero_copy(row, n_rows):
            dst = pl.multiple_of(row * ROW_TILES, ROW_TILES)
            return pltpu.make_async_copy(zero_ref.at[pl.ds(0, n_rows * ROW_TILES), :],
                                         xs_hbm.at[pl.ds(dst, n_rows * ROW_TILES), :], zsem)

        def per_expert(start):
            def body(e, carry):
                first = pad_ref[e]
                n = pad_ref[N_EXPERTS + 1 + e]
                n_big = n // ZERO_ROWS

                def big(j, c):
                    cp = zero_copy(first + j * ZERO_ROWS, ZERO_ROWS)
                    cp.start() if start else cp.wait()
                    return c

                def small(j, c):
                    cp = zero_copy(first + j, 1)
                    cp.start() if start else cp.wait()
                    return c

                lax.fori_loop(0, n_big, big, 0)
                lax.fori_loop(n_big * ZERO_ROWS, n, small, 0)
                return carry

            lax.fori_loop(0, N_EXPERTS + 1, body, 0)

        per_expert(True)
        per_expert(False)

    def issue(tk, carry):
        src = pl.multiple_of(tk * ROW_TILES, ROW_TILES)
        for k in range(TOP_K):
            dest = pstart_ref[ir_ref[tk * SUBLANES + k]] + ir_ref[tk * SUBLANES + TOP_K + k]
            dst = pl.multiple_of(dest * ROW_TILES, ROW_TILES)
            pltpu.make_async_copy(x_ref.at[pl.ds(src, ROW_TILES), :],
                                  xs_hbm.at[pl.ds(dst, ROW_TILES), :], sem).start()
        return carry

    lax.fori_loop(0, tm, issue, 0)
    for _ in range(TOP_K):
        _row_copy_wait(x_ref, xs_hbm, sem, tm)


def _dispatch(x1r, ir_flat, pstart, pad_info, rows, t, tm):
    return pl.pallas_call(
        functools.partial(_dispatch_kernel, tm=tm),
        grid_spec=pltpu.PrefetchScalarGridSpec(
            num_scalar_prefetch=2,
            grid=(t // tm,),
            in_specs=[
                pl.BlockSpec((tm * SUBLANES,), lambda i, ps, pd: (i,), memory_space=pltpu.SMEM),
                pl.BlockSpec((tm * ROW_TILES, LANES), lambda i, ps, pd: (i, 0)),
            ],
            out_specs=pl.BlockSpec(memory_space=pl.ANY),
            scratch_shapes=[pltpu.VMEM((ZERO_ROWS * ROW_TILES, LANES), F32),
                            pltpu.SemaphoreType.DMA, pltpu.SemaphoreType.DMA],
        ),
        out_shape=jax.ShapeDtypeStruct((rows * ROW_TILES, LANES), F32),
        compiler_params=_cparams("arbitrary"),
        name="moe_dispatch",
    )(pstart, pad_info, ir_flat, x1r)


def _moe_kernel(be_ref, nv_ref, x_ref, wu_ref, bu_ref, wd_ref, bd_ref, o_ref, wub_ref, wdb_ref, *, tmo):
    i = pl.program_id(0)
    valid = i < nv_ref[0]
    new_expert = jnp.logical_or(i == 0, be_ref[i] != be_ref[jnp.maximum(i - 1, 0)])

    @pl.when(jnp.logical_and(valid, new_expert))
    def _():
        wub_ref[...] = wu_ref[0, 0].astype(BF16)
        wdb_ref[...] = wd_ref[0, 0].astype(BF16)

    @pl.when(valid)
    def _():
        x = jnp.concatenate([x_ref[pl.ds(j, tmo, stride=ROW_TILES), :] for j in range(ROW_TILES)], axis=1)
        h = jnp.dot(x.astype(BF16), wub_ref[...], preferred_element_type=F32) + bu_ref[0, 0]
        gate = jnp.minimum(h[:, :D_FF], SWIGLU_LIMIT)
        up = jnp.clip(h[:, D_FF:], -SWIGLU_LIMIT, SWIGLU_LIMIT)
        act = (up + 1.0) * gate * _sigmoid(SWIGLU_ALPHA * gate)
        o = jnp.dot(act.astype(BF16), wdb_ref[...], preferred_element_type=F32) + bd_ref[0, 0]
        for j in range(ROW_TILES):
            o_ref[pl.ds(j, tmo, stride=ROW_TILES), :] = o[:, j * LANES:(j + 1) * LANES]

    @pl.when(jnp.logical_not(valid))
    def _():
        o_ref[...] = jnp.zeros_like(o_ref)


def _moe_experts(xs, block_e, nvalid, p, n_blocks, tmo):
    l = p["layer"]
    last_valid = lambda i, nv: jnp.minimum(i, nv[0] - 1)
    return pl.pallas_call(
        functools.partial(_moe_kernel, tmo=tmo),
        grid_spec=pltpu.PrefetchScalarGridSpec(
            num_scalar_prefetch=2,
            grid=(n_blocks,),
            in_specs=[
                pl.BlockSpec((tmo * ROW_TILES, LANES), lambda i, be, nv: (last_valid(i, nv), 0)),
                pl.BlockSpec((1, 1, D_MODEL, 2 * D_FF), lambda i, be, nv: (l, be[i], 0, 0)),
                pl.BlockSpec((1, 1, 1, 2 * D_FF), lambda i, be, nv: (l, be[i], 0, 0)),
                pl.BlockSpec((1, 1, D_FF, D_MODEL), lambda i, be, nv: (l, be[i], 0, 0)),
                pl.BlockSpec((1, 1, 1, D_MODEL), lambda i, be, nv: (l, be[i], 0, 0)),
            ],
            out_specs=pl.BlockSpec((tmo * ROW_TILES, LANES), lambda i, be, nv: (i, 0)),
            scratch_shapes=[pltpu.VMEM((D_MODEL, 2 * D_FF), BF16), pltpu.VMEM((D_FF, D_MODEL), BF16)],
        ),
        out_shape=jax.ShapeDtypeStruct(xs.shape, F32),
        compiler_params=_cparams("arbitrary"),
        name="moe_experts",
    )(block_e, nvalid, xs, p["w_up"], p["b_up"], p["w_down"], p["b_down"])


def _combine_kernel(pstart_ref, ir_ref, irn_ref, x1r_ref, gate_ref, lng_ref, lnb_ref, eo_hbm, o_ref, buf_ref, sem,
                    *, tm, nt):
    i = pl.program_id(0)
    slot = i % 2

    def issue(idx_ref, s):
        def body(tk, carry):
            dst = pl.multiple_of(tk * ROW_TILES, ROW_TILES)
            for k in range(TOP_K):
                dest = pstart_ref[idx_ref[tk * SUBLANES + k]] + idx_ref[tk * SUBLANES + TOP_K + k]
                src = pl.multiple_of(dest * ROW_TILES, ROW_TILES)
                pltpu.make_async_copy(eo_hbm.at[pl.ds(src, ROW_TILES), :],
                                      buf_ref.at[s, k, pl.ds(dst, ROW_TILES), :], sem.at[s]).start()
            return carry

        lax.fori_loop(0, tm, body, 0)

    @pl.when(i == 0)
    def _():
        issue(ir_ref, 0)

    @pl.when(i + 1 < nt)
    def _():
        issue(irn_ref, 1 - slot)

    for k in range(TOP_K):
        _row_copy_wait(eo_hbm, buf_ref.at[slot, k], sem.at[slot], tm)

    gates = gate_ref[...]
    vs = []
    tot = jnp.zeros((tm, 1), F32)
    for j in range(ROW_TILES):
        acc = ALPHA * x1r_ref[pl.ds(j, tm, stride=ROW_TILES), :]
        for k in range(TOP_K):
            acc = acc + buf_ref[slot, k, pl.ds(j, tm, stride=ROW_TILES), :] * gates[:, k:k + 1]
        vs.append(acc)
        tot = tot + jnp.sum(acc, axis=-1, keepdims=True)
    mu = tot * (1.0 / D_MODEL)
    sq = jnp.zeros((tm, 1), F32)
    for j in range(ROW_TILES):
        vs[j] = vs[j] - mu
        sq = sq + jnp.sum(vs[j] * vs[j], axis=-1, keepdims=True)
    inv = lax.rsqrt(sq * (1.0 / D_MODEL) + LN_EPS)
    for j in range(ROW_TILES):
        sl = slice(j * LANES, (j + 1) * LANES)
        o_ref[:, sl] = vs[j] * inv * lng_ref[:, sl] + lnb_ref[:, sl]


def _combine(x1r, ir_flat, gates, pstart, eo, p, t, tm):
    nt = t // tm
    return pl.pallas_call(
        functools.partial(_combine_kernel, tm=tm, nt=nt),
        grid_spec=pltpu.PrefetchScalarGridSpec(
            num_scalar_prefetch=1,
            grid=(nt,),
            in_specs=[
                pl.BlockSpec((tm * SUBLANES,), lambda i, ps: (i,), memory_space=pltpu.SMEM),
                pl.BlockSpec((tm * SUBLANES,), lambda i, ps: (jnp.minimum(i + 1, nt - 1),), memory_space=pltpu.SMEM),
                pl.BlockSpec((tm * ROW_TILES, LANES), lambda i, ps: (i, 0)),
                pl.BlockSpec((tm, LANES), lambda i, ps: (i, 0)),
                pl.BlockSpec((1, D_MODEL), lambda i, ps: (0, 0)),
                pl.BlockSpec((1, D_MODEL), lambda i, ps: (0, 0)),
                pl.BlockSpec(memory_space=pl.ANY),
            ],
            out_specs=pl.BlockSpec((tm, D_MODEL), lambda i, ps: (i, 0)),
            scratch_shapes=[pltpu.VMEM((2, TOP_K, tm * ROW_TILES, LANES), F32), pltpu.SemaphoreType.DMA((2,))],
        ),
        out_shape=jax.ShapeDtypeStruct((t, D_MODEL), F32),
        compiler_params=_cparams("arbitrary"),
        name="moe_combine_ln",
    )(pstart, ir_flat, ir_flat, x1r, gates, p["ln2_g"], p["ln2_b"], eo)


def _combine_seg_kernel(seg_ref, segn_ref, ir_ref, x1r_ref, gate_ref, lng_ref, lnb_ref, eo_hbm, o_ref,
                        stage_ref, sem, *, rt, nt):
    i = pl.program_id(0)
    slot = i % 2
    sr = stage_ref.shape[1] // ROW_TILES

    @pl.when(i == 0)
    def _():
        stage_ref[...] = jnp.zeros_like(stage_ref)

    def copies(s_ref, s, start):
        def per_expert(e, carry):
            n_copies = (s_ref[e] + (SEG_ALIGN - 1)) // SEG_ALIGN
            dst0 = s_ref[N_EXPERTS + e]
            src0 = s_ref[2 * N_EXPERTS + e]

            def one(j, c):
                src = pl.multiple_of((src0 + j * SEG_ALIGN) * ROW_TILES, ROW_TILES)
                dst = pl.multiple_of((dst0 + j * SEG_ALIGN) * ROW_TILES, SEG_ALIGN * ROW_TILES)
                cp = pltpu.make_async_copy(eo_hbm.at[pl.ds(src, SEG_ALIGN * ROW_TILES), :],
                                           stage_ref.at[s, pl.ds(dst, SEG_ALIGN * ROW_TILES), :], sem.at[s])
                cp.start() if start else cp.wait()
                return c

            lax.fori_loop(0, n_copies, one, 0)
            return carry

        lax.fori_loop(0, N_EXPERTS, per_expert, 0)

    @pl.when(i == 0)
    def _():
        copies(seg_ref, 0, True)

    @pl.when(i + 1 < nt)
    def _():
        copies(segn_ref, 1 - slot, True)

    copies(seg_ref, slot, False)

    rows = jnp.concatenate([stage_ref[slot, pl.ds(j, sr, stride=ROW_TILES), :] for j in range(ROW_TILES)], axis=1)
    rows_hi = rows.astype(BF16)
    rows_lo = (rows - rows_hi.astype(F32)).astype(BF16)
    pos = lax.broadcasted_iota(jnp.int32, (rt, sr), 1)
    ir = ir_ref[...]
    gates = gate_ref[...]
    weight = jnp.zeros((rt, sr), F32)
    for k in range(TOP_K):
        weight = jnp.where(pos == ir[:, 2 * TOP_K + k:2 * TOP_K + k + 1], gates[:, k:k + 1], weight)
    w_hi = weight.astype(BF16)
    w_lo = (weight - w_hi.astype(F32)).astype(BF16)
    moe = (jnp.dot(w_hi, rows_hi, preferred_element_type=F32)
           + jnp.dot(w_hi, rows_lo, preferred_element_type=F32)
           + jnp.dot(w_lo, rows_hi, preferred_element_type=F32))
    x1 = jnp.concatenate([x1r_ref[pl.ds(j, rt, stride=ROW_TILES), :] for j in range(ROW_TILES)], axis=1)
    v = ALPHA * x1 + moe
    mu = jnp.mean(v, axis=-1, keepdims=True)
    vc = v - mu
    var = jnp.mean(vc * vc, axis=-1, keepdims=True)
    o_ref[...] = vc * lax.rsqrt(var + LN_EPS) * lng_ref[...] + lnb_ref[...]


def _combine_seg(x1r, ir, gates, seg, eo, p, t, rt):
    nt = t // rt
    sr = rt * TOP_K + N_EXPERTS * SEG_ALIGN
    seg_len = 4 * N_EXPERTS
    return pl.pallas_call(
        functools.partial(_combine_seg_kernel, rt=rt, nt=nt),
        grid=(nt,),
        in_specs=[
            pl.BlockSpec((seg_len,), lambda i: (i,), memory_space=pltpu.SMEM),
            pl.BlockSpec((seg_len,), lambda i: (jnp.minimum(i + 1, nt - 1),), memory_space=pltpu.SMEM),
            pl.BlockSpec((rt, LANES), lambda i: (i, 0)),
            pl.BlockSpec((rt * ROW_TILES, LANES), lambda i: (i, 0)),
            pl.BlockSpec((rt, LANES), lambda i: (i, 0)),
            pl.BlockSpec((1, D_MODEL), lambda i: (0, 0)),
            pl.BlockSpec((1, D_MODEL), lambda i: (0, 0)),
            pl.BlockSpec(memory_space=pl.ANY),
        ],
        out_specs=pl.BlockSpec((rt, D_MODEL), lambda i: (i, 0)),
        out_shape=jax.ShapeDtypeStruct((t, D_MODEL), F32),
        scratch_shapes=[pltpu.VMEM((2, sr * ROW_TILES, LANES), F32), pltpu.SemaphoreType.DMA((2,))],
        compiler_params=_cparams("arbitrary"),
        name="moe_combine_ln",
    )(seg, seg, ir, x1r, gates, p["ln2_g"], p["ln2_b"], eo)


def _moe(x1r, ir, gates, cnt, stat, p, t, tm, tmo, rt):
    n_assign = t * TOP_K
    n_blocks = (n_assign + N_EXPERTS * (tmo - 1) + tmo - 1) // tmo
    rows = n_blocks * tmo
    counts = cnt[0, :N_EXPERTS].astype(jnp.int32)
    padded = (counts + tmo - 1) // tmo * tmo
    pend = jnp.cumsum(padded)
    pstart = (pend - padded).astype(jnp.int32)
    pad_info = jnp.concatenate([pstart + counts, pend[-1:], padded - counts, rows - pend[-1:]]).astype(jnp.int32)
    block_start = jnp.arange(n_blocks, dtype=jnp.int32) * tmo
    block_e = jnp.minimum(jnp.sum((pend[None, :] <= block_start[:, None]).astype(jnp.int32), axis=1),
                          N_EXPERTS - 1)
    nvalid = (pend[-1:] // tmo).astype(jnp.int32)
    ir_flat = ir[:, :SUBLANES].reshape(t * SUBLANES)
    xs = _dispatch(x1r, ir_flat, pstart, pad_info, rows, t, tm)
    eo = _moe_experts(xs, block_e, nvalid, p, n_blocks, tmo)
    st3 = stat.reshape(t // rt, SUBLANES, LANES)[:, :3, :N_EXPERTS].astype(jnp.int32)
    seg = jnp.concatenate([st3[:, 0], st3[:, 1], st3[:, 2] + pstart[None, :], jnp.zeros_like(st3[:, 0])],
                          axis=1).reshape(-1)
    return _combine_seg(x1r, ir, gates, seg, eo, p, t, rt)


def _t5_bucket(rel):
    half = N_BUCKETS // 2
    max_exact = half // 2
    ret = jnp.where(rel > 0, half, 0)
    n = jnp.abs(rel)
    nf = jnp.maximum(n, 1).astype(F32)
    large = max_exact + (jnp.log(nf / max_exact) / math.log(MAX_DISTANCE / max_exact)
                         * (half - max_exact)).astype(jnp.int32)
    large = jnp.minimum(large, half - 1)
    return ret + jnp.where(n < max_exact, n, large)


def _rel_bias(table, cq):
    koff = jnp.arange(WINDOW + cq) - WINDOW
    rel = koff[None, :] - jnp.arange(cq)[:, None]
    return jnp.transpose(table[_t5_bucket(rel)].astype(F32), (2, 0, 1))


def _pad_rows_front(a, rows):
    pad = jnp.zeros(a.shape[:1] + (rows - a.shape[1],) + a.shape[2:], a.dtype)
    return jnp.concatenate([pad, a], axis=1)


def _layer_params(l, w_in, ssd_conv_w, ssd_conv_b, ssd_dt_bias, ssd_a_log, ssd_d, ssd_norm_w, w_ssd_out,
                  attn_sinks, w_attn_out, sc_conv_w, w_sc_out, w_o, ln1_g, ln1_b,
                  router_w, router_b, w_up, b_up, w_down, b_down, ln2_g, ln2_b):
    wi = w_in[l]
    w_dt = wi[:, OFF_DT:OFF_Q]
    return {
        "w_main": jnp.concatenate([wi[:, :OFF_DT], wi[:, OFF_Q:OFF_K], wi[:, OFF_SCB:]], axis=1).astype(BF16),
        "w_kv": wi[:, OFF_K:OFF_SCB].astype(BF16),
        "w_dt": jnp.pad(w_dt, ((0, 0), (0, LANES - SSD_N_HEADS))).astype(BF16),
        "w_dtT": w_dt.T.astype(BF16),
        "conv_w": ssd_conv_w[l],
        "conv_b": ssd_conv_b[l][None, :],
        "dt_b128": jnp.pad(ssd_dt_bias[l], (0, LANES - SSD_N_HEADS))[None, :],
        "dt_bT": ssd_dt_bias[l][:, None],
        "a_log128": jnp.pad(ssd_a_log[l], (0, LANES - SSD_N_HEADS))[None, :],
        "expand": jnp.asarray(_head_expand_matrix(), BF16),
        "a_logT": ssd_a_log[l][:, None],
        "d_exp": jnp.repeat(ssd_d[l], SSD_HEAD_DIM)[None, :],
        "norm_w": ssd_norm_w[l][None, :],
        "w_ssd_out": w_ssd_out[l].astype(BF16),
        "sinks": attn_sinks[l],
        "w_att_out": w_attn_out[l].astype(BF16),
        "sc_w": sc_conv_w[l],
        "w_sc_out": w_sc_out[l].astype(BF16),
        "w_o": w_o[l].astype(BF16),
        "ln1_g": ln1_g[l][None, :],
        "ln1_b": ln1_b[l][None, :],
        "router_w": jnp.pad(router_w[l], ((0, 0), (0, LANES - N_EXPERTS))).astype(BF16),
        "router_b": jnp.pad(router_b[l], (0, LANES - N_EXPERTS))[None, :],
        "layer": l,
        "w_up": w_up,
        "b_up": b_up[:, :, None, :],
        "w_down": w_down,
        "b_down": b_down[:, :, None, :],
        "ln2_g": ln2_g[l][None, :],
        "ln2_b": ln2_b[l][None, :],
    }


def _trunk_layer(x2d, bt, s, p, bias, ssd_hist, ssd_state, sc_hist, kv_cache, cfg):
    t = bt * s
    h_main = _matmul(x2d, p["w_main"], BF16, cfg["tm_in"], cfg["tn_in"])
    h_kv = _matmul(x2d, p["w_kv"], F32, cfg["tm_in"], 2 * ATT_KV_DIM)
    k_new = h_kv[:, :ATT_KV_DIM].reshape(bt, s, ATT_KV_DIM)
    v_new = h_kv[:, ATT_KV_DIM:].reshape(bt, s, ATT_KV_DIM)

    if kv_cache is None:
        zpad = jnp.zeros((bt, WINDOW, ATT_KV_DIM), BF16)
        kall = jnp.concatenate([zpad, k_new.astype(BF16)], axis=1)
        vall = jnp.concatenate([zpad, v_new.astype(BF16)], axis=1)
    else:
        kall = jnp.concatenate([kv_cache[0].astype(BF16), k_new.astype(BF16)], axis=1)
        vall = jnp.concatenate([kv_cache[1].astype(BF16), v_new.astype(BF16)], axis=1)
    att = _attention(h_main, kall, vall, bias, p["sinks"], bt, s, cfg["cq"], cfg["n_sub"], kv_cache is None)

    if ssd_hist is None:
        hist8 = jnp.zeros((bt, SUBLANES, SSD_CONV_DIM), F32)
        st0 = jnp.zeros((bt, SSD_N_GROUPS, SSD_D_STATE, SSD_HEADS_PER_GROUP * SSD_HEAD_DIM), F32)
        schist8 = jnp.zeros((bt, SUBLANES, D_MODEL), F32)
    else:
        hist8 = _pad_rows_front(ssd_hist, SUBLANES)
        st0 = _state_to_kernel(ssd_state)
        schist8 = _pad_rows_front(sc_hist, SUBLANES)
    ssd_pre, st_out, hist_out = _ssd_grouped(x2d, h_main, hist8, st0, p, bt, s, cfg["cq"])

    x1r, ir, gates, cnt, schist_out, stat = _merge(x2d, h_main, ssd_pre, att, schist8, p, bt, s, cfg["tm_merge"])
    x2 = _moe(x1r, ir, gates, cnt, stat, p, t, cfg["tm_moe"], cfg["tmo"], min(cfg["tm_merge"], ROUTE_TILE))

    n_keep = min(s, WINDOW)
    new_k = k_new[:, s - n_keep:].reshape(bt, n_keep, ATT_KV_HEADS, ATT_HEAD_DIM)
    new_v = v_new[:, s - n_keep:].reshape(bt, n_keep, ATT_KV_HEADS, ATT_HEAD_DIM)
    new_state = _state_from_kernel(st_out)
    new_hist = hist_out[:, SUBLANES - (SSD_CONV - 1):]
    new_sc_hist = schist_out[:, SUBLANES - (SC_WIDTH - 1):]
    return x2, new_k, new_v, new_state, new_hist, new_sc_hist


def _config(bt, s, prompt):
    t = bt * s
    if prompt:
        return {"tm_in": min(t, 2048), "tn_in": 1024, "cq": CHUNK, "n_sub": min(8, s // CHUNK),
                "tm_merge": min(s, 512), "tm_moe": min(t, 256), "tmo": min(512, max(64, t // 32))}
    return {"tm_in": min(t, 256), "tn_in": 1024, "cq": s, "n_sub": 1,
            "tm_merge": s, "tm_moe": min(t, 256), "tmo": 64}


def _forward(x_prompt, x_sample, cache_attn_k, cache_attn_v, state_ssd, state_ssd_conv, state_short_conv,
             rel_bias, layer_weights):
    bp, sp, _ = x_prompt.shape
    bd, sd, _ = x_sample.shape
    cfg_p = _config(bp, sp, True)
    cfg_d = _config(bd, sd, False)
    bias_p = _rel_bias(rel_bias, cfg_p["cq"])
    bias_d = _rel_bias(rel_bias, cfg_d["cq"])
    yp = x_prompt.reshape(bp * sp, D_MODEL)
    ys = x_sample.reshape(bd * sd, D_MODEL)
    outs_p, outs_d = [], []
    for l in range(DEPTH):
        p = _layer_params(l, *layer_weights)
        yp, *rest = _trunk_layer(yp, bp, sp, p, bias_p, None, None, None, None, cfg_p)
        outs_p.append(rest)
        cache = (cache_attn_k[l].reshape(bd, WINDOW, ATT_KV_DIM), cache_attn_v[l].reshape(bd, WINDOW, ATT_KV_DIM))
        ys, *rest = _trunk_layer(ys, bd, sd, p, bias_d, state_ssd_conv[l], state_ssd[l], state_short_conv[l],
                                 cache, cfg_d)
        outs_d.append(rest)
    stack = lambda outs, i: jnp.stack([o[i] for o in outs])
    return (yp.reshape(bp, sp, D_MODEL), ys.reshape(bd, sd, D_MODEL),
            stack(outs_p, 0), stack(outs_p, 1), stack(outs_p, 2), stack(outs_p, 3), stack(outs_p, 4),
            stack(outs_d, 0), stack(outs_d, 1), stack(outs_d, 2), stack(outs_d, 3), stack(outs_d, 4))


def kernel(x_prompt, x_sample, cache_attn_k, cache_attn_v, state_ssd, state_ssd_conv, state_short_conv, w_in, ssd_conv_w, ssd_conv_b, ssd_dt_bias, ssd_a_log, ssd_d, ssd_norm_w, w_ssd_out, attn_sinks, w_attn_out, rel_bias, sc_conv_w, w_sc_out, w_o, ln1_g, ln1_b, router_w, router_b, w_up, b_up, w_down, b_down, ln2_g, ln2_b):
    layer_weights = (w_in, ssd_conv_w, ssd_conv_b, ssd_dt_bias, ssd_a_log, ssd_d, ssd_norm_w, w_ssd_out,
                     attn_sinks, w_attn_out, sc_conv_w, w_sc_out, w_o, ln1_g, ln1_b,
                     router_w, router_b, w_up, b_up, w_down, b_down, ln2_g, ln2_b)
    return _forward(x_prompt, x_sample, cache_attn_k, cache_attn_v, state_ssd, state_ssd_conv, state_short_conv,
                    rel_bias, layer_weights)
```

```python
import functools
import math

import jax
import jax.numpy as jnp
import numpy as np
from jax import lax
from jax.experimental import pallas as pl
from jax.experimental.pallas import tpu as pltpu

F32 = jnp.float32
BF16 = jnp.bfloat16

D_MODEL = 1024
DEPTH = 2
CHUNK = 64
SSD_D_INNER = 2 * D_MODEL
SSD_HEAD_DIM = 64
SSD_N_HEADS = SSD_D_INNER // SSD_HEAD_DIM
SSD_N_GROUPS = 8
SSD_D_STATE = 128
SSD_CONV = 4
SSD_GN = SSD_N_GROUPS * SSD_D_STATE
SSD_CONV_DIM = SSD_D_INNER + 2 * SSD_GN
SSD_HEADS_PER_GROUP = SSD_N_HEADS // SSD_N_GROUPS
SSD_NORM_GROUP = SSD_D_INNER // SSD_N_GROUPS
ATT_HEAD_DIM = 64
ATT_HEADS = D_MODEL // ATT_HEAD_DIM
ATT_KV_HEADS = ATT_HEADS // 4
ATT_GROUP = ATT_HEADS // ATT_KV_HEADS
ATT_KV_DIM = ATT_KV_HEADS * ATT_HEAD_DIM
WINDOW = 128
N_BUCKETS = 32
MAX_DISTANCE = 128
SC_WIDTH = 3
N_EXPERTS = 32
TOP_K = 4
D_FF = D_MODEL
SWIGLU_LIMIT = 7.0
SWIGLU_ALPHA = 1.702
LN_EPS = 1e-5
ALPHA = (2.0 * DEPTH) ** 0.25

LANES = 128
SUBLANES = 8
ROW_TILES = D_MODEL // LANES
VMEM_LIMIT = 56 * 1024 * 1024
ZERO_ROWS = 16
ROUTE_TILE = 256
SEG_ALIGN = 8

COL_Z, COL_XS, COL_BC = 0, 1, 2
COL_Q, COL_SCB, COL_SCC, COL_SCH, COL_GSSD, COL_GATT, COL_GSC = 6, 7, 8, 9, 10, 11, 12
N_MAIN = 13 * D_MODEL
OFF_DT = SSD_D_INNER + SSD_CONV_DIM
OFF_Q = OFF_DT + SSD_N_HEADS
OFF_K = OFF_Q + D_MODEL
OFF_SCB = OFF_K + 2 * ATT_KV_DIM


def _sigmoid(v):
    return 0.5 * jnp.tanh(0.5 * v) + 0.5


def _cparams(*sem):
    return pltpu.CompilerParams(dimension_semantics=sem, vmem_limit_bytes=VMEM_LIMIT)


def _in_proj_kernel(x_ref, w_ref, wkv_ref, o_ref, kv_ref, xb_ref, *, n_main):
    j = pl.program_id(1)

    @pl.when(j == 0)
    def _():
        xb_ref[...] = x_ref[...].astype(BF16)

    @pl.when(j < n_main)
    def _():
        o_ref[...] = jnp.dot(xb_ref[...], w_ref[...], preferred_element_type=F32).astype(o_ref.dtype)

    @pl.when(j == n_main)
    def _():
        kv_ref[...] = jnp.dot(xb_ref[...], wkv_ref[...], preferred_element_type=F32)


def _in_proj(x, w, w_kv, tm, tn):
    t, k = x.shape
    n = w.shape[1]
    n_kv = w_kv.shape[1]
    n_main = n // tn
    main_col = lambda i, j: (i, jnp.minimum(j, n_main - 1))
    return pl.pallas_call(
        functools.partial(_in_proj_kernel, n_main=n_main),
        grid=(t // tm, n_main + 1),
        in_specs=[pl.BlockSpec((tm, k), lambda i, j: (i, 0)),
                  pl.BlockSpec((k, tn), lambda i, j: (0, jnp.minimum(j, n_main - 1))),
                  pl.BlockSpec((k, n_kv), lambda i, j: (0, 0))],
        out_specs=[pl.BlockSpec((tm, tn), main_col),
                   pl.BlockSpec((tm, n_kv), lambda i, j: (i, 0))],
        out_shape=[jax.ShapeDtypeStruct((t, n), BF16), jax.ShapeDtypeStruct((t, n_kv), F32)],
        scratch_shapes=[pltpu.VMEM((tm, k), BF16)],
        compiler_params=_cparams("arbitrary", "arbitrary"),
        name="in_proj",
    )(x, w, w_kv)


def _ssd_kernel(x_ref, z_ref, xs_ref, bc_ref, hist_ref, st0_ref,
                wdt_ref, wdtT_ref, convw_ref, convb_ref, dtb_ref, dtbT_ref, alog_ref, alogT_ref,
                dexp_ref, normw_ref,
                y_ref, stout_ref, histout_ref,
                ext_ref, st_ref, ybuf_ref, *, L, nc):
    c = pl.program_id(1)

    @pl.when(c == 0)
    def _():
        ext_ref[0:SUBLANES, :] = hist_ref[0]
        st_ref[...] = st0_ref[0]

    xbc = jnp.concatenate([xs_ref[...], bc_ref[...]], axis=1).astype(F32)
    ext_ref[SUBLANES:SUBLANES + L, :] = xbc
    base = SUBLANES - (SSD_CONV - 1)
    conv = convb_ref[...]
    for j in range(SSD_CONV):
        conv = conv + ext_ref[base + j:base + j + L, :] * convw_ref[j:j + 1, :]
    tail = ext_ref[L:L + SUBLANES, :]
    ext_ref[0:SUBLANES, :] = tail
    histout_ref[0] = tail
    xc = conv * _sigmoid(conv)
    xs_c = xc[:, :SSD_D_INNER]
    b_m = xc[:, SSD_D_INNER:SSD_D_INNER + SSD_GN]
    c_m = xc[:, SSD_D_INNER + SSD_GN:]

    xb = x_ref[...].astype(BF16)
    dtr = jnp.dot(xb, wdt_ref[...], preferred_element_type=F32)[:, :SSD_N_HEADS]
    dtrT = lax.dot_general(wdtT_ref[...], xb, (((1,), (1,)), ((), ())),
                           preferred_element_type=F32)

    def softplus(v):
        return jnp.maximum(v, 0.0) + jnp.log1p(jnp.exp(-jnp.abs(v)))

    dt = softplus(dtr + dtb_ref[...])
    dtT = softplus(dtrT + dtbT_ref[...])
    adt = dt * (-jnp.exp(alog_ref[...]))
    adtT = dtT * (-jnp.exp(alogT_ref[...]))
    ri = lax.broadcasted_iota(jnp.int32, (L, L), 0)
    ci = lax.broadcasted_iota(jnp.int32, (L, L), 1)
    tri = ri >= ci
    acs = jnp.dot(tri.astype(F32), adt, precision=lax.Precision.HIGHEST,
                  preferred_element_type=F32)
    acsT = jnp.dot(adtT, (ri <= ci).astype(F32), precision=lax.Precision.HIGHEST,
                   preferred_element_type=F32)
    totT = jnp.sum(adtT, axis=1, keepdims=True)
    eacs = jnp.exp(acs)
    wT = jnp.exp(totT - acsT) * dtT
    cdT = jnp.exp(totT)

    for g in range(SSD_N_GROUPS):
        bg = b_m[:, g * SSD_D_STATE:(g + 1) * SSD_D_STATE]
        cg_b = c_m[:, g * SSD_D_STATE:(g + 1) * SSD_D_STATE].astype(BF16)
        cb = lax.dot_general(cg_b, bg.astype(BF16), (((1,), (1,)), ((), ())),
                             preferred_element_type=F32)
        if L < LANES:
            bg = jnp.concatenate([bg, jnp.zeros((LANES - L, SSD_D_STATE), F32)], axis=0)
        bgT = bg.T[:, :L]
        for r in range(SSD_HEADS_PER_GROUP):
            h = g * SSD_HEADS_PER_GROUP + r
            col = acs[:, h:h + 1]
            row = acsT[h:h + 1, :]
            decay = jnp.exp(jnp.where(tri, col - row, -jnp.inf))
            m = (cb * decay * dtT[h:h + 1, :]).astype(BF16)
            xh_b = xs_c[:, h * SSD_HEAD_DIM:(h + 1) * SSD_HEAD_DIM].astype(BF16)
            st = st_ref[h]
            yd = jnp.dot(m, xh_b, preferred_element_type=F32)
            yo = jnp.dot(cg_b, st.astype(BF16), preferred_element_type=F32)
            ybuf_ref[:, h * SSD_HEAD_DIM:(h + 1) * SSD_HEAD_DIM] = yd + yo * eacs[:, h:h + 1]
            wh = (bgT * wT[h:h + 1, :]).astype(BF16)
            stc = jnp.dot(wh, xh_b, preferred_element_type=F32)
            st_ref[h] = st * cdT[h:h + 1, :] + stc

    y = ybuf_ref[...] + dexp_ref[...] * xs_c
    z = z_ref[...].astype(F32)
    y = y * (z * _sigmoid(z))
    for g in range(SSD_N_GROUPS):
        sl = slice(g * SSD_NORM_GROUP, (g + 1) * SSD_NORM_GROUP)
        blk = y[:, sl]
        ms = jnp.mean(blk * blk, axis=-1, keepdims=True)
        y_ref[:, sl] = (blk * lax.rsqrt(ms + LN_EPS) * normw_ref[:, sl]).astype(y_ref.dtype)

    @pl.when(c == nc - 1)
    def _():
        stout_ref[0] = st_ref[...]


def _ssd(x2d, h_main, hist8, st0, p, bt, s, L):
    nc = s // L
    t = bt * s
    w2 = 2 * D_MODEL
    row = lambda b, c: (b * nc + c, 0)
    const2 = lambda b, c: (0, 0)
    kern = functools.partial(_ssd_kernel, L=L, nc=nc)
    return pl.pallas_call(
        kern,
        grid=(bt, nc),
        in_specs=[
            pl.BlockSpec((L, D_MODEL), row),
            pl.BlockSpec((L, w2), lambda b, c: (b * nc + c, COL_Z)),
            pl.BlockSpec((L, w2), lambda b, c: (b * nc + c, COL_XS)),
            pl.BlockSpec((L, w2), lambda b, c: (b * nc + c, COL_BC)),
            pl.BlockSpec((1, SUBLANES, SSD_CONV_DIM), lambda b, c: (b, 0, 0)),
            pl.BlockSpec((1, SSD_N_HEADS, SSD_D_STATE, SSD_HEAD_DIM), lambda b, c: (b, 0, 0, 0)),
            pl.BlockSpec((D_MODEL, LANES), const2),
            pl.BlockSpec((SSD_N_HEADS, D_MODEL), const2),
            pl.BlockSpec((SSD_CONV, SSD_CONV_DIM), const2),
            pl.BlockSpec((1, SSD_CONV_DIM), const2),
            pl.BlockSpec((1, SSD_N_HEADS), const2),
            pl.BlockSpec((SSD_N_HEADS, 1), const2),
            pl.BlockSpec((1, SSD_N_HEADS), const2),
            pl.BlockSpec((SSD_N_HEADS, 1), const2),
            pl.BlockSpec((1, SSD_D_INNER), const2),
            pl.BlockSpec((1, SSD_D_INNER), const2),
        ],
        out_specs=[
            pl.BlockSpec((L, SSD_D_INNER), row),
            pl.BlockSpec((1, SSD_N_HEADS, SSD_D_STATE, SSD_HEAD_DIM), lambda b, c: (b, 0, 0, 0)),
            pl.BlockSpec((1, SUBLANES, SSD_CONV_DIM), lambda b, c: (b, 0, 0)),
        ],
        out_shape=[
            jax.ShapeDtypeStruct((t, SSD_D_INNER), BF16),
            jax.ShapeDtypeStruct((bt, SSD_N_HEADS, SSD_D_STATE, SSD_HEAD_DIM), F32),
            jax.ShapeDtypeStruct((bt, SUBLANES, SSD_CONV_DIM), F32),
        ],
        scratch_shapes=[
            pltpu.VMEM((L + SUBLANES, SSD_CONV_DIM), F32),
            pltpu.VMEM((SSD_N_HEADS, SSD_D_STATE, SSD_HEAD_DIM), F32),
            pltpu.VMEM((L, SSD_D_INNER), F32),
        ],
        compiler_params=_cparams("arbitrary", "arbitrary"),
        name="ssd_scan",
    )(x2d, h_main, h_main, h_main, hist8, st0,
      p["w_dt"], p["w_dtT"], p["conv_w"], p["conv_b"], p["dt_b"], p["dt_bT"], p["a_log"], p["a_logT"],
      p["d_exp"], p["norm_w"])


def _ssd_group_kernel(x_ref, z_ref, xs_ref, bc_ref, hist_ref, st0_ref,
                      wdt_ref, wdtT_ref, convw_ref, convb_ref, dtb_ref, dtbT_ref, alog_ref, alogT_ref,
                      dexp_ref, normw_ref, expand_ref,
                      y_ref, stout_ref, histout_ref,
                      ext_ref, ext_next_ref, xrow_ref, ynat_ref, st_refs, *, L, first, last):
    stride = L // SUBLANES
    gw = SSD_HEADS_PER_GROUP * SSD_HEAD_DIM

    n_cb = SSD_CONV_DIM // LANES
    lanes = lambda cb: slice(cb * LANES, (cb + 1) * LANES)

    def strided_rows(ref, start):
        return jnp.concatenate([ref[cb, pl.ds(start, SUBLANES, stride=stride), :] for cb in range(ref.shape[0])],
                               axis=1)

    if first is not None:
        @pl.when(first)
        def _():
            for cb in range(n_cb):
                ext_ref[cb, 0:SUBLANES, :] = hist_ref[0, :, lanes(cb)]
            for g in range(SSD_N_GROUPS):
                st_refs[g][...] = st0_ref[0, g]

    for cb in range(n_cb):
        src = xs_ref if cb < SSD_D_INNER // LANES else bc_ref
        ext_ref[cb, SUBLANES:SUBLANES + L, :] = src[:, lanes(cb % (SSD_D_INNER // LANES))].astype(F32)
    base = SUBLANES - (SSD_CONV - 1)
    taps = {o: strided_rows(ext_ref, o) for o in range(base, base + stride + SSD_CONV - 1)}
    pieces = []
    for r in range(stride):
        conv = convb_ref[...] + taps[base + r] * convw_ref[0:1, :]
        for j in range(1, SSD_CONV):
            conv = conv + taps[base + r + j] * convw_ref[j:j + 1, :]
        pieces.append(conv * _sigmoid(conv))
    xc = jnp.concatenate(pieces, axis=0)
    for cb in range(n_cb):
        tail = ext_ref[cb, L:L + SUBLANES, :]
        ext_next_ref[cb, 0:SUBLANES, :] = tail
        histout_ref[0, :, lanes(cb)] = tail
    xs_c = xc[:, :SSD_D_INNER]
    b_m = xc[:, SSD_D_INNER:SSD_D_INNER + SSD_GN]
    c_m = xc[:, SSD_D_INNER + SSD_GN:]

    for cb in range(D_MODEL // LANES):
        xrow_ref[cb] = x_ref[:, lanes(cb)]
    xb = jnp.concatenate([strided_rows(xrow_ref, r) for r in range(stride)], axis=0).astype(BF16)
    lane_h = lax.broadcasted_iota(jnp.int32, (1, LANES), 1) < SSD_N_HEADS
    dtr = jnp.dot(xb, wdt_ref[...], preferred_element_type=F32)
    dtrT = lax.dot_general(wdtT_ref[...], xb, (((1,), (1,)), ((), ())),
                           preferred_element_type=F32)

    def softplus(v):
        return jnp.maximum(v, 0.0) + jnp.log1p(jnp.exp(-jnp.abs(v)))

    dt = jnp.where(lane_h, softplus(dtr + dtb_ref[...]), 0.0)
    dtT = softplus(dtrT + dtbT_ref[...])
    adt = dt * (-jnp.exp(alog_ref[...]))
    adtT = dtT * (-jnp.exp(alogT_ref[...]))
    ri = lax.broadcasted_iota(jnp.int32, (L, L), 0)
    ci = lax.broadcasted_iota(jnp.int32, (L, L), 1)
    time_r = stride * (ri % SUBLANES) + ri // SUBLANES
    time_c = stride * (ci % SUBLANES) + ci // SUBLANES
    tri = time_r >= time_c
    acs = jnp.dot(tri.astype(F32), adt, precision=lax.Precision.HIGHEST,
                  preferred_element_type=F32)
    acsT = jnp.dot(adtT, (time_r <= time_c).astype(F32), precision=lax.Precision.HIGHEST,
                   preferred_element_type=F32)
    tot = jnp.sum(adt, axis=0, keepdims=True)

    cols = jnp.concatenate([acs, jnp.exp(acs), jnp.exp(tot - acs) * dt,
                            jnp.broadcast_to(jnp.exp(tot), (SUBLANES, LANES))], axis=0)
    cols = jnp.where(lane_h, cols, 0.0)
    hi = cols.astype(BF16).astype(F32)
    mid = (cols - hi).astype(BF16).astype(F32)
    lo = (cols - hi - mid).astype(BF16).astype(F32)
    packed = (hi + pltpu.roll(mid, SSD_N_HEADS, 1) + pltpu.roll(lo, 2 * SSD_N_HEADS, 1)).astype(BF16)
    wide = jnp.dot(packed, expand_ref[...], preferred_element_type=F32)
    acs_w = wide[0:L]
    eacs_w = wide[L:2 * L]
    w_w = wide[2 * L:3 * L]
    cd_w = wide[3 * L:3 * L + 1]

    for g in range(SSD_N_GROUPS):
        gs = slice(g * gw, (g + 1) * gw)
        bg = b_m[:, g * SSD_D_STATE:(g + 1) * SSD_D_STATE]
        cg_b = c_m[:, g * SSD_D_STATE:(g + 1) * SSD_D_STATE].astype(BF16)
        cb = lax.dot_general(cg_b, bg.astype(BF16), (((1,), (1,)), ((), ())),
                             preferred_element_type=F32)
        if L < LANES:
            bg = jnp.concatenate([bg, jnp.zeros((LANES - L, SSD_D_STATE), F32)], axis=0)
        bgT_b = bg.T[:, :L].astype(BF16)
        xs_g = xs_c[:, gs]
        st = st_refs[g][...]
        yo = jnp.dot(cg_b, st.astype(BF16), preferred_element_type=F32)
        stc = jnp.dot(bgT_b, (xs_g * w_w[:, gs]).astype(BF16), preferred_element_type=F32)
        st_refs[g][...] = st * cd_w[:, gs] + stc
        yds = []
        for r in range(SSD_HEADS_PER_GROUP):
            h = g * SSD_HEADS_PER_GROUP + r
            col = acs_w[:, h * SSD_HEAD_DIM:h * SSD_HEAD_DIM + L]
            decay = jnp.exp(jnp.where(tri, col - acsT[h:h + 1, :], -jnp.inf))
            m = (cb * decay * dtT[h:h + 1, :]).astype(BF16)
            yds.append(jnp.dot(m, xs_g[:, r * SSD_HEAD_DIM:(r + 1) * SSD_HEAD_DIM].astype(BF16),
                               preferred_element_type=F32))
        y_g = jnp.concatenate(yds, axis=1) + yo * eacs_w[:, gs] + dexp_ref[:, gs] * xs_g
        for r in range(stride):
            for cb in range(gw // LANES):
                ynat_ref[g * (gw // LANES) + cb, pl.ds(r, SUBLANES, stride=stride), :] = (
                    y_g[r * SUBLANES:(r + 1) * SUBLANES, lanes(cb)])

    z = z_ref[...].astype(F32)
    y = jnp.concatenate([ynat_ref[cb] for cb in range(SSD_D_INNER // LANES)], axis=1) * (z * _sigmoid(z))
    for g in range(SSD_N_GROUPS):
        sl = slice(g * SSD_NORM_GROUP, (g + 1) * SSD_NORM_GROUP)
        blk = y[:, sl]
        ms = jnp.mean(blk * blk, axis=-1, keepdims=True)
        y_ref[:, sl] = (blk * lax.rsqrt(ms + LN_EPS) * normw_ref[:, sl]).astype(y_ref.dtype)

    if last is not None:
        @pl.when(last)
        def _():
            for g in range(SSD_N_GROUPS):
                stout_ref[0, g] = st_refs[g][...]


def _ssd_step_kernel(x_ref, z_ref, xs_ref, bc_ref, hist_ref, st0_ref,
                     wdt_ref, wdtT_ref, convw_ref, convb_ref, dtb_ref, dtbT_ref, alog_ref, alogT_ref,
                     dexp_ref, normw_ref, expand_ref,
                     y_ref, stout_ref, histout_ref,
                     ext_ref, xrow_ref, ynat_ref, *st_refs, L, nc, cps):
    c = pl.program_id(1)
    for sub in range(cps):
        rows = pl.ds(sub * L, L)
        _ssd_group_kernel(
            x_ref.at[rows], z_ref.at[rows], xs_ref.at[rows], bc_ref.at[rows], hist_ref, st0_ref,
            wdt_ref, wdtT_ref, convw_ref, convb_ref, dtb_ref, dtbT_ref, alog_ref, alogT_ref,
            dexp_ref, normw_ref, expand_ref,
            y_ref.at[rows], stout_ref, histout_ref,
            ext_ref.at[sub], ext_ref.at[(sub + 1) % cps], xrow_ref.at[sub], ynat_ref.at[sub], st_refs,
            L=L, first=(c == 0) if sub == 0 else None, last=(c == nc - 1) if sub == cps - 1 else None)


def _ssd_grouped(x2d, h_main, hist8, st0, p, bt, s, L, cps):
    assert L % SUBLANES == 0 and L <= SSD_HEAD_DIM and s % (L * cps) == 0
    nc = s // (L * cps)
    L_blk = L * cps
    t = bt * s
    w2 = 2 * D_MODEL
    gw = SSD_HEADS_PER_GROUP * SSD_HEAD_DIM
    row = lambda b, c: (b * nc + c, 0)
    const2 = lambda b, c: (0, 0)
    state_spec = pl.BlockSpec((1, SSD_N_GROUPS, SSD_D_STATE, gw), lambda b, c: (b, 0, 0, 0))
    kern = functools.partial(_ssd_step_kernel, L=L, nc=nc, cps=cps)
    return pl.pallas_call(
        kern,
        grid=(bt, nc),
        in_specs=[
            pl.BlockSpec((L_blk, D_MODEL), row),
            pl.BlockSpec((L_blk, w2), lambda b, c: (b * nc + c, COL_Z)),
            pl.BlockSpec((L_blk, w2), lambda b, c: (b * nc + c, COL_XS)),
            pl.BlockSpec((L_blk, w2), lambda b, c: (b * nc + c, COL_BC)),
            pl.BlockSpec((1, SUBLANES, SSD_CONV_DIM), lambda b, c: (b, 0, 0)),
            state_spec,
            pl.BlockSpec((D_MODEL, LANES), const2),
            pl.BlockSpec((SSD_N_HEADS, D_MODEL), const2),
            pl.BlockSpec((SSD_CONV, SSD_CONV_DIM), const2),
            pl.BlockSpec((1, SSD_CONV_DIM), const2),
            pl.BlockSpec((1, LANES), const2),
            pl.BlockSpec((SSD_N_HEADS, 1), const2),
            pl.BlockSpec((1, LANES), const2),
            pl.BlockSpec((SSD_N_HEADS, 1), const2),
            pl.BlockSpec((1, SSD_D_INNER), const2),
            pl.BlockSpec((1, SSD_D_INNER), const2),
            pl.BlockSpec((LANES, SSD_D_INNER), const2),
        ],
        out_specs=[
            pl.BlockSpec((L_blk, SSD_D_INNER), row),
            state_spec,
            pl.BlockSpec((1, SUBLANES, SSD_CONV_DIM), lambda b, c: (b, 0, 0)),
        ],
        out_shape=[
            jax.ShapeDtypeStruct((t, SSD_D_INNER), BF16),
            jax.ShapeDtypeStruct((bt, SSD_N_GROUPS, SSD_D_STATE, gw), F32),
            jax.ShapeDtypeStruct((bt, SUBLANES, SSD_CONV_DIM), F32),
        ],
        scratch_shapes=[
            pltpu.VMEM((cps, SSD_CONV_DIM // LANES, L + SUBLANES, LANES), F32),
            pltpu.VMEM((cps, D_MODEL // LANES, L, LANES), F32),
            pltpu.VMEM((cps, SSD_D_INNER // LANES, L, LANES), F32),
        ] + [pltpu.VMEM((SSD_D_STATE, gw), F32) for _ in range(SSD_N_GROUPS)],
        compiler_params=_cparams("arbitrary", "arbitrary"),
        name="ssd_scan",
    )(x2d, h_main, h_main, h_main, hist8, st0,
      p["w_dt"], p["w_dtT"], p["conv_w"], p["conv_b"], p["dt_b128"], p["dt_bT"], p["a_log128"], p["a_logT"],
      p["d_exp"], p["norm_w"], p["expand"])


def _head_expand_matrix():
    e = np.zeros((LANES, SSD_D_INNER), np.float32)
    for part in range(3):
        for h in range(SSD_N_HEADS):
            e[part * SSD_N_HEADS + h, h * SSD_HEAD_DIM:(h + 1) * SSD_HEAD_DIM] = 1.0
    return e


def _state_to_kernel(state):
    b = state.shape[0]
    s5 = state.reshape(b, SSD_N_GROUPS, SSD_HEADS_PER_GROUP, SSD_HEAD_DIM, SSD_D_STATE)
    return jnp.transpose(s5, (0, 1, 4, 2, 3)).reshape(b, SSD_N_GROUPS, SSD_D_STATE,
                                                       SSD_HEADS_PER_GROUP * SSD_HEAD_DIM)


def _state_from_kernel(st):
    b = st.shape[0]
    s5 = st.reshape(b, SSD_N_GROUPS, SSD_D_STATE, SSD_HEADS_PER_GROUP, SSD_HEAD_DIM)
    return jnp.transpose(s5, (0, 1, 3, 4, 2)).reshape(b, SSD_N_HEADS, SSD_HEAD_DIM, SSD_D_STATE)


def _attn_kernel(q_ref, k_ref, v_ref, bias_ref, sink_ref, o_ref, *, cq, n_sub, mask_prefix):
    step = pl.program_id(1)
    kw = WINDOW + cq
    scale = ATT_HEAD_DIM ** -0.5
    gw = ATT_GROUP * ATT_HEAD_DIM
    pairs = [(sub, hk) for sub in range(n_sub) for hk in range(ATT_KV_HEADS)]
    scs, vhs = {}, {}
    for sub in range(n_sub):
        r0 = pl.multiple_of((step * n_sub + sub) * cq, cq)
        qq = q_ref[sub * cq:(sub + 1) * cq, :]
        if mask_prefix:
            valid = (r0 + lax.broadcasted_iota(jnp.int32, (1, kw), 1)) >= WINDOW
        for hk in range(ATT_KV_HEADS):
            q4 = jnp.concatenate(
                [qq[:, hk * gw + g * ATT_HEAD_DIM:hk * gw + (g + 1) * ATT_HEAD_DIM] for g in range(ATT_GROUP)],
                axis=0)
            kh = k_ref[0, hk, pl.ds(r0, kw), :]
            sc = lax.dot_general(q4, kh, (((1,), (1,)), ((), ())), preferred_element_type=F32)
            sc = sc * scale + bias_ref[hk]
            if mask_prefix:
                sc = jnp.where(valid, sc, -1e30)
            scs[sub, hk] = sc
            vhs[sub, hk] = v_ref[0, hk, pl.ds(r0, kw), :]
    es, rs = {}, {}
    for sub, hk in pairs:
        sink = sink_ref[hk]
        mx = jnp.maximum(jnp.max(scs[sub, hk], axis=-1, keepdims=True), sink)
        e = jnp.exp(scs[sub, hk] - mx)
        den = jnp.sum(e, axis=-1, keepdims=True) + jnp.exp(sink - mx)
        es[sub, hk] = e.astype(BF16)
        rs[sub, hk] = 1.0 / den
    for sub, hk in pairs:
        o4 = jnp.dot(es[sub, hk], vhs[sub, hk], preferred_element_type=F32) * rs[sub, hk]
        o_ref[sub * cq:(sub + 1) * cq, hk * gw:(hk + 1) * gw] = jnp.concatenate(
            [o4[g * cq:(g + 1) * cq, :] for g in range(ATT_GROUP)], axis=1).astype(o_ref.dtype)


def _attention(h_main, kall, vall, bias, sinks, bt, s, cq, n_sub, mask_prefix):
    tq = cq * n_sub
    nq = s // tq
    kw = WINDOW + cq
    bias4 = bias.reshape(ATT_KV_HEADS, ATT_GROUP * cq, kw)
    sink4 = jnp.repeat(sinks, cq).reshape(ATT_KV_HEADS, ATT_GROUP * cq, 1)
    per_head = lambda a: jnp.swapaxes(a.reshape(bt, WINDOW + s, ATT_KV_HEADS, ATT_HEAD_DIM), 1, 2)
    kern = functools.partial(_attn_kernel, cq=cq, n_sub=n_sub, mask_prefix=mask_prefix)
    return pl.pallas_call(
        kern,
        grid=(bt, nq),
        in_specs=[
            pl.BlockSpec((tq, D_MODEL), lambda b, i: (b * nq + i, COL_Q)),
            pl.BlockSpec((1, ATT_KV_HEADS, WINDOW + s, ATT_HEAD_DIM), lambda b, i: (b, 0, 0, 0)),
            pl.BlockSpec((1, ATT_KV_HEADS, WINDOW + s, ATT_HEAD_DIM), lambda b, i: (b, 0, 0, 0)),
            pl.BlockSpec((ATT_KV_HEADS, ATT_GROUP * cq, kw), lambda b, i: (0, 0, 0)),
            pl.BlockSpec((ATT_KV_HEADS, ATT_GROUP * cq, 1), lambda b, i: (0, 0, 0)),
        ],
        out_specs=pl.BlockSpec((tq, D_MODEL), lambda b, i: (b * nq + i, 0)),
        out_shape=jax.ShapeDtypeStruct((bt * s, D_MODEL), BF16),
        compiler_params=_cparams("arbitrary", "arbitrary"),
        name="swa_attention",
    )(h_main, per_head(kall), per_head(vall), bias4, sink4)


def _merge_kernel(x_ref, ssd_ref, att_ref, scb_ref, scc_ref, sch_ref, gssd_ref, gatt_ref, gsc_ref, hist_ref,
                  wssd_ref, watt_ref, wsc_ref, wo_ref, scw_ref, lng_ref, lnb_ref, rw_ref, rb_ref,
                  x1r_ref, ir_ref, gate_ref, cnt_ref, histout_ref, stat_ref,
                  ext_ref, base_ref, *, tm, tiles_per_batch):
    i = pl.program_id(0)

    @pl.when(i == 0)
    def _():
        base_ref[...] = jnp.zeros_like(base_ref)

    @pl.when(i % tiles_per_batch == 0)
    def _():
        ext_ref[0:SUBLANES, :] = hist_ref[0]

    u = scc_ref[...].astype(F32) * sch_ref[...].astype(F32)
    ext_ref[SUBLANES:SUBLANES + tm, :] = u
    off = SUBLANES - (SC_WIDTH - 1)
    conv = ext_ref[off:off + tm, :] * scw_ref[0:1, :]
    for j in range(1, SC_WIDTH):
        conv = conv + ext_ref[off + j:off + j + tm, :] * scw_ref[j:j + 1, :]
    tail = ext_ref[tm:tm + SUBLANES, :]
    ext_ref[0:SUBLANES, :] = tail
    histout_ref[0] = tail
    sc_pre = (scb_ref[...].astype(F32) * conv).astype(BF16)

    y_sc = jnp.dot(sc_pre, wsc_ref[...], preferred_element_type=F32)
    y_ssd = jnp.dot(ssd_ref[...], wssd_ref[...], preferred_element_type=F32)
    y_att = jnp.dot(att_ref[...], watt_ref[...], preferred_element_type=F32)
    merged = (_sigmoid(gssd_ref[...].astype(F32)) * y_ssd
              + _sigmoid(gatt_ref[...].astype(F32)) * y_att
              + _sigmoid(gsc_ref[...].astype(F32)) * y_sc)
    v = ALPHA * x_ref[...] + jnp.dot(merged.astype(BF16), wo_ref[...], preferred_element_type=F32)
    mu = jnp.mean(v, axis=-1, keepdims=True)
    vc = v - mu
    var = jnp.mean(vc * vc, axis=-1, keepdims=True)
    x1 = vc * lax.rsqrt(var + LN_EPS) * lng_ref[...] + lnb_ref[...]
    for j in range(ROW_TILES):
        x1r_ref[pl.ds(j, tm, stride=ROW_TILES), :] = x1[:, j * LANES:(j + 1) * LANES]

    rt = min(tm, ROUTE_TILE)
    lane = lax.broadcasted_iota(jnp.int32, (rt, LANES), 1)
    lane1 = lax.broadcasted_iota(jnp.int32, (1, LANES), 1)
    row8 = lax.broadcasted_iota(jnp.int32, (SUBLANES, LANES), 0)
    ri = lax.broadcasted_iota(jnp.int32, (rt, rt), 0)
    ci = lax.broadcasted_iota(jnp.int32, (rt, rt), 1)
    ltri = (ri > ci).astype(BF16)
    gi = lax.broadcasted_iota(jnp.int32, (LANES, LANES), 0)
    gj = lax.broadcasted_iota(jnp.int32, (LANES, LANES), 1)
    earlier_expert = jnp.logical_and(gi < gj, gi // N_EXPERTS == gj // N_EXPERTS).astype(BF16)
    logits_all = jnp.dot(x1.astype(BF16), rw_ref[...], preferred_element_type=F32) + rb_ref[...]
    for sub in range(tm // rt):
        rows = slice(sub * rt, (sub + 1) * rt)
        work = jnp.where(lane < N_EXPERTS, logits_all[rows], -jnp.inf)
        vals, idxs = [], []
        onehot = jnp.zeros((rt, LANES), F32)
        for k in range(TOP_K):
            mv = jnp.max(work, axis=-1, keepdims=True)
            mi = jnp.min(jnp.where(work == mv, lane, LANES), axis=-1, keepdims=True)
            vals.append(mv)
            idxs.append(mi)
            work = jnp.where(lane == mi, -jnp.inf, work)
            onehot = onehot + (lane == mi + k * N_EXPERTS).astype(F32)
        es = [jnp.exp(vk - vals[0]) for vk in vals]
        den = es[0] + es[1] + es[2] + es[3]
        prefix = jnp.dot(ltri, onehot.astype(BF16), preferred_element_type=F32)
        cnt = jnp.sum(onehot, axis=0, keepdims=True)
        base = base_ref[0:1, :]
        within = jnp.zeros((1, LANES), F32)
        tot = cnt
        for sh in range(1, TOP_K):
            rolled = pltpu.roll(cnt, sh * N_EXPERTS, 1)
            within = within + jnp.where(lane1 >= sh * N_EXPERTS, rolled, 0.0)
            tot = tot + rolled
        chunks = jnp.floor((tot + (SEG_ALIGN - 1)) * (1.0 / SEG_ALIGN))
        seg_start = SEG_ALIGN * jnp.dot(jnp.broadcast_to(chunks, (SUBLANES, LANES)).astype(BF16), earlier_expert,
                                        preferred_element_type=F32)[0:1, :]
        rank_all = prefix + (within + base)
        loc_all = prefix + (within + seg_start)
        ir = jnp.zeros((rt, LANES), jnp.int32)
        gt = jnp.zeros((rt, LANES), F32)
        for k in range(TOP_K):
            sel = lane == idxs[k] + k * N_EXPERTS
            rk = jnp.sum(jnp.where(sel, rank_all, 0.0), axis=-1, keepdims=True).astype(jnp.int32)
            lc = jnp.sum(jnp.where(sel, loc_all, 0.0), axis=-1, keepdims=True).astype(jnp.int32)
            ir = jnp.where(lane == k, idxs[k], ir)
            ir = jnp.where(lane == TOP_K + k, rk, ir)
            ir = jnp.where(lane == 2 * TOP_K + k, lc, ir)
            gt = jnp.where(lane == k, es[k] / den, gt)
        ir_ref[rows, :] = ir
        gate_ref[rows, :] = gt
        stat_ref[sub * SUBLANES:(sub + 1) * SUBLANES, :] = jnp.where(
            row8 == 0, tot, jnp.where(row8 == 1, seg_start, jnp.where(row8 == 2, base, 0.0)))
        base_ref[...] = jnp.broadcast_to(base + tot, base_ref.shape)
    cnt_ref[...] = base_ref[...]


def _merge(x2d, h_main, ssd_pre, att, schist8, p, bt, s, tm):
    t = bt * s
    nt = t // tm
    tiles_per_batch = s // tm
    row = lambda i: (i, 0)
    const2 = lambda i: (0, 0)
    hcol = lambda cidx: pl.BlockSpec((tm, D_MODEL), lambda i: (i, cidx))
    kern = functools.partial(_merge_kernel, tm=tm, tiles_per_batch=tiles_per_batch)
    return pl.pallas_call(
        kern,
        grid=(nt,),
        in_specs=[
            pl.BlockSpec((tm, D_MODEL), row),
            pl.BlockSpec((tm, SSD_D_INNER), row),
            pl.BlockSpec((tm, D_MODEL), row),
            hcol(COL_SCB), hcol(COL_SCC), hcol(COL_SCH), hcol(COL_GSSD), hcol(COL_GATT), hcol(COL_GSC),
            pl.BlockSpec((1, SUBLANES, D_MODEL), lambda i: (i // tiles_per_batch, 0, 0)),
            pl.BlockSpec((SSD_D_INNER, D_MODEL), const2, pipeline_mode=pl.Buffered(1)),
            pl.BlockSpec((D_MODEL, D_MODEL), const2, pipeline_mode=pl.Buffered(1)),
            pl.BlockSpec((D_MODEL, D_MODEL), const2, pipeline_mode=pl.Buffered(1)),
            pl.BlockSpec((D_MODEL, D_MODEL), const2, pipeline_mode=pl.Buffered(1)),
            pl.BlockSpec((SC_WIDTH, D_MODEL), const2),
            pl.BlockSpec((1, D_MODEL), const2),
            pl.BlockSpec((1, D_MODEL), const2),
            pl.BlockSpec((D_MODEL, LANES), const2),
            pl.BlockSpec((1, LANES), const2),
        ],
        out_specs=[
            pl.BlockSpec((tm * ROW_TILES, LANES), row),
            pl.BlockSpec((tm, LANES), row),
            pl.BlockSpec((tm, LANES), row),
            pl.BlockSpec((SUBLANES, LANES), const2),
            pl.BlockSpec((1, SUBLANES, D_MODEL), lambda i: (i // tiles_per_batch, 0, 0)),
            pl.BlockSpec((tm // min(tm, ROUTE_TILE) * SUBLANES, LANES), row),
        ],
        out_shape=[
            jax.ShapeDtypeStruct((t * ROW_TILES, LANES), F32),
            jax.ShapeDtypeStruct((t, LANES), jnp.int32),
            jax.ShapeDtypeStruct((t, LANES), F32),
            jax.ShapeDtypeStruct((SUBLANES, LANES), F32),
            jax.ShapeDtypeStruct((bt, SUBLANES, D_MODEL), F32),
            jax.ShapeDtypeStruct((t // min(tm, ROUTE_TILE) * SUBLANES, LANES), F32),
        ],
        scratch_shapes=[
            pltpu.VMEM((tm + SUBLANES, D_MODEL), F32),
            pltpu.VMEM((SUBLANES, LANES), F32),
        ],
        compiler_params=_cparams("arbitrary"),
        name="merge_ln_router",
    )(x2d, ssd_pre, att, h_main, h_main, h_main, h_main, h_main, h_main, schist8,
      p["w_ssd_out"], p["w_att_out"], p["w_sc_out"], p["w_o"], p["sc_w"], p["ln1_g"], p["ln1_b"],
      p["router_w"], p["router_b"])


def _row_copy_wait(src_hbm, dst, sem, n_rows):
    pltpu.make_async_copy(src_hbm.at[pl.ds(0, n_rows * ROW_TILES), :],
                          dst.at[pl.ds(0, n_rows * ROW_TILES), :], sem).wait()


def _dispatch_kernel(pstart_ref, pad_ref, ir_ref, x_ref, xs_hbm, zero_ref, sem, zsem, *, tm):
    @pl.when(pl.program_id(0) == 0)
    def _():
        zero_ref[...] = jnp.zeros_like(zero_ref)

        def zero_copy(row, n_rows):
            dst = pl.multiple_of(row * ROW_TILES, ROW_TILES)
            return pltpu.make_async_copy(zero_ref.at[pl.ds(0, n_rows * ROW_TILES), :],
                                         xs_hbm.at[pl.ds(dst, n_rows * ROW_TILES), :], zsem)

        def per_expert(start):
            def body(e, carry):
                first = pad_ref[e]
                n = pad_ref[N_EXPERTS + 1 + e]
                n_big = n // ZERO_ROWS

                def big(j, c):
                    cp = zero_copy(first + j * ZERO_ROWS, ZERO_ROWS)
                    cp.start() if start else cp.wait()
                    return c

                def small(j, c):
                    cp = zero_copy(first + j, 1)
                    cp.start() if start else cp.wait()
                    return c

                lax.fori_loop(0, n_big, big, 0)
                lax.fori_loop(n_big * ZERO_ROWS, n, small, 0)
                return carry

            lax.fori_loop(0, N_EXPERTS + 1, body, 0)

        per_expert(True)
        per_expert(False)

    def issue(tk, carry):
        src = pl.multiple_of(tk * ROW_TILES, ROW_TILES)
        for k in range(TOP_K):
            dest = pstart_ref[ir_ref[tk * SUBLANES + k]] + ir_ref[tk * SUBLANES + TOP_K + k]
            dst = pl.multiple_of(dest * ROW_TILES, ROW_TILES)
            pltpu.make_async_copy(x_ref.at[pl.ds(src, ROW_TILES), :],
                                  xs_hbm.at[pl.ds(dst, ROW_TILES), :], sem).start()
        return carry

    lax.fori_loop(0, tm, issue, 0)
    for _ in range(TOP_K):
        _row_copy_wait(x_ref, xs_hbm, sem, tm)


def _dispatch(x1r, ir_flat, pstart, pad_info, rows, t, tm):
    return pl.pallas_call(
        functools.partial(_dispatch_kernel, tm=tm),
        grid_spec=pltpu.PrefetchScalarGridSpec(
            num_scalar_prefetch=2,
            grid=(t // tm,),
            in_specs=[
                pl.BlockSpec((tm * SUBLANES,), lambda i, ps, pd: (i,), memory_space=pltpu.SMEM),
                pl.BlockSpec((tm * ROW_TILES, LANES), lambda i, ps, pd: (i, 0)),
            ],
            out_specs=pl.BlockSpec(memory_space=pl.ANY),
            scratch_shapes=[pltpu.VMEM((ZERO_ROWS * ROW_TILES, LANES), F32),
                            pltpu.SemaphoreType.DMA, pltpu.SemaphoreType.DMA],
        ),
        out_shape=jax.ShapeDtypeStruct((rows * ROW_TILES, LANES), F32),
        compiler_params=_cparams("arbitrary"),
        name="moe_dispatch",
    )(pstart, pad_info, ir_flat, x1r)


def _moe_kernel(be_ref, nv_ref, x_ref, wu_ref, bu_ref, wd_ref, bd_ref, o_ref, wub_ref, wdb_ref, *, tmo):
    i = pl.program_id(0)
    valid = i < nv_ref[0]
    new_expert = jnp.logical_or(i == 0, be_ref[i] != be_ref[jnp.maximum(i - 1, 0)])

    @pl.when(jnp.logical_and(valid, new_expert))
    def _():
        wub_ref[...] = wu_ref[0, 0].astype(BF16)
        wdb_ref[...] = wd_ref[0, 0].astype(BF16)

    @pl.when(valid)
    def _():
        x = jnp.concatenate([x_ref[pl.ds(j, tmo, stride=ROW_TILES), :] for j in range(ROW_TILES)], axis=1)
        h = jnp.dot(x.astype(BF16), wub_ref[...], preferred_element_type=F32) + bu_ref[0, 0]
        gate = jnp.minimum(h[:, :D_FF], SWIGLU_LIMIT)
        up = jnp.clip(h[:, D_FF:], -SWIGLU_LIMIT, SWIGLU_LIMIT)
        act = (up + 1.0) * gate * _sigmoid(SWIGLU_ALPHA * gate)
        o = jnp.dot(act.astype(BF16), wdb_ref[...], preferred_element_type=F32) + bd_ref[0, 0]
        for j in range(ROW_TILES):
            o_ref[pl.ds(j, tmo, stride=ROW_TILES), :] = o[:, j * LANES:(j + 1) * LANES]

    @pl.when(jnp.logical_not(valid))
    def _():
        o_ref[...] = jnp.zeros_like(o_ref)


def _moe_experts(xs, block_e, nvalid, p, n_blocks, tmo):
    l = p["layer"]
    last_valid = lambda i, nv: jnp.minimum(i, nv[0] - 1)
    return pl.pallas_call(
        functools.partial(_moe_kernel, tmo=tmo),
        grid_spec=pltpu.PrefetchScalarGridSpec(
            num_scalar_prefetch=2,
            grid=(n_blocks,),
            in_specs=[
                pl.BlockSpec((tmo * ROW_TILES, LANES), lambda i, be, nv: (last_valid(i, nv), 0)),
                pl.BlockSpec((1, 1, D_MODEL, 2 * D_FF), lambda i, be, nv: (l, be[i], 0, 0)),
                pl.BlockSpec((1, 1, 1, 2 * D_FF), lambda i, be, nv: (l, be[i], 0, 0)),
                pl.BlockSpec((1, 1, D_FF, D_MODEL), lambda i, be, nv: (l, be[i], 0, 0)),
                pl.BlockSpec((1, 1, 1, D_MODEL), lambda i, be, nv: (l, be[i], 0, 0)),
            ],
            out_specs=pl.BlockSpec((tmo * ROW_TILES, LANES), lambda i, be, nv: (i, 0)),
            scratch_shapes=[pltpu.VMEM((D_MODEL, 2 * D_FF), BF16), pltpu.VMEM((D_FF, D_MODEL), BF16)],
        ),
        out_shape=jax.ShapeDtypeStruct(xs.shape, F32),
        compiler_params=_cparams("arbitrary"),
        name="moe_experts",
    )(block_e, nvalid, xs, p["w_up"], p["b_up"], p["w_down"], p["b_down"])


def _combine_kernel(pstart_ref, ir_ref, irn_ref, x1r_ref, gate_ref, lng_ref, lnb_ref, eo_hbm, o_ref, buf_ref, sem,
                    *, tm, nt):
    i = pl.program_id(0)
    slot = i % 2

    def issue(idx_ref, s):
        def body(tk, carry):
            dst = pl.multiple_of(tk * ROW_TILES, ROW_TILES)
            for k in range(TOP_K):
                dest = pstart_ref[idx_ref[tk * SUBLANES + k]] + idx_ref[tk * SUBLANES + TOP_K + k]
                src = pl.multiple_of(dest * ROW_TILES, ROW_TILES)
                pltpu.make_async_copy(eo_hbm.at[pl.ds(src, ROW_TILES), :],
                                      buf_ref.at[s, k, pl.ds(dst, ROW_TILES), :], sem.at[s]).start()
            return carry

        lax.fori_loop(0, tm, body, 0)

    @pl.when(i == 0)
    def _():
        issue(ir_ref, 0)

    @pl.when(i + 1 < nt)
    def _():
        issue(irn_ref, 1 - slot)

    for k in range(TOP_K):
        _row_copy_wait(eo_hbm, buf_ref.at[slot, k], sem.at[slot], tm)

    gates = gate_ref[...]
    vs = []
    tot = jnp.zeros((tm, 1), F32)
    for j in range(ROW_TILES):
        acc = ALPHA * x1r_ref[pl.ds(j, tm, stride=ROW_TILES), :]
        for k in range(TOP_K):
            acc = acc + buf_ref[slot, k, pl.ds(j, tm, stride=ROW_TILES), :] * gates[:, k:k + 1]
        vs.append(acc)
        tot = tot + jnp.sum(acc, axis=-1, keepdims=True)
    mu = tot * (1.0 / D_MODEL)
    sq = jnp.zeros((tm, 1), F32)
    for j in range(ROW_TILES):
        vs[j] = vs[j] - mu
        sq = sq + jnp.sum(vs[j] * vs[j], axis=-1, keepdims=True)
    inv = lax.rsqrt(sq * (1.0 / D_MODEL) + LN_EPS)
    for j in range(ROW_TILES):
        sl = slice(j * LANES, (j + 1) * LANES)
        o_ref[:, sl] = vs[j] * inv * lng_ref[:, sl] + lnb_ref[:, sl]


def _combine(x1r, ir_flat, gates, pstart, eo, p, t, tm):
    nt = t // tm
    return pl.pallas_call(
        functools.partial(_combine_kernel, tm=tm, nt=nt),
        grid_spec=pltpu.PrefetchScalarGridSpec(
            num_scalar_prefetch=1,
            grid=(nt,),
            in_specs=[
                pl.BlockSpec((tm * SUBLANES,), lambda i, ps: (i,), memory_space=pltpu.SMEM),
                pl.BlockSpec((tm * SUBLANES,), lambda i, ps: (jnp.minimum(i + 1, nt - 1),), memory_space=pltpu.SMEM),
                pl.BlockSpec((tm * ROW_TILES, LANES), lambda i, ps: (i, 0)),
                pl.BlockSpec((tm, LANES), lambda i, ps: (i, 0)),
                pl.BlockSpec((1, D_MODEL), lambda i, ps: (0, 0)),
                pl.BlockSpec((1, D_MODEL), lambda i, ps: (0, 0)),
                pl.BlockSpec(memory_space=pl.ANY),
            ],
            out_specs=pl.BlockSpec((tm, D_MODEL), lambda i, ps: (i, 0)),
            scratch_shapes=[pltpu.VMEM((2, TOP_K, tm * ROW_TILES, LANES), F32), pltpu.SemaphoreType.DMA((2,))],
        ),
        out_shape=jax.ShapeDtypeStruct((t, D_MODEL), F32),
        compiler_params=_cparams("arbitrary"),
        name="moe_combine_ln",
    )(pstart, ir_flat, ir_flat, x1r, gates, p["ln2_g"], p["ln2_b"], eo)


def _combine_seg_kernel(seg_ref, segn_ref, ir_ref, x1r_ref, gate_ref, lng_ref, lnb_ref, eo_hbm, o_ref,
                        stage_ref, sem, *, rt, nt):
    i = pl.program_id(0)
    slot = i % 2
    sr = stage_ref.shape[1] // ROW_TILES

    @pl.when(i == 0)
    def _():
        stage_ref[...] = jnp.zeros_like(stage_ref)

    def copies(s_ref, s, start):
        def per_expert(e, carry):
            n_copies = (s_ref[e] + (SEG_ALIGN - 1)) // SEG_ALIGN
            dst0 = s_ref[N_EXPERTS + e]
            src0 = s_ref[2 * N_EXPERTS + e]

            def one(j, c):
                src = pl.multiple_of((src0 + j * SEG_ALIGN) * ROW_TILES, ROW_TILES)
                dst = pl.multiple_of((dst0 + j * SEG_ALIGN) * ROW_TILES, SEG_ALIGN * ROW_TILES)
                cp = pltpu.make_async_copy(eo_hbm.at[pl.ds(src, SEG_ALIGN * ROW_TILES), :],
                                           stage_ref.at[s, pl.ds(dst, SEG_ALIGN * ROW_TILES), :], sem.at[s])
                cp.start() if start else cp.wait()
                return c

            lax.fori_loop(0, n_copies, one, 0)
            return carry

        lax.fori_loop(0, N_EXPERTS, per_expert, 0)

    @pl.when(i == 0)
    def _():
        copies(seg_ref, 0, True)

    @pl.when(i + 1 < nt)
    def _():
        copies(segn_ref, 1 - slot, True)

    copies(seg_ref, slot, False)

    rows = jnp.concatenate([stage_ref[slot, pl.ds(j, sr, stride=ROW_TILES), :] for j in range(ROW_TILES)], axis=1)
    rows_b = rows.astype(BF16)
    pos = lax.broadcasted_iota(jnp.int32, (rt, sr), 1)
    ir = ir_ref[...]
    gates = gate_ref[...]
    weight = jnp.zeros((rt, sr), F32)
    for k in range(TOP_K):
        weight = jnp.where(pos == ir[:, 2 * TOP_K + k:2 * TOP_K + k + 1], gates[:, k:k + 1], weight)
    w_hi = weight.astype(BF16)
    w_lo = (weight - w_hi.astype(F32)).astype(BF16)
    moe = jnp.dot(w_hi, rows_b, preferred_element_type=F32) + jnp.dot(w_lo, rows_b, preferred_element_type=F32)
    x1 = jnp.concatenate([x1r_ref[pl.ds(j, rt, stride=ROW_TILES), :] for j in range(ROW_TILES)], axis=1)
    v = ALPHA * x1 + moe
    mu = jnp.mean(v, axis=-1, keepdims=True)
    vc = v - mu
    var = jnp.mean(vc * vc, axis=-1, keepdims=True)
    o_ref[...] = vc * lax.rsqrt(var + LN_EPS) * lng_ref[...] + lnb_ref[...]


def _combine_seg(x1r, ir, gates, seg, eo, p, t, rt):
    nt = t // rt
    sr = rt * TOP_K + N_EXPERTS * SEG_ALIGN
    seg_len = 4 * N_EXPERTS
    return pl.pallas_call(
        functools.partial(_combine_seg_kernel, rt=rt, nt=nt),
        grid=(nt,),
        in_specs=[
            pl.BlockSpec((seg_len,), lambda i: (i,), memory_space=pltpu.SMEM),
            pl.BlockSpec((seg_len,), lambda i: (jnp.minimum(i + 1, nt - 1),), memory_space=pltpu.SMEM),
            pl.BlockSpec((rt, LANES), lambda i: (i, 0)),
            pl.BlockSpec((rt * ROW_TILES, LANES), lambda i: (i, 0)),
            pl.BlockSpec((rt, LANES), lambda i: (i, 0)),
            pl.BlockSpec((1, D_MODEL), lambda i: (0, 0)),
            pl.BlockSpec((1, D_MODEL), lambda i: (0, 0)),
            pl.BlockSpec(memory_space=pl.ANY),
        ],
        out_specs=pl.BlockSpec((rt, D_MODEL), lambda i: (i, 0)),
        out_shape=jax.ShapeDtypeStruct((t, D_MODEL), F32),
        scratch_shapes=[pltpu.VMEM((2, sr * ROW_TILES, LANES), F32), pltpu.SemaphoreType.DMA((2,))],
        compiler_params=_cparams("arbitrary"),
        name="moe_combine_ln",
    )(seg, seg, ir, x1r, gates, p["ln2_g"], p["ln2_b"], eo)


def _moe(x1r, ir, gates, cnt, stat, p, t, tm, tmo, rt):
    n_assign = t * TOP_K
    n_blocks = (n_assign + N_EXPERTS * (tmo - 1) + tmo - 1) // tmo
    rows = n_blocks * tmo
    counts = cnt[0, :N_EXPERTS].astype(jnp.int32)
    padded = (counts + tmo - 1) // tmo * tmo
    pend = jnp.cumsum(padded)
    pstart = (pend - padded).astype(jnp.int32)
    pad_info = jnp.concatenate([pstart + counts, pend[-1:], padded - counts, rows - pend[-1:]]).astype(jnp.int32)
    block_start = jnp.arange(n_blocks, dtype=jnp.int32) * tmo
    block_e = jnp.minimum(jnp.sum((pend[None, :] <= block_start[:, None]).astype(jnp.int32), axis=1),
                          N_EXPERTS - 1)
    nvalid = (pend[-1:] // tmo).astype(jnp.int32)
    ir_flat = ir[:, :SUBLANES].reshape(t * SUBLANES)
    xs = _dispatch(x1r, ir_flat, pstart, pad_info, rows, t, tm)
    eo = _moe_experts(xs, block_e, nvalid, p, n_blocks, tmo)
    st3 = stat.reshape(t // rt, SUBLANES, LANES)[:, :3, :N_EXPERTS].astype(jnp.int32)
    seg = jnp.concatenate([st3[:, 0], st3[:, 1], st3[:, 2] + pstart[None, :], jnp.zeros_like(st3[:, 0])],
                          axis=1).reshape(-1)
    return _combine_seg(x1r, ir, gates, seg, eo, p, t, rt)


def _t5_bucket(rel):
    half = N_BUCKETS // 2
    max_exact = half // 2
    ret = jnp.where(rel > 0, half, 0)
    n = jnp.abs(rel)
    nf = jnp.maximum(n, 1).astype(F32)
    large = max_exact + (jnp.log(nf / max_exact) / math.log(MAX_DISTANCE / max_exact)
                         * (half - max_exact)).astype(jnp.int32)
    large = jnp.minimum(large, half - 1)
    return ret + jnp.where(n < max_exact, n, large)


def _rel_bias(table, cq):
    koff = jnp.arange(WINDOW + cq) - WINDOW
    rel = koff[None, :] - jnp.arange(cq)[:, None]
    return jnp.transpose(table[_t5_bucket(rel)].astype(F32), (2, 0, 1))


def _pad_rows_front(a, rows):
    pad = jnp.zeros(a.shape[:1] + (rows - a.shape[1],) + a.shape[2:], a.dtype)
    return jnp.concatenate([pad, a], axis=1)


def _layer_params(l, w_in, ssd_conv_w, ssd_conv_b, ssd_dt_bias, ssd_a_log, ssd_d, ssd_norm_w, w_ssd_out,
                  attn_sinks, w_attn_out, sc_conv_w, w_sc_out, w_o, ln1_g, ln1_b,
                  router_w, router_b, w_up, b_up, w_down, b_down, ln2_g, ln2_b):
    wi = w_in[l]
    w_dt = wi[:, OFF_DT:OFF_Q]
    return {
        "w_main": jnp.concatenate([wi[:, :OFF_DT], wi[:, OFF_Q:OFF_K], wi[:, OFF_SCB:]], axis=1).astype(BF16),
        "w_kv": wi[:, OFF_K:OFF_SCB].astype(BF16),
        "w_dt": jnp.pad(w_dt, ((0, 0), (0, LANES - SSD_N_HEADS))).astype(BF16),
        "w_dtT": w_dt.T.astype(BF16),
        "conv_w": ssd_conv_w[l],
        "conv_b": ssd_conv_b[l][None, :],
        "dt_b128": jnp.pad(ssd_dt_bias[l], (0, LANES - SSD_N_HEADS))[None, :],
        "dt_bT": ssd_dt_bias[l][:, None],
        "a_log128": jnp.pad(ssd_a_log[l], (0, LANES - SSD_N_HEADS))[None, :],
        "expand": jnp.asarray(_head_expand_matrix(), BF16),
        "a_logT": ssd_a_log[l][:, None],
        "d_exp": jnp.repeat(ssd_d[l], SSD_HEAD_DIM)[None, :],
        "norm_w": ssd_norm_w[l][None, :],
        "w_ssd_out": w_ssd_out[l].astype(BF16),
        "sinks": attn_sinks[l],
        "w_att_out": w_attn_out[l].astype(BF16),
        "sc_w": sc_conv_w[l],
        "w_sc_out": w_sc_out[l].astype(BF16),
        "w_o": w_o[l].astype(BF16),
        "ln1_g": ln1_g[l][None, :],
        "ln1_b": ln1_b[l][None, :],
        "router_w": jnp.pad(router_w[l], ((0, 0), (0, LANES - N_EXPERTS))).astype(BF16),
        "router_b": jnp.pad(router_b[l], (0, LANES - N_EXPERTS))[None, :],
        "layer": l,
        "w_up": w_up,
        "b_up": b_up[:, :, None, :],
        "w_down": w_down,
        "b_down": b_down[:, :, None, :],
        "ln2_g": ln2_g[l][None, :],
        "ln2_b": ln2_b[l][None, :],
    }


def _trunk_layer(x2d, bt, s, p, bias, ssd_hist, ssd_state, sc_hist, kv_cache, cfg):
    t = bt * s
    h_main, h_kv = _in_proj(x2d, p["w_main"], p["w_kv"], cfg["tm_in"], cfg["tn_in"])
    k_new = h_kv[:, :ATT_KV_DIM].reshape(bt, s, ATT_KV_DIM)
    v_new = h_kv[:, ATT_KV_DIM:].reshape(bt, s, ATT_KV_DIM)

    if kv_cache is None:
        zpad = jnp.zeros((bt, WINDOW, ATT_KV_DIM), BF16)
        kall = jnp.concatenate([zpad, k_new.astype(BF16)], axis=1)
        vall = jnp.concatenate([zpad, v_new.astype(BF16)], axis=1)
    else:
        kall = jnp.concatenate([kv_cache[0].astype(BF16), k_new.astype(BF16)], axis=1)
        vall = jnp.concatenate([kv_cache[1].astype(BF16), v_new.astype(BF16)], axis=1)
    att = _attention(h_main, kall, vall, bias, p["sinks"], bt, s, cfg["cq"], cfg["n_sub"], kv_cache is None)

    if ssd_hist is None:
        hist8 = jnp.zeros((bt, SUBLANES, SSD_CONV_DIM), F32)
        st0 = jnp.zeros((bt, SSD_N_GROUPS, SSD_D_STATE, SSD_HEADS_PER_GROUP * SSD_HEAD_DIM), F32)
        schist8 = jnp.zeros((bt, SUBLANES, D_MODEL), F32)
    else:
        hist8 = _pad_rows_front(ssd_hist, SUBLANES)
        st0 = _state_to_kernel(ssd_state)
        schist8 = _pad_rows_front(sc_hist, SUBLANES)
    ssd_pre, st_out, hist_out = _ssd_grouped(x2d, h_main, hist8, st0, p, bt, s, cfg["cq"], cfg["ssd_cps"])

    x1r, ir, gates, cnt, schist_out, stat = _merge(x2d, h_main, ssd_pre, att, schist8, p, bt, s, cfg["tm_merge"])
    x2 = _moe(x1r, ir, gates, cnt, stat, p, t, cfg["tm_moe"], cfg["tmo"], min(cfg["tm_merge"], ROUTE_TILE))

    n_keep = min(s, WINDOW)
    new_k = k_new[:, s - n_keep:].reshape(bt, n_keep, ATT_KV_HEADS, ATT_HEAD_DIM)
    new_v = v_new[:, s - n_keep:].reshape(bt, n_keep, ATT_KV_HEADS, ATT_HEAD_DIM)
    new_state = _state_from_kernel(st_out)
    new_hist = hist_out[:, SUBLANES - (SSD_CONV - 1):]
    new_sc_hist = schist_out[:, SUBLANES - (SC_WIDTH - 1):]
    return x2, new_k, new_v, new_state, new_hist, new_sc_hist


def _config(bt, s, prompt):
    t = bt * s
    if prompt:
        return {"tm_in": min(t, 2048), "tn_in": 1024, "cq": CHUNK, "n_sub": min(8, s // CHUNK),
                "tm_merge": min(s, 512), "tm_moe": min(t, 256), "tmo": min(512, max(64, t // 32)),
                "ssd_cps": 2 if s % (2 * CHUNK) == 0 else 1}
    return {"tm_in": min(t, 256), "tn_in": 1024, "cq": s, "n_sub": 1,
            "tm_merge": s, "tm_moe": min(t, 256), "tmo": 64, "ssd_cps": 1}


def _forward(x_prompt, x_sample, cache_attn_k, cache_attn_v, state_ssd, state_ssd_conv, state_short_conv,
             rel_bias, layer_weights):
    bp, sp, _ = x_prompt.shape
    bd, sd, _ = x_sample.shape
    cfg_p = _config(bp, sp, True)
    cfg_d = _config(bd, sd, False)
    bias_p = _rel_bias(rel_bias, cfg_p["cq"])
    bias_d = _rel_bias(rel_bias, cfg_d["cq"])
    yp = x_prompt.reshape(bp * sp, D_MODEL)
    ys = x_sample.reshape(bd * sd, D_MODEL)
    outs_p, outs_d = [], []
    for l in range(DEPTH):
        p = _layer_params(l, *layer_weights)
        yp, *rest = _trunk_layer(yp, bp, sp, p, bias_p, None, None, None, None, cfg_p)
        outs_p.append(rest)
        cache = (cache_attn_k[l].reshape(bd, WINDOW, ATT_KV_DIM), cache_attn_v[l].reshape(bd, WINDOW, ATT_KV_DIM))
        ys, *rest = _trunk_layer(ys, bd, sd, p, bias_d, state_ssd_conv[l], state_ssd[l], state_short_conv[l],
                                 cache, cfg_d)
        outs_d.append(rest)
    stack = lambda outs, i: jnp.stack([o[i] for o in outs])
    return (yp.reshape(bp, sp, D_MODEL), ys.reshape(bd, sd, D_MODEL),
            stack(outs_p, 0), stack(outs_p, 1), stack(outs_p, 2), stack(outs_p, 3), stack(outs_p, 4),
            stack(outs_d, 0), stack(outs_d, 1), stack(outs_d, 2), stack(outs_d, 3), stack(outs_d, 4))


def kernel(x_prompt, x_sample, cache_attn_k, cache_attn_v, state_ssd, state_ssd_conv, state_short_conv, w_in, ssd_conv_w, ssd_conv_b, ssd_dt_bias, ssd_a_log, ssd_d, ssd_norm_w, w_ssd_out, attn_sinks, w_attn_out, rel_bias, sc_conv_w, w_sc_out, w_o, ln1_g, ln1_b, router_w, router_b, w_up, b_up, w_down, b_down, ln2_g, ln2_b):
    layer_weights = (w_in, ssd_conv_w, ssd_conv_b, ssd_dt_bias, ssd_a_log, ssd_d, ssd_norm_w, w_ssd_out,
                     attn_sinks, w_attn_out, sc_conv_w, w_sc_out, w_o, ln1_g, ln1_b,
                     router_w, router_b, w_up, b_up, w_down, b_down, ln2_g, ln2_b)
    return _forward(x_prompt, x_sample, cache_attn_k, cache_attn_v, state_ssd, state_ssd_conv, state_short_conv,
                    rel_bias, layer_weights)
```

```python
import functools
import math

import jax
import jax.numpy as jnp
import numpy as np
from jax import lax
from jax.experimental import pallas as pl
from jax.experimental.pallas import tpu as pltpu

F32 = jnp.float32
BF16 = jnp.bfloat16

D_MODEL = 1024
DEPTH = 2
CHUNK = 64
SSD_D_INNER = 2 * D_MODEL
SSD_HEAD_DIM = 64
SSD_N_HEADS = SSD_D_INNER // SSD_HEAD_DIM
SSD_N_GROUPS = 8
SSD_D_STATE = 128
SSD_CONV = 4
SSD_GN = SSD_N_GROUPS * SSD_D_STATE
SSD_CONV_DIM = SSD_D_INNER + 2 * SSD_GN
SSD_HEADS_PER_GROUP = SSD_N_HEADS // SSD_N_GROUPS
SSD_NORM_GROUP = SSD_D_INNER // SSD_N_GROUPS
ATT_HEAD_DIM = 64
ATT_HEADS = D_MODEL // ATT_HEAD_DIM
ATT_KV_HEADS = ATT_HEADS // 4
ATT_GROUP = ATT_HEADS // ATT_KV_HEADS
ATT_KV_DIM = ATT_KV_HEADS * ATT_HEAD_DIM
WINDOW = 128
N_BUCKETS = 32
MAX_DISTANCE = 128
SC_WIDTH = 3
N_EXPERTS = 32
TOP_K = 4
D_FF = D_MODEL
SWIGLU_LIMIT = 7.0
SWIGLU_ALPHA = 1.702
LN_EPS = 1e-5
ALPHA = (2.0 * DEPTH) ** 0.25

LANES = 128
SUBLANES = 8
ROW_TILES = D_MODEL // LANES
VMEM_LIMIT = 56 * 1024 * 1024
ZERO_ROWS = 16
ROUTE_TILE = 256
SEG_ALIGN = 8

COL_Z, COL_XS, COL_BC = 0, 1, 2
COL_Q, COL_SCB, COL_SCC, COL_SCH, COL_GSSD, COL_GATT, COL_GSC = 6, 7, 8, 9, 10, 11, 12
N_MAIN = 13 * D_MODEL
OFF_DT = SSD_D_INNER + SSD_CONV_DIM
OFF_Q = OFF_DT + SSD_N_HEADS
OFF_K = OFF_Q + D_MODEL
OFF_SCB = OFF_K + 2 * ATT_KV_DIM


def _sigmoid(v):
    return 0.5 * jnp.tanh(0.5 * v) + 0.5


def _cparams(*sem):
    return pltpu.CompilerParams(dimension_semantics=sem, vmem_limit_bytes=VMEM_LIMIT)


def _in_proj_kernel(x_ref, w_ref, wkv_ref, o_ref, kv_ref, xb_ref, *, n_main):
    j = pl.program_id(1)

    @pl.when(j == 0)
    def _():
        xb_ref[...] = x_ref[...].astype(BF16)

    @pl.when(j < n_main)
    def _():
        o_ref[...] = jnp.dot(xb_ref[...], w_ref[...], preferred_element_type=F32).astype(o_ref.dtype)

    @pl.when(j == n_main)
    def _():
        kv_ref[...] = jnp.dot(xb_ref[...], wkv_ref[...], preferred_element_type=F32)


def _in_proj(x, w, w_kv, tm, tn):
    t, k = x.shape
    n = w.shape[1]
    n_kv = w_kv.shape[1]
    n_main = n // tn
    main_col = lambda i, j: (i, jnp.minimum(j, n_main - 1))
    return pl.pallas_call(
        functools.partial(_in_proj_kernel, n_main=n_main),
        grid=(t // tm, n_main + 1),
        in_specs=[pl.BlockSpec((tm, k), lambda i, j: (i, 0)),
                  pl.BlockSpec((k, tn), lambda i, j: (0, jnp.minimum(j, n_main - 1))),
                  pl.BlockSpec((k, n_kv), lambda i, j: (0, 0))],
        out_specs=[pl.BlockSpec((tm, tn), main_col),
                   pl.BlockSpec((tm, n_kv), lambda i, j: (i, 0))],
        out_shape=[jax.ShapeDtypeStruct((t, n), BF16), jax.ShapeDtypeStruct((t, n_kv), F32)],
        scratch_shapes=[pltpu.VMEM((tm, k), BF16)],
        compiler_params=_cparams("arbitrary", "arbitrary"),
        name="in_proj",
    )(x, w, w_kv)


def _ssd_group_kernel(x_ref, z_ref, xs_ref, bc_ref, hist_ref, st0_ref,
                      wdt_ref, wdtT_ref, convw_ref, convb_ref, dtb_ref, dtbT_ref, alog_ref, alogT_ref,
                      dexp_ref, normw_ref, expand_ref,
                      y_ref, stout_ref, histout_ref,
                      ext_ref, ext_next_ref, xrow_ref, ynat_ref, st_refs, *, L, first, last):
    stride = L // SUBLANES
    gw = SSD_HEADS_PER_GROUP * SSD_HEAD_DIM

    n_cb = SSD_CONV_DIM // LANES
    lanes = lambda cb: slice(cb * LANES, (cb + 1) * LANES)

    def strided_rows(ref, start):
        return jnp.concatenate([ref[cb, pl.ds(start, SUBLANES, stride=stride), :] for cb in range(ref.shape[0])],
                               axis=1)

    if first is not None:
        @pl.when(first)
        def _():
            for cb in range(n_cb):
                ext_ref[cb, 0:SUBLANES, :] = hist_ref[0, :, lanes(cb)]
            for g in range(SSD_N_GROUPS):
                st_refs[g][...] = st0_ref[0, g]

    for cb in range(n_cb):
        src = xs_ref if cb < SSD_D_INNER // LANES else bc_ref
        ext_ref[cb, SUBLANES:SUBLANES + L, :] = src[:, lanes(cb % (SSD_D_INNER // LANES))].astype(F32)
    base = SUBLANES - (SSD_CONV - 1)
    taps = {o: strided_rows(ext_ref, o) for o in range(base, base + stride + SSD_CONV - 1)}
    pieces = []
    for r in range(stride):
        conv = convb_ref[...] + taps[base + r] * convw_ref[0:1, :]
        for j in range(1, SSD_CONV):
            conv = conv + taps[base + r + j] * convw_ref[j:j + 1, :]
        pieces.append(conv * _sigmoid(conv))
    xc = jnp.concatenate(pieces, axis=0)
    for cb in range(n_cb):
        tail = ext_ref[cb, L:L + SUBLANES, :]
        ext_next_ref[cb, 0:SUBLANES, :] = tail
        histout_ref[0, :, lanes(cb)] = tail
    xs_c = xc[:, :SSD_D_INNER]
    b_m = xc[:, SSD_D_INNER:SSD_D_INNER + SSD_GN]
    c_m = xc[:, SSD_D_INNER + SSD_GN:]

    for cb in range(D_MODEL // LANES):
        xrow_ref[cb] = x_ref[:, lanes(cb)]
    xb = jnp.concatenate([strided_rows(xrow_ref, r) for r in range(stride)], axis=0).astype(BF16)
    lane_h = lax.broadcasted_iota(jnp.int32, (1, LANES), 1) < SSD_N_HEADS
    dtr = jnp.dot(xb, wdt_ref[...], preferred_element_type=F32)
    dtrT = lax.dot_general(wdtT_ref[...], xb, (((1,), (1,)), ((), ())),
                           preferred_element_type=F32)

    def softplus(v):
        return jnp.maximum(v, 0.0) + jnp.log1p(jnp.exp(-jnp.abs(v)))

    dt = jnp.where(lane_h, softplus(dtr + dtb_ref[...]), 0.0)
    dtT = softplus(dtrT + dtbT_ref[...])
    adt = dt * (-jnp.exp(alog_ref[...]))
    adtT = dtT * (-jnp.exp(alogT_ref[...]))
    ri = lax.broadcasted_iota(jnp.int32, (L, L), 0)
    ci = lax.broadcasted_iota(jnp.int32, (L, L), 1)
    time_r = stride * (ri % SUBLANES) + ri // SUBLANES
    time_c = stride * (ci % SUBLANES) + ci // SUBLANES
    tri = time_r >= time_c
    acs = jnp.dot(tri.astype(F32), adt, precision=lax.Precision.HIGHEST,
                  preferred_element_type=F32)
    acsT = jnp.dot(adtT, (time_r <= time_c).astype(F32), precision=lax.Precision.HIGHEST,
                   preferred_element_type=F32)
    tot = jnp.sum(adt, axis=0, keepdims=True)
    paired = L == SSD_HEAD_DIM and 2 * L == LANES
    if paired:
        half = SSD_N_HEADS // 2
        acsT2 = jnp.concatenate([acsT[:half], acsT[half:]], axis=1)
        dtT2 = jnp.concatenate([dtT[:half], dtT[half:]], axis=1)
        rj = lax.broadcasted_iota(jnp.int32, (L, LANES), 0)
        cj = lax.broadcasted_iota(jnp.int32, (L, LANES), 1) % L
        tri2 = (stride * (rj % SUBLANES) + rj // SUBLANES) >= (stride * (cj % SUBLANES) + cj // SUBLANES)
        low_half = lax.broadcasted_iota(jnp.int32, (L, LANES), 1) < L

    cols = jnp.concatenate([acs, jnp.exp(acs), jnp.exp(tot - acs) * dt,
                            jnp.broadcast_to(jnp.exp(tot), (SUBLANES, LANES))], axis=0)
    cols = jnp.where(lane_h, cols, 0.0)
    hi = cols.astype(BF16).astype(F32)
    mid = (cols - hi).astype(BF16).astype(F32)
    lo = (cols - hi - mid).astype(BF16).astype(F32)
    packed = (hi + pltpu.roll(mid, SSD_N_HEADS, 1) + pltpu.roll(lo, 2 * SSD_N_HEADS, 1)).astype(BF16)
    wide = jnp.dot(packed, expand_ref[...], preferred_element_type=F32)
    acs_w = wide[0:L]
    eacs_w = wide[L:2 * L]
    w_w = wide[2 * L:3 * L]
    cd_w = wide[3 * L:3 * L + 1]

    for g in range(SSD_N_GROUPS):
        gs = slice(g * gw, (g + 1) * gw)
        bg = b_m[:, g * SSD_D_STATE:(g + 1) * SSD_D_STATE]
        bg_b = bg.astype(BF16)
        cg_b = c_m[:, g * SSD_D_STATE:(g + 1) * SSD_D_STATE].astype(BF16)
        if not paired:
            cb = lax.dot_general(cg_b, bg_b, (((1,), (1,)), ((), ())), preferred_element_type=F32)
        if L < LANES:
            bg = jnp.concatenate([bg, jnp.zeros((LANES - L, SSD_D_STATE), F32)], axis=0)
        bgT_b = bg.T[:, :L].astype(BF16)
        xs_g = xs_c[:, gs]
        st = st_refs[g][...]
        yo = jnp.dot(cg_b, st.astype(BF16), preferred_element_type=F32)
        stc = jnp.dot(bgT_b, (xs_g * w_w[:, gs]).astype(BF16), preferred_element_type=F32)
        st_refs[g][...] = st * cd_w[:, gs] + stc
        yds = []
        if paired:
            bg2_b = jnp.concatenate([bg_b, bg_b], axis=0)
            cb2 = lax.dot_general(cg_b, bg2_b, (((1,), (1,)), ((), ())), preferred_element_type=F32)
            for q in range(SSD_HEADS_PER_GROUP // 2):
                pair = g * (SSD_HEADS_PER_GROUP // 2) + q
                ls = slice(pair * LANES, (pair + 1) * LANES)
                decay = jnp.exp(jnp.where(tri2, acs_w[:, ls] - acsT2[pair:pair + 1, :], -jnp.inf))
                m = (cb2 * decay * dtT2[pair:pair + 1, :]).astype(BF16)
                xs2 = xs_c[:, ls]
                xs_bd = jnp.concatenate([jnp.where(low_half, xs2, 0.0), jnp.where(low_half, 0.0, xs2)],
                                        axis=0).astype(BF16)
                yds.append(jnp.dot(m, xs_bd, preferred_element_type=F32))
        else:
            for r in range(SSD_HEADS_PER_GROUP):
                h = g * SSD_HEADS_PER_GROUP + r
                hr = (h % 2) * (SSD_N_HEADS // 2) + h // 2
                col = acs_w[:, h * SSD_HEAD_DIM:h * SSD_HEAD_DIM + L]
                decay = jnp.exp(jnp.where(tri, col - acsT[hr:hr + 1, :], -jnp.inf))
                m = (cb * decay * dtT[hr:hr + 1, :]).astype(BF16)
                yds.append(jnp.dot(m, xs_g[:, r * SSD_HEAD_DIM:(r + 1) * SSD_HEAD_DIM].astype(BF16),
                                   preferred_element_type=F32))
        y_g = jnp.concatenate(yds, axis=1) + yo * eacs_w[:, gs] + dexp_ref[:, gs] * xs_g
        for r in range(stride):
            for cb in range(gw // LANES):
                ynat_ref[g * (gw // LANES) + cb, pl.ds(r, SUBLANES, stride=stride), :] = (
                    y_g[r * SUBLANES:(r + 1) * SUBLANES, lanes(cb)])

    z = z_ref[...].astype(F32)
    y = jnp.concatenate([ynat_ref[cb] for cb in range(SSD_D_INNER // LANES)], axis=1) * (z * _sigmoid(z))
    for g in range(SSD_N_GROUPS):
        sl = slice(g * SSD_NORM_GROUP, (g + 1) * SSD_NORM_GROUP)
        blk = y[:, sl]
        ms = jnp.mean(blk * blk, axis=-1, keepdims=True)
        y_ref[:, sl] = (blk * lax.rsqrt(ms + LN_EPS) * normw_ref[:, sl]).astype(y_ref.dtype)

    if last is not None:
        @pl.when(last)
        def _():
            for g in range(SSD_N_GROUPS):
                stout_ref[0, g] = st_refs[g][...]


def _ssd_step_kernel(x_ref, z_ref, xs_ref, bc_ref, hist_ref, st0_ref,
                     wdt_ref, wdtT_ref, convw_ref, convb_ref, dtb_ref, dtbT_ref, alog_ref, alogT_ref,
                     dexp_ref, normw_ref, expand_ref,
                     y_ref, stout_ref, histout_ref,
                     ext_ref, xrow_ref, ynat_ref, *st_refs, L, nc, cps):
    c = pl.program_id(1)
    for sub in range(cps):
        rows = pl.ds(sub * L, L)
        _ssd_group_kernel(
            x_ref.at[rows], z_ref.at[rows], xs_ref.at[rows], bc_ref.at[rows], hist_ref, st0_ref,
            wdt_ref, wdtT_ref, convw_ref, convb_ref, dtb_ref, dtbT_ref, alog_ref, alogT_ref,
            dexp_ref, normw_ref, expand_ref,
            y_ref.at[rows], stout_ref, histout_ref,
            ext_ref.at[sub], ext_ref.at[(sub + 1) % cps], xrow_ref.at[sub], ynat_ref.at[sub], st_refs,
            L=L, first=(c == 0) if sub == 0 else None, last=(c == nc - 1) if sub == cps - 1 else None)


def _ssd_grouped(x2d, h_main, hist8, st0, p, bt, s, L, cps):
    assert L % SUBLANES == 0 and L <= SSD_HEAD_DIM and s % (L * cps) == 0
    nc = s // (L * cps)
    L_blk = L * cps
    t = bt * s
    w2 = 2 * D_MODEL
    gw = SSD_HEADS_PER_GROUP * SSD_HEAD_DIM
    row = lambda b, c: (b * nc + c, 0)
    const2 = lambda b, c: (0, 0)
    state_spec = pl.BlockSpec((1, SSD_N_GROUPS, SSD_D_STATE, gw), lambda b, c: (b, 0, 0, 0))
    kern = functools.partial(_ssd_step_kernel, L=L, nc=nc, cps=cps)
    return pl.pallas_call(
        kern,
        grid=(bt, nc),
        in_specs=[
            pl.BlockSpec((L_blk, D_MODEL), row),
            pl.BlockSpec((L_blk, w2), lambda b, c: (b * nc + c, COL_Z)),
            pl.BlockSpec((L_blk, w2), lambda b, c: (b * nc + c, COL_XS)),
            pl.BlockSpec((L_blk, w2), lambda b, c: (b * nc + c, COL_BC)),
            pl.BlockSpec((1, SUBLANES, SSD_CONV_DIM), lambda b, c: (b, 0, 0)),
            state_spec,
            pl.BlockSpec((D_MODEL, LANES), const2),
            pl.BlockSpec((SSD_N_HEADS, D_MODEL), const2),
            pl.BlockSpec((SSD_CONV, SSD_CONV_DIM), const2),
            pl.BlockSpec((1, SSD_CONV_DIM), const2),
            pl.BlockSpec((1, LANES), const2),
            pl.BlockSpec((SSD_N_HEADS, 1), const2),
            pl.BlockSpec((1, LANES), const2),
            pl.BlockSpec((SSD_N_HEADS, 1), const2),
            pl.BlockSpec((1, SSD_D_INNER), const2),
            pl.BlockSpec((1, SSD_D_INNER), const2),
            pl.BlockSpec((LANES, SSD_D_INNER), const2),
        ],
        out_specs=[
            pl.BlockSpec((L_blk, SSD_D_INNER), row),
            state_spec,
            pl.BlockSpec((1, SUBLANES, SSD_CONV_DIM), lambda b, c: (b, 0, 0)),
        ],
        out_shape=[
            jax.ShapeDtypeStruct((t, SSD_D_INNER), BF16),
            jax.ShapeDtypeStruct((bt, SSD_N_GROUPS, SSD_D_STATE, gw), F32),
            jax.ShapeDtypeStruct((bt, SUBLANES, SSD_CONV_DIM), F32),
        ],
        scratch_shapes=[
            pltpu.VMEM((cps, SSD_CONV_DIM // LANES, L + SUBLANES, LANES), F32),
            pltpu.VMEM((cps, D_MODEL // LANES, L, LANES), F32),
            pltpu.VMEM((cps, SSD_D_INNER // LANES, L, LANES), F32),
        ] + [pltpu.VMEM((SSD_D_STATE, gw), F32) for _ in range(SSD_N_GROUPS)],
        compiler_params=_cparams("arbitrary", "arbitrary"),
        name="ssd_scan",
    )(x2d, h_main, h_main, h_main, hist8, st0,
      p["w_dt"], p["w_dtT"], p["conv_w"], p["conv_b"], p["dt_b128"], p["dt_bT"], p["a_log128"], p["a_logT"],
      p["d_exp"], p["norm_w"], p["expand"])


_HEADS_EVEN_ODD = np.concatenate([np.arange(0, SSD_N_HEADS, 2), np.arange(1, SSD_N_HEADS, 2)])


def _head_expand_matrix():
    e = np.zeros((LANES, SSD_D_INNER), np.float32)
    for part in range(3):
        for h in range(SSD_N_HEADS):
            e[part * SSD_N_HEADS + h, h * SSD_HEAD_DIM:(h + 1) * SSD_HEAD_DIM] = 1.0
    return e


def _state_to_kernel(state):
    b = state.shape[0]
    s5 = state.reshape(b, SSD_N_GROUPS, SSD_HEADS_PER_GROUP, SSD_HEAD_DIM, SSD_D_STATE)
    return jnp.transpose(s5, (0, 1, 4, 2, 3)).reshape(b, SSD_N_GROUPS, SSD_D_STATE,
                                                       SSD_HEADS_PER_GROUP * SSD_HEAD_DIM)


def _state_from_kernel(st):
    b = st.shape[0]
    s5 = st.reshape(b, SSD_N_GROUPS, SSD_D_STATE, SSD_HEADS_PER_GROUP, SSD_HEAD_DIM)
    return jnp.transpose(s5, (0, 1, 3, 4, 2)).reshape(b, SSD_N_HEADS, SSD_HEAD_DIM, SSD_D_STATE)


def _attn_kernel(q_ref, k_ref, v_ref, bias_ref, sink_ref, o_ref, *, cq, n_sub, mask_prefix):
    step = pl.program_id(1)
    kw = WINDOW + cq
    scale = ATT_HEAD_DIM ** -0.5
    gw = ATT_GROUP * ATT_HEAD_DIM
    pairs = [(sub, hk) for sub in range(n_sub) for hk in range(ATT_KV_HEADS)]
    scs, vhs = {}, {}
    for sub in range(n_sub):
        r0 = pl.multiple_of((step * n_sub + sub) * cq, cq)
        qq = q_ref[sub * cq:(sub + 1) * cq, :]
        if mask_prefix:
            valid = (r0 + lax.broadcasted_iota(jnp.int32, (1, kw), 1)) >= WINDOW
        for hk in range(ATT_KV_HEADS):
            q4 = jnp.concatenate(
                [qq[:, hk * gw + g * ATT_HEAD_DIM:hk * gw + (g + 1) * ATT_HEAD_DIM] for g in range(ATT_GROUP)],
                axis=0)
            kh = k_ref[0, hk, pl.ds(r0, kw), :]
            sc = lax.dot_general(q4, kh, (((1,), (1,)), ((), ())), preferred_element_type=F32)
            sc = sc * scale + bias_ref[hk]
            if mask_prefix:
                sc = jnp.where(valid, sc, -1e30)
            scs[sub, hk] = sc
            vhs[sub, hk] = v_ref[0, hk, pl.ds(r0, kw), :]
    es, rs = {}, {}
    for sub, hk in pairs:
        sink = sink_ref[hk]
        mx = jnp.maximum(jnp.max(scs[sub, hk], axis=-1, keepdims=True), sink)
        e = jnp.exp(scs[sub, hk] - mx)
        den = jnp.sum(e, axis=-1, keepdims=True) + jnp.exp(sink - mx)
        es[sub, hk] = e.astype(BF16)
        rs[sub, hk] = 1.0 / den
    for sub, hk in pairs:
        o4 = jnp.dot(es[sub, hk], vhs[sub, hk], preferred_element_type=F32) * rs[sub, hk]
        o_ref[sub * cq:(sub + 1) * cq, hk * gw:(hk + 1) * gw] = jnp.concatenate(
            [o4[g * cq:(g + 1) * cq, :] for g in range(ATT_GROUP)], axis=1).astype(o_ref.dtype)


def _attention(h_main, kall, vall, bias, sinks, bt, s, cq, n_sub, mask_prefix):
    tq = cq * n_sub
    nq = s // tq
    kw = WINDOW + cq
    bias4 = bias.reshape(ATT_KV_HEADS, ATT_GROUP * cq, kw)
    sink4 = jnp.repeat(sinks, cq).reshape(ATT_KV_HEADS, ATT_GROUP * cq, 1)
    per_head = lambda a: jnp.swapaxes(a.reshape(bt, WINDOW + s, ATT_KV_HEADS, ATT_HEAD_DIM), 1, 2)
    kern = functools.partial(_attn_kernel, cq=cq, n_sub=n_sub, mask_prefix=mask_prefix)
    return pl.pallas_call(
        kern,
        grid=(bt, nq),
        in_specs=[
            pl.BlockSpec((tq, D_MODEL), lambda b, i: (b * nq + i, COL_Q)),
            pl.BlockSpec((1, ATT_KV_HEADS, WINDOW + s, ATT_HEAD_DIM), lambda b, i: (b, 0, 0, 0)),
            pl.BlockSpec((1, ATT_KV_HEADS, WINDOW + s, ATT_HEAD_DIM), lambda b, i: (b, 0, 0, 0)),
            pl.BlockSpec((ATT_KV_HEADS, ATT_GROUP * cq, kw), lambda b, i: (0, 0, 0)),
            pl.BlockSpec((ATT_KV_HEADS, ATT_GROUP * cq, 1), lambda b, i: (0, 0, 0)),
        ],
        out_specs=pl.BlockSpec((tq, D_MODEL), lambda b, i: (b * nq + i, 0)),
        out_shape=jax.ShapeDtypeStruct((bt * s, D_MODEL), BF16),
        compiler_params=_cparams("arbitrary", "arbitrary"),
        name="swa_attention",
    )(h_main, per_head(kall), per_head(vall), bias4, sink4)


def _merge_kernel(x_ref, ssd_ref, att_ref, scb_ref, scc_ref, sch_ref, gssd_ref, gatt_ref, gsc_ref, hist_ref,
                  wssd_ref, watt_ref, wsc_ref, wo_ref, scw_ref, lng_ref, lnb_ref, rw_ref, rb_ref,
                  x1r_ref, ir_ref, gate_ref, cnt_ref, histout_ref, stat_ref,
                  ext_ref, base_ref, *, tm, tiles_per_batch):
    i = pl.program_id(0)

    @pl.when(i == 0)
    def _():
        base_ref[...] = jnp.zeros_like(base_ref)

    @pl.when(i % tiles_per_batch == 0)
    def _():
        ext_ref[0:SUBLANES, :] = hist_ref[0]

    u = scc_ref[...].astype(F32) * sch_ref[...].astype(F32)
    ext_ref[SUBLANES:SUBLANES + tm, :] = u
    off = SUBLANES - (SC_WIDTH - 1)
    conv = ext_ref[off:off + tm, :] * scw_ref[0:1, :]
    for j in range(1, SC_WIDTH):
        conv = conv + ext_ref[off + j:off + j + tm, :] * scw_ref[j:j + 1, :]
    tail = ext_ref[tm:tm + SUBLANES, :]
    ext_ref[0:SUBLANES, :] = tail
    histout_ref[0] = tail
    sc_pre = (scb_ref[...].astype(F32) * conv).astype(BF16)

    y_sc = jnp.dot(sc_pre, wsc_ref[...], preferred_element_type=F32)
    y_ssd = jnp.dot(ssd_ref[...], wssd_ref[...], preferred_element_type=F32)
    y_att = jnp.dot(att_ref[...], watt_ref[...], preferred_element_type=F32)
    merged = (_sigmoid(gssd_ref[...].astype(F32)) * y_ssd
              + _sigmoid(gatt_ref[...].astype(F32)) * y_att
              + _sigmoid(gsc_ref[...].astype(F32)) * y_sc)
    v = ALPHA * x_ref[...] + jnp.dot(merged.astype(BF16), wo_ref[...], preferred_element_type=F32)
    mu = jnp.mean(v, axis=-1, keepdims=True)
    vc = v - mu
    var = jnp.mean(vc * vc, axis=-1, keepdims=True)
    x1 = vc * lax.rsqrt(var + LN_EPS) * lng_ref[...] + lnb_ref[...]
    for j in range(ROW_TILES):
        x1r_ref[pl.ds(j, tm, stride=ROW_TILES), :] = x1[:, j * LANES:(j + 1) * LANES]

    rt = min(tm, ROUTE_TILE)
    lane = lax.broadcasted_iota(jnp.int32, (rt, LANES), 1)
    lane1 = lax.broadcasted_iota(jnp.int32, (1, LANES), 1)
    row8 = lax.broadcasted_iota(jnp.int32, (SUBLANES, LANES), 0)
    ri = lax.broadcasted_iota(jnp.int32, (rt, rt), 0)
    ci = lax.broadcasted_iota(jnp.int32, (rt, rt), 1)
    ltri = (ri > ci).astype(BF16)
    gi = lax.broadcasted_iota(jnp.int32, (LANES, LANES), 0)
    gj = lax.broadcasted_iota(jnp.int32, (LANES, LANES), 1)
    earlier_expert = jnp.logical_and(gi < gj, gi // N_EXPERTS == gj // N_EXPERTS).astype(BF16)
    logits_all = jnp.dot(x1.astype(BF16), rw_ref[...], preferred_element_type=F32) + rb_ref[...]
    for sub in range(tm // rt):
        rows = slice(sub * rt, (sub + 1) * rt)
        work = jnp.where(lane < N_EXPERTS, logits_all[rows], -jnp.inf)
        vals, idxs = [], []
        onehot = jnp.zeros((rt, LANES), F32)
        for k in range(TOP_K):
            mv = jnp.max(work, axis=-1, keepdims=True)
            mi = jnp.min(jnp.where(work == mv, lane, LANES), axis=-1, keepdims=True)
            vals.append(mv)
            idxs.append(mi)
            work = jnp.where(lane == mi, -jnp.inf, work)
            onehot = onehot + (lane == mi + k * N_EXPERTS).astype(F32)
        es = [jnp.exp(vk - vals[0]) for vk in vals]
        den = es[0] + es[1] + es[2] + es[3]
        prefix = jnp.dot(ltri, onehot.astype(BF16), preferred_element_type=F32)
        cnt = jnp.sum(onehot, axis=0, keepdims=True)
        base = base_ref[0:1, :]
        within = jnp.zeros((1, LANES), F32)
        tot = cnt
        for sh in range(1, TOP_K):
            rolled = pltpu.roll(cnt, sh * N_EXPERTS, 1)
            within = within + jnp.where(lane1 >= sh * N_EXPERTS, rolled, 0.0)
            tot = tot + rolled
        chunks = jnp.floor((tot + (SEG_ALIGN - 1)) * (1.0 / SEG_ALIGN))
        seg_start = SEG_ALIGN * jnp.dot(jnp.broadcast_to(chunks, (SUBLANES, LANES)).astype(BF16), earlier_expert,
                                        preferred_element_type=F32)[0:1, :]
        rank_all = prefix + (within + base)
        loc_all = prefix + (within + seg_start)
        ir = jnp.zeros((rt, LANES), jnp.int32)
        gt = jnp.zeros((rt, LANES), F32)
        for k in range(TOP_K):
            sel = lane == idxs[k] + k * N_EXPERTS
            rk = jnp.sum(jnp.where(sel, rank_all, 0.0), axis=-1, keepdims=True).astype(jnp.int32)
            lc = jnp.sum(jnp.where(sel, loc_all, 0.0), axis=-1, keepdims=True).astype(jnp.int32)
            ir = jnp.where(lane == k, idxs[k], ir)
            ir = jnp.where(lane == TOP_K + k, rk, ir)
            ir = jnp.where(lane == 2 * TOP_K + k, lc, ir)
            gt = jnp.where(lane == k, es[k] / den, gt)
        ir_ref[rows, :] = ir
        gate_ref[rows, :] = gt
        stat_ref[sub * SUBLANES:(sub + 1) * SUBLANES, :] = jnp.where(
            row8 == 0, tot, jnp.where(row8 == 1, seg_start, jnp.where(row8 == 2, base, 0.0)))
        base_ref[...] = jnp.broadcast_to(base + tot, base_ref.shape)
    cnt_ref[...] = base_ref[...]


def _merge(x2d, h_main, ssd_pre, att, schist8, p, bt, s, tm):
    t = bt * s
    nt = t // tm
    tiles_per_batch = s // tm
    row = lambda i: (i, 0)
    const2 = lambda i: (0, 0)
    hcol = lambda cidx: pl.BlockSpec((tm, D_MODEL), lambda i: (i, cidx))
    kern = functools.partial(_merge_kernel, tm=tm, tiles_per_batch=tiles_per_batch)
    return pl.pallas_call(
        kern,
        grid=(nt,),
        in_specs=[
            pl.BlockSpec((tm, D_MODEL), row),
            pl.BlockSpec((tm, SSD_D_INNER), row),
            pl.BlockSpec((tm, D_MODEL), row),
            hcol(COL_SCB), hcol(COL_SCC), hcol(COL_SCH), hcol(COL_GSSD), hcol(COL_GATT), hcol(COL_GSC),
            pl.BlockSpec((1, SUBLANES, D_MODEL), lambda i: (i // tiles_per_batch, 0, 0)),
            pl.BlockSpec((SSD_D_INNER, D_MODEL), const2, pipeline_mode=pl.Buffered(1)),
            pl.BlockSpec((D_MODEL, D_MODEL), const2, pipeline_mode=pl.Buffered(1)),
            pl.BlockSpec((D_MODEL, D_MODEL), const2, pipeline_mode=pl.Buffered(1)),
            pl.BlockSpec((D_MODEL, D_MODEL), const2, pipeline_mode=pl.Buffered(1)),
            pl.BlockSpec((SC_WIDTH, D_MODEL), const2),
            pl.BlockSpec((1, D_MODEL), const2),
            pl.BlockSpec((1, D_MODEL), const2),
            pl.BlockSpec((D_MODEL, LANES), const2),
            pl.BlockSpec((1, LANES), const2),
        ],
        out_specs=[
            pl.BlockSpec((tm * ROW_TILES, LANES), row),
            pl.BlockSpec((tm, LANES), row),
            pl.BlockSpec((tm, LANES), row),
            pl.BlockSpec((SUBLANES, LANES), const2),
            pl.BlockSpec((1, SUBLANES, D_MODEL), lambda i: (i // tiles_per_batch, 0, 0)),
            pl.BlockSpec((tm // min(tm, ROUTE_TILE) * SUBLANES, LANES), row),
        ],
        out_shape=[
            jax.ShapeDtypeStruct((t * ROW_TILES, LANES), F32),
            jax.ShapeDtypeStruct((t, LANES), jnp.int32),
            jax.ShapeDtypeStruct((t, LANES), F32),
            jax.ShapeDtypeStruct((SUBLANES, LANES), F32),
            jax.ShapeDtypeStruct((bt, SUBLANES, D_MODEL), F32),
            jax.ShapeDtypeStruct((t // min(tm, ROUTE_TILE) * SUBLANES, LANES), F32),
        ],
        scratch_shapes=[
            pltpu.VMEM((tm + SUBLANES, D_MODEL), F32),
            pltpu.VMEM((SUBLANES, LANES), F32),
        ],
        compiler_params=_cparams("arbitrary"),
        name="merge_ln_router",
    )(x2d, ssd_pre, att, h_main, h_main, h_main, h_main, h_main, h_main, schist8,
      p["w_ssd_out"], p["w_att_out"], p["w_sc_out"], p["w_o"], p["sc_w"], p["ln1_g"], p["ln1_b"],
      p["router_w"], p["router_b"])


def _row_copy_wait(src_hbm, dst, sem, n_rows):
    pltpu.make_async_copy(src_hbm.at[pl.ds(0, n_rows * ROW_TILES), :],
                          dst.at[pl.ds(0, n_rows * ROW_TILES), :], sem).wait()


def _dispatch_kernel(pstart_ref, pad_ref, ir_ref, x_ref, xs_hbm, zero_ref, sem, zsem, *, tm):
    @pl.when(pl.program_id(0) == 0)
    def _():
        zero_ref[...] = jnp.zeros_like(zero_ref)

        def zero_copy(row, n_rows):
            dst = pl.multiple_of(row * ROW_TILES, ROW_TILES)
            return pltpu.make_async_copy(zero_ref.at[pl.ds(0, n_rows * ROW_TILES), :],
                                         xs_hbm.at[pl.ds(dst, n_rows * ROW_TILES), :], zsem)

        def per_expert(start):
            def body(e, carry):
                first = pad_ref[e]
                n = pad_ref[N_EXPERTS + 1 + e]
                n_big = n // ZERO_ROWS

                def big(j, c):
                    cp = zero_copy(first + j * ZERO_ROWS, ZERO_ROWS)
                    cp.start() if start else cp.wait()
                    return c

                def small(j, c):
                    cp = zero_copy(first + j, 1)
                    cp.start() if start else cp.wait()
                    return c

                lax.fori_loop(0, n_big, big, 0)
                lax.fori_loop(n_big * ZERO_ROWS, n, small, 0)
                return carry

            lax.fori_loop(0, N_EXPERTS + 1, body, 0)

        per_expert(True)
        per_expert(False)

    def issue(tk, carry):
        src = pl.multiple_of(tk * ROW_TILES, ROW_TILES)
        for k in range(TOP_K):
            dest = pstart_ref[ir_ref[tk * SUBLANES + k]] + ir_ref[tk * SUBLANES + TOP_K + k]
            dst = pl.multiple_of(dest * ROW_TILES, ROW_TILES)
            pltpu.make_async_copy(x_ref.at[pl.ds(src, ROW_TILES), :],
                                  xs_hbm.at[pl.ds(dst, ROW_TILES), :], sem).start()
        return carry

    lax.fori_loop(0, tm, issue, 0)
    for _ in range(TOP_K):
        _row_copy_wait(x_ref, xs_hbm, sem, tm)


def _dispatch(x1r, ir_flat, pstart, pad_info, rows, t, tm):
    return pl.pallas_call(
        functools.partial(_dispatch_kernel, tm=tm),
        grid_spec=pltpu.PrefetchScalarGridSpec(
            num_scalar_prefetch=2,
            grid=(t // tm,),
            in_specs=[
                pl.BlockSpec((tm * SUBLANES,), lambda i, ps, pd: (i,), memory_space=pltpu.SMEM),
                pl.BlockSpec((tm * ROW_TILES, LANES), lambda i, ps, pd: (i, 0)),
            ],
            out_specs=pl.BlockSpec(memory_space=pl.ANY),
            scratch_shapes=[pltpu.VMEM((ZERO_ROWS * ROW_TILES, LANES), F32),
                            pltpu.SemaphoreType.DMA, pltpu.SemaphoreType.DMA],
        ),
        out_shape=jax.ShapeDtypeStruct((rows * ROW_TILES, LANES), F32),
        compiler_params=_cparams("arbitrary"),
        name="moe_dispatch",
    )(pstart, pad_info, ir_flat, x1r)


def _moe_kernel(be_ref, nv_ref, x_ref, wu_ref, bu_ref, wd_ref, bd_ref, o_ref, wub_ref, wdb_ref, *, tmo):
    i = pl.program_id(0)
    valid = i < nv_ref[0]
    new_expert = jnp.logical_or(i == 0, be_ref[i] != be_ref[jnp.maximum(i - 1, 0)])

    @pl.when(jnp.logical_and(valid, new_expert))
    def _():
        wub_ref[...] = wu_ref[0, 0].astype(BF16)
        wdb_ref[...] = wd_ref[0, 0].astype(BF16)

    @pl.when(valid)
    def _():
        x = jnp.concatenate([x_ref[pl.ds(j, tmo, stride=ROW_TILES), :] for j in range(ROW_TILES)], axis=1)
        h = jnp.dot(x.astype(BF16), wub_ref[...], preferred_element_type=F32) + bu_ref[0, 0]
        gate = jnp.minimum(h[:, :D_FF], SWIGLU_LIMIT)
        up = jnp.clip(h[:, D_FF:], -SWIGLU_LIMIT, SWIGLU_LIMIT)
        act = (up + 1.0) * gate * _sigmoid(SWIGLU_ALPHA * gate)
        o = jnp.dot(act.astype(BF16), wdb_ref[...], preferred_element_type=F32) + bd_ref[0, 0]
        for j in range(ROW_TILES):
            o_ref[pl.ds(j, tmo, stride=ROW_TILES), :] = o[:, j * LANES:(j + 1) * LANES]

    @pl.when(jnp.logical_not(valid))
    def _():
        o_ref[...] = jnp.zeros_like(o_ref)


def _moe_experts(xs, block_e, nvalid, p, n_blocks, tmo):
    l = p["layer"]
    last_valid = lambda i, nv: jnp.minimum(i, nv[0] - 1)
    return pl.pallas_call(
        functools.partial(_moe_kernel, tmo=tmo),
        grid_spec=pltpu.PrefetchScalarGridSpec(
            num_scalar_prefetch=2,
            grid=(n_blocks,),
            in_specs=[
                pl.BlockSpec((tmo * ROW_TILES, LANES), lambda i, be, nv: (last_valid(i, nv), 0)),
                pl.BlockSpec((1, 1, D_MODEL, 2 * D_FF), lambda i, be, nv: (l, be[i], 0, 0)),
                pl.BlockSpec((1, 1, 1, 2 * D_FF), lambda i, be, nv: (l, be[i], 0, 0)),
                pl.BlockSpec((1, 1, D_FF, D_MODEL), lambda i, be, nv: (l, be[i], 0, 0)),
                pl.BlockSpec((1, 1, 1, D_MODEL), lambda i, be, nv: (l, be[i], 0, 0)),
            ],
            out_specs=pl.BlockSpec((tmo * ROW_TILES, LANES), lambda i, be, nv: (i, 0)),
            scratch_shapes=[pltpu.VMEM((D_MODEL, 2 * D_FF), BF16), pltpu.VMEM((D_FF, D_MODEL), BF16)],
        ),
        out_shape=jax.ShapeDtypeStruct(xs.shape, F32),
        compiler_params=_cparams("arbitrary"),
        name="moe_experts",
    )(block_e, nvalid, xs, p["w_up"], p["b_up"], p["w_down"], p["b_down"])


def _combine_seg_kernel(seg_ref, segn_ref, ir_ref, x1r_ref, gate_ref, lng_ref, lnb_ref, eo_hbm, o_ref,
                        stage_ref, sem, *, rt, nt):
    i = pl.program_id(0)
    slot = i % 2
    sr = stage_ref.shape[1] // ROW_TILES

    @pl.when(i == 0)
    def _():
        stage_ref[...] = jnp.zeros_like(stage_ref)

    def copies(s_ref, s, start):
        def per_expert(e, carry):
            n_copies = (s_ref[e] + (SEG_ALIGN - 1)) // SEG_ALIGN
            dst0 = s_ref[N_EXPERTS + e]
            src0 = s_ref[2 * N_EXPERTS + e]

            def one(j, c):
                src = pl.multiple_of((src0 + j * SEG_ALIGN) * ROW_TILES, ROW_TILES)
                dst = pl.multiple_of((dst0 + j * SEG_ALIGN) * ROW_TILES, SEG_ALIGN * ROW_TILES)
                cp = pltpu.make_async_copy(eo_hbm.at[pl.ds(src, SEG_ALIGN * ROW_TILES), :],
                                           stage_ref.at[s, pl.ds(dst, SEG_ALIGN * ROW_TILES), :], sem.at[s])
                cp.start() if start else cp.wait()
                return c

            lax.fori_loop(0, n_copies, one, 0)
            return carry

        lax.fori_loop(0, N_EXPERTS, per_expert, 0)

    @pl.when(i == 0)
    def _():
        copies(seg_ref, 0, True)

    @pl.when(i + 1 < nt)
    def _():
        copies(segn_ref, 1 - slot, True)

    copies(seg_ref, slot, False)

    rows = jnp.concatenate([stage_ref[slot, pl.ds(j, sr, stride=ROW_TILES), :] for j in range(ROW_TILES)], axis=1)
    rows_b = rows.astype(BF16)
    pos = lax.broadcasted_iota(jnp.int32, (rt, sr), 1)
    ir = ir_ref[...]
    gates = gate_ref[...]
    weight = jnp.zeros((rt, sr), F32)
    for k in range(TOP_K):
        weight = jnp.where(pos == ir[:, 2 * TOP_K + k:2 * TOP_K + k + 1], gates[:, k:k + 1], weight)
    w_hi = weight.astype(BF16)
    w_lo = (weight - w_hi.astype(F32)).astype(BF16)
    moe = jnp.dot(w_hi, rows_b, preferred_element_type=F32) + jnp.dot(w_lo, rows_b, preferred_element_type=F32)
    x1 = jnp.concatenate([x1r_ref[pl.ds(j, rt, stride=ROW_TILES), :] for j in range(ROW_TILES)], axis=1)
    v = ALPHA * x1 + moe
    mu = jnp.mean(v, axis=-1, keepdims=True)
    vc = v - mu
    var = jnp.mean(vc * vc, axis=-1, keepdims=True)
    o_ref[...] = vc * lax.rsqrt(var + LN_EPS) * lng_ref[...] + lnb_ref[...]


def _combine_seg(x1r, ir, gates, seg, eo, p, t, rt):
    nt = t // rt
    sr = rt * TOP_K + N_EXPERTS * SEG_ALIGN
    seg_len = 4 * N_EXPERTS
    return pl.pallas_call(
        functools.partial(_combine_seg_kernel, rt=rt, nt=nt),
        grid=(nt,),
        in_specs=[
            pl.BlockSpec((seg_len,), lambda i: (i,), memory_space=pltpu.SMEM),
            pl.BlockSpec((seg_len,), lambda i: (jnp.minimum(i + 1, nt - 1),), memory_space=pltpu.SMEM),
            pl.BlockSpec((rt, LANES), lambda i: (i, 0)),
            pl.BlockSpec((rt * ROW_TILES, LANES), lambda i: (i, 0)),
            pl.BlockSpec((rt, LANES), lambda i: (i, 0)),
            pl.BlockSpec((1, D_MODEL), lambda i: (0, 0)),
            pl.BlockSpec((1, D_MODEL), lambda i: (0, 0)),
            pl.BlockSpec(memory_space=pl.ANY),
        ],
        out_specs=pl.BlockSpec((rt, D_MODEL), lambda i: (i, 0)),
        out_shape=jax.ShapeDtypeStruct((t, D_MODEL), F32),
        scratch_shapes=[pltpu.VMEM((2, sr * ROW_TILES, LANES), F32), pltpu.SemaphoreType.DMA((2,))],
        compiler_params=_cparams("arbitrary"),
        name="moe_combine_ln",
    )(seg, seg, ir, x1r, gates, p["ln2_g"], p["ln2_b"], eo)


def _moe(x1r, ir, gates, cnt, stat, p, t, tm, tmo, rt):
    n_assign = t * TOP_K
    n_blocks = (n_assign + N_EXPERTS * (tmo - 1) + tmo - 1) // tmo
    rows = n_blocks * tmo
    counts = cnt[0, :N_EXPERTS].astype(jnp.int32)
    padded = (counts + tmo - 1) // tmo * tmo
    pend = jnp.cumsum(padded)
    pstart = (pend - padded).astype(jnp.int32)
    pad_info = jnp.concatenate([pstart + counts, pend[-1:], padded - counts, rows - pend[-1:]]).astype(jnp.int32)
    block_start = jnp.arange(n_blocks, dtype=jnp.int32) * tmo
    block_e = jnp.minimum(jnp.sum((pend[None, :] <= block_start[:, None]).astype(jnp.int32), axis=1),
                          N_EXPERTS - 1)
    nvalid = (pend[-1:] // tmo).astype(jnp.int32)
    ir_flat = ir[:, :SUBLANES].reshape(t * SUBLANES)
    xs = _dispatch(x1r, ir_flat, pstart, pad_info, rows, t, tm)
    eo = _moe_experts(xs, block_e, nvalid, p, n_blocks, tmo)
    st3 = stat.reshape(t // rt, SUBLANES, LANES)[:, :3, :N_EXPERTS].astype(jnp.int32)
    seg = jnp.concatenate([st3[:, 0], st3[:, 1], st3[:, 2] + pstart[None, :], jnp.zeros_like(st3[:, 0])],
                          axis=1).reshape(-1)
    return _combine_seg(x1r, ir, gates, seg, eo, p, t, rt)


def _t5_bucket(rel):
    half = N_BUCKETS // 2
    max_exact = half // 2
    ret = jnp.where(rel > 0, half, 0)
    n = jnp.abs(rel)
    nf = jnp.maximum(n, 1).astype(F32)
    large = max_exact + (jnp.log(nf / max_exact) / math.log(MAX_DISTANCE / max_exact)
                         * (half - max_exact)).astype(jnp.int32)
    large = jnp.minimum(large, half - 1)
    return ret + jnp.where(n < max_exact, n, large)


def _rel_bias(table, cq):
    koff = jnp.arange(WINDOW + cq) - WINDOW
    rel = koff[None, :] - jnp.arange(cq)[:, None]
    return jnp.transpose(table[_t5_bucket(rel)].astype(F32), (2, 0, 1))


def _pad_rows_front(a, rows):
    pad = jnp.zeros(a.shape[:1] + (rows - a.shape[1],) + a.shape[2:], a.dtype)
    return jnp.concatenate([pad, a], axis=1)


def _layer_params(l, w_in, ssd_conv_w, ssd_conv_b, ssd_dt_bias, ssd_a_log, ssd_d, ssd_norm_w, w_ssd_out,
                  attn_sinks, w_attn_out, sc_conv_w, w_sc_out, w_o, ln1_g, ln1_b,
                  router_w, router_b, w_up, b_up, w_down, b_down, ln2_g, ln2_b):
    wi = w_in[l]
    w_dt = wi[:, OFF_DT:OFF_Q]
    return {
        "w_main": jnp.concatenate([wi[:, :OFF_DT], wi[:, OFF_Q:OFF_K], wi[:, OFF_SCB:]], axis=1).astype(BF16),
        "w_kv": wi[:, OFF_K:OFF_SCB].astype(BF16),
        "w_dt": jnp.pad(w_dt, ((0, 0), (0, LANES - SSD_N_HEADS))).astype(BF16),
        "w_dtT": w_dt.T[_HEADS_EVEN_ODD].astype(BF16),
        "conv_w": ssd_conv_w[l],
        "conv_b": ssd_conv_b[l][None, :],
        "dt_b128": jnp.pad(ssd_dt_bias[l], (0, LANES - SSD_N_HEADS))[None, :],
        "dt_bT": ssd_dt_bias[l][_HEADS_EVEN_ODD][:, None],
        "a_log128": jnp.pad(ssd_a_log[l], (0, LANES - SSD_N_HEADS))[None, :],
        "expand": jnp.asarray(_head_expand_matrix(), BF16),
        "a_logT": ssd_a_log[l][_HEADS_EVEN_ODD][:, None],
        "d_exp": jnp.repeat(ssd_d[l], SSD_HEAD_DIM)[None, :],
        "norm_w": ssd_norm_w[l][None, :],
        "w_ssd_out": w_ssd_out[l].astype(BF16),
        "sinks": attn_sinks[l],
        "w_att_out": w_attn_out[l].astype(BF16),
        "sc_w": sc_conv_w[l],
        "w_sc_out": w_sc_out[l].astype(BF16),
        "w_o": w_o[l].astype(BF16),
        "ln1_g": ln1_g[l][None, :],
        "ln1_b": ln1_b[l][None, :],
        "router_w": jnp.pad(router_w[l], ((0, 0), (0, LANES - N_EXPERTS))).astype(BF16),
        "router_b": jnp.pad(router_b[l], (0, LANES - N_EXPERTS))[None, :],
        "layer": l,
        "w_up": w_up,
        "b_up": b_up[:, :, None, :],
        "w_down": w_down,
        "b_down": b_down[:, :, None, :],
        "ln2_g": ln2_g[l][None, :],
        "ln2_b": ln2_b[l][None, :],
    }


def _trunk_layer(x2d, bt, s, p, bias, ssd_hist, ssd_state, sc_hist, kv_cache, cfg):
    t = bt * s
    h_main, h_kv = _in_proj(x2d, p["w_main"], p["w_kv"], cfg["tm_in"], cfg["tn_in"])
    k_new = h_kv[:, :ATT_KV_DIM].reshape(bt, s, ATT_KV_DIM)
    v_new = h_kv[:, ATT_KV_DIM:].reshape(bt, s, ATT_KV_DIM)

    if kv_cache is None:
        zpad = jnp.zeros((bt, WINDOW, ATT_KV_DIM), BF16)
        kall = jnp.concatenate([zpad, k_new.astype(BF16)], axis=1)
        vall = jnp.concatenate([zpad, v_new.astype(BF16)], axis=1)
    else:
        kall = jnp.concatenate([kv_cache[0].astype(BF16), k_new.astype(BF16)], axis=1)
        vall = jnp.concatenate([kv_cache[1].astype(BF16), v_new.astype(BF16)], axis=1)
    att = _attention(h_main, kall, vall, bias, p["sinks"], bt, s, cfg["cq"], cfg["n_sub"], kv_cache is None)

    if ssd_hist is None:
        hist8 = jnp.zeros((bt, SUBLANES, SSD_CONV_DIM), F32)
        st0 = jnp.zeros((bt, SSD_N_GROUPS, SSD_D_STATE, SSD_HEADS_PER_GROUP * SSD_HEAD_DIM), F32)
        schist8 = jnp.zeros((bt, SUBLANES, D_MODEL), F32)
    else:
        hist8 = _pad_rows_front(ssd_hist, SUBLANES)
        st0 = _state_to_kernel(ssd_state)
        schist8 = _pad_rows_front(sc_hist, SUBLANES)
    ssd_pre, st_out, hist_out = _ssd_grouped(x2d, h_main, hist8, st0, p, bt, s, cfg["cq"], cfg["ssd_cps"])

    x1r, ir, gates, cnt, schist_out, stat = _merge(x2d, h_main, ssd_pre, att, schist8, p, bt, s, cfg["tm_merge"])
    x2 = _moe(x1r, ir, gates, cnt, stat, p, t, cfg["tm_moe"], cfg["tmo"], min(cfg["tm_merge"], ROUTE_TILE))

    n_keep = min(s, WINDOW)
    new_k = k_new[:, s - n_keep:].reshape(bt, n_keep, ATT_KV_HEADS, ATT_HEAD_DIM)
    new_v = v_new[:, s - n_keep:].reshape(bt, n_keep, ATT_KV_HEADS, ATT_HEAD_DIM)
    new_state = _state_from_kernel(st_out)
    new_hist = hist_out[:, SUBLANES - (SSD_CONV - 1):]
    new_sc_hist = schist_out[:, SUBLANES - (SC_WIDTH - 1):]
    return x2, new_k, new_v, new_state, new_hist, new_sc_hist


def _config(bt, s, prompt):
    t = bt * s
    if prompt:
        return {"tm_in": min(t, 2048), "tn_in": 1024, "cq": CHUNK, "n_sub": min(8, s // CHUNK),
                "tm_merge": min(s, 512), "tm_moe": min(t, 256), "tmo": min(512, max(64, t // 32)),
                "ssd_cps": 4 if s % (4 * CHUNK) == 0 else 1}
    return {"tm_in": min(t, 256), "tn_in": 1024, "cq": s, "n_sub": 1,
            "tm_merge": s, "tm_moe": min(t, 256), "tmo": 64, "ssd_cps": 1}


def _forward(x_prompt, x_sample, cache_attn_k, cache_attn_v, state_ssd, state_ssd_conv, state_short_conv,
             rel_bias, layer_weights):
    bp, sp, _ = x_prompt.shape
    bd, sd, _ = x_sample.shape
    cfg_p = _config(bp, sp, True)
    cfg_d = _config(bd, sd, False)
    bias_p = _rel_bias(rel_bias, cfg_p["cq"])
    bias_d = _rel_bias(rel_bias, cfg_d["cq"])
    yp = x_prompt.reshape(bp * sp, D_MODEL)
    ys = x_sample.reshape(bd * sd, D_MODEL)
    outs_p, outs_d = [], []
    for l in range(DEPTH):
        p = _layer_params(l, *layer_weights)
        yp, *rest = _trunk_layer(yp, bp, sp, p, bias_p, None, None, None, None, cfg_p)
        outs_p.append(rest)
        cache = (cache_attn_k[l].reshape(bd, WINDOW, ATT_KV_DIM), cache_attn_v[l].reshape(bd, WINDOW, ATT_KV_DIM))
        ys, *rest = _trunk_layer(ys, bd, sd, p, bias_d, state_ssd_conv[l], state_ssd[l], state_short_conv[l],
                                 cache, cfg_d)
        outs_d.append(rest)
    stack = lambda outs, i: jnp.stack([o[i] for o in outs])
    return (yp.reshape(bp, sp, D_MODEL), ys.reshape(bd, sd, D_MODEL),
            stack(outs_p, 0), stack(outs_p, 1), stack(outs_p, 2), stack(outs_p, 3), stack(outs_p, 4),
            stack(outs_d, 0), stack(outs_d, 1), stack(outs_d, 2), stack(outs_d, 3), stack(outs_d, 4))


def kernel(x_prompt, x_sample, cache_attn_k, cache_attn_v, state_ssd, state_ssd_conv, state_short_conv, w_in, ssd_conv_w, ssd_conv_b, ssd_dt_bias, ssd_a_log, ssd_d, ssd_norm_w, w_ssd_out, attn_sinks, w_attn_out, rel_bias, sc_conv_w, w_sc_out, w_o, ln1_g, ln1_b, router_w, router_b, w_up, b_up, w_down, b_down, ln2_g, ln2_b):
    layer_weights = (w_in, ssd_conv_w, ssd_conv_b, ssd_dt_bias, ssd_a_log, ssd_d, ssd_norm_w, w_ssd_out,
                     attn_sinks, w_attn_out, sc_conv_w, w_sc_out, w_o, ln1_g, ln1_b,
                     router_w, router_b, w_up, b_up, w_down, b_down, ln2_g, ln2_b)
    return _forward(x_prompt, x_sample, cache_attn_k, cache_attn_v, state_ssd, state_ssd_conv, state_short_conv,
                    rel_bias, layer_weights)
```

```python
import functools
import math

import jax
import jax.numpy as jnp
import numpy as np
from jax import lax
from jax.experimental import pallas as pl
from jax.experimental.pallas import tpu as pltpu

F32 = jnp.float32
BF16 = jnp.bfloat16

D_MODEL = 1024
DEPTH = 2
CHUNK = 64
SSD_D_INNER = 2 * D_MODEL
SSD_HEAD_DIM = 64
SSD_N_HEADS = SSD_D_INNER // SSD_HEAD_DIM
SSD_N_GROUPS = 8
SSD_D_STATE = 128
SSD_CONV = 4
SSD_GN = SSD_N_GROUPS * SSD_D_STATE
SSD_CONV_DIM = SSD_D_INNER + 2 * SSD_GN
SSD_HEADS_PER_GROUP = SSD_N_HEADS // SSD_N_GROUPS
SSD_NORM_GROUP = SSD_D_INNER // SSD_N_GROUPS
ATT_HEAD_DIM = 64
ATT_HEADS = D_MODEL // ATT_HEAD_DIM
ATT_KV_HEADS = ATT_HEADS // 4
ATT_GROUP = ATT_HEADS // ATT_KV_HEADS
ATT_KV_DIM = ATT_KV_HEADS * ATT_HEAD_DIM
WINDOW = 128
N_BUCKETS = 32
MAX_DISTANCE = 128
SC_WIDTH = 3
N_EXPERTS = 32
TOP_K = 4
D_FF = D_MODEL
SWIGLU_LIMIT = 7.0
SWIGLU_ALPHA = 1.702
LN_EPS = 1e-5
ALPHA = (2.0 * DEPTH) ** 0.25

LANES = 128
SUBLANES = 8
ROW_TILES = D_MODEL // LANES
VMEM_LIMIT = 56 * 1024 * 1024
ZERO_ROWS = 16
ROUTE_TILE = 256
SEG_ALIGN = 8

COL_Z, COL_XS, COL_BC = 0, 1, 2
COL_Q, COL_SCB, COL_SCC, COL_SCH, COL_GSSD, COL_GATT, COL_GSC = 6, 7, 8, 9, 10, 11, 12
N_MAIN = 13 * D_MODEL
OFF_DT = SSD_D_INNER + SSD_CONV_DIM
OFF_Q = OFF_DT + SSD_N_HEADS
OFF_K = OFF_Q + D_MODEL
OFF_SCB = OFF_K + 2 * ATT_KV_DIM


def _sigmoid(v):
    return 0.5 * jnp.tanh(0.5 * v) + 0.5


def _cparams(*sem):
    return pltpu.CompilerParams(dimension_semantics=sem, vmem_limit_bytes=VMEM_LIMIT)


def _in_proj_kernel(x_ref, w_ref, wkv_ref, o_ref, kv_ref, xb_ref, *, n_main):
    j = pl.program_id(1)

    @pl.when(j == 0)
    def _():
        xb_ref[...] = x_ref[...].astype(BF16)

    @pl.when(j < n_main)
    def _():
        o_ref[...] = jnp.dot(xb_ref[...], w_ref[...], preferred_element_type=F32).astype(o_ref.dtype)

    @pl.when(j == n_main)
    def _():
        kv_ref[...] = jnp.dot(xb_ref[...], wkv_ref[...], preferred_element_type=F32)


def _in_proj(x, w, w_kv, tm, tn):
    t, k = x.shape
    n = w.shape[1]
    n_kv = w_kv.shape[1]
    n_main = n // tn
    main_col = lambda i, j: (i, jnp.minimum(j, n_main - 1))
    return pl.pallas_call(
        functools.partial(_in_proj_kernel, n_main=n_main),
        grid=(t // tm, n_main + 1),
        in_specs=[pl.BlockSpec((tm, k), lambda i, j: (i, 0)),
                  pl.BlockSpec((k, tn), lambda i, j: (0, jnp.minimum(j, n_main - 1))),
                  pl.BlockSpec((k, n_kv), lambda i, j: (0, 0))],
        out_specs=[pl.BlockSpec((tm, tn), main_col),
                   pl.BlockSpec((tm, n_kv), lambda i, j: (i, 0))],
        out_shape=[jax.ShapeDtypeStruct((t, n), BF16), jax.ShapeDtypeStruct((t, n_kv), F32)],
        scratch_shapes=[pltpu.VMEM((tm, k), BF16)],
        compiler_params=_cparams("arbitrary", "arbitrary"),
        name="in_proj",
    )(x, w, w_kv)


def _ssd_group_kernel(x_ref, z_ref, xs_ref, bc_ref, hist_ref, st0_ref,
                      wdt_ref, wdtT_ref, convw_ref, convb_ref, dtb_ref, dtbT_ref, alog_ref, alogT_ref,
                      dexp_ref, normw_ref, expand_ref,
                      y_ref, stout_ref, histout_ref,
                      ext_ref, ext_next_ref, xrow_ref, ynat_ref, st_refs, *, L, first, last):
    stride = L // SUBLANES
    gw = SSD_HEADS_PER_GROUP * SSD_HEAD_DIM

    n_cb = SSD_CONV_DIM // LANES
    lanes = lambda cb: slice(cb * LANES, (cb + 1) * LANES)

    def strided_rows(ref, start):
        return jnp.concatenate([ref[cb, pl.ds(start, SUBLANES, stride=stride), :] for cb in range(ref.shape[0])],
                               axis=1)

    if first is not None:
        @pl.when(first)
        def _():
            for cb in range(n_cb):
                ext_ref[cb, 0:SUBLANES, :] = hist_ref[0, :, lanes(cb)]
            for g in range(SSD_N_GROUPS):
                st_refs[g][...] = st0_ref[0, g]

    for cb in range(n_cb):
        src = xs_ref if cb < SSD_D_INNER // LANES else bc_ref
        ext_ref[cb, SUBLANES:SUBLANES + L, :] = src[:, lanes(cb % (SSD_D_INNER // LANES))].astype(F32)
    base = SUBLANES - (SSD_CONV - 1)
    taps = {o: strided_rows(ext_ref, o) for o in range(base, base + stride + SSD_CONV - 1)}
    pieces = []
    for r in range(stride):
        conv = convb_ref[...] + taps[base + r] * convw_ref[0:1, :]
        for j in range(1, SSD_CONV):
            conv = conv + taps[base + r + j] * convw_ref[j:j + 1, :]
        pieces.append(conv * _sigmoid(conv))
    xc = jnp.concatenate(pieces, axis=0)
    for cb in range(n_cb):
        tail = ext_ref[cb, L:L + SUBLANES, :]
        ext_next_ref[cb, 0:SUBLANES, :] = tail
        histout_ref[0, :, lanes(cb)] = tail
    xs_c = xc[:, :SSD_D_INNER]
    b_m = xc[:, SSD_D_INNER:SSD_D_INNER + SSD_GN]
    c_m = xc[:, SSD_D_INNER + SSD_GN:]

    for cb in range(D_MODEL // LANES):
        xrow_ref[cb] = x_ref[:, lanes(cb)]
    xb = jnp.concatenate([strided_rows(xrow_ref, r) for r in range(stride)], axis=0).astype(BF16)
    lane_h = lax.broadcasted_iota(jnp.int32, (1, LANES), 1) < SSD_N_HEADS
    dtr = jnp.dot(xb, wdt_ref[...], preferred_element_type=F32)
    dtrT = lax.dot_general(wdtT_ref[...], xb, (((1,), (1,)), ((), ())),
                           preferred_element_type=F32)

    def softplus(v):
        return jnp.maximum(v, 0.0) + jnp.log1p(jnp.exp(-jnp.abs(v)))

    dt = jnp.where(lane_h, softplus(dtr + dtb_ref[...]), 0.0)
    dtT = softplus(dtrT + dtbT_ref[...])
    adt = dt * (-jnp.exp(alog_ref[...]))
    adtT = dtT * (-jnp.exp(alogT_ref[...]))
    ri = lax.broadcasted_iota(jnp.int32, (L, L), 0)
    ci = lax.broadcasted_iota(jnp.int32, (L, L), 1)
    time_r = stride * (ri % SUBLANES) + ri // SUBLANES
    time_c = stride * (ci % SUBLANES) + ci // SUBLANES
    tri = time_r >= time_c
    acs = jnp.dot(tri.astype(F32), adt, precision=lax.Precision.HIGHEST,
                  preferred_element_type=F32)
    acsT = jnp.dot(adtT, (time_r <= time_c).astype(F32), precision=lax.Precision.HIGHEST,
                   preferred_element_type=F32)
    tot = jnp.sum(adt, axis=0, keepdims=True)
    paired = L == SSD_HEAD_DIM and 2 * L == LANES
    if paired:
        half = SSD_N_HEADS // 2
        acsT2 = jnp.concatenate([acsT[:half], acsT[half:]], axis=1)
        dtT2 = jnp.concatenate([dtT[:half], dtT[half:]], axis=1)
        rj = lax.broadcasted_iota(jnp.int32, (L, LANES), 0)
        cj = lax.broadcasted_iota(jnp.int32, (L, LANES), 1) % L
        tri2 = (stride * (rj % SUBLANES) + rj // SUBLANES) >= (stride * (cj % SUBLANES) + cj // SUBLANES)
        low_half = lax.broadcasted_iota(jnp.int32, (L, LANES), 1) < L

    cols = jnp.concatenate([acs, jnp.exp(acs), jnp.exp(tot - acs) * dt,
                            jnp.broadcast_to(jnp.exp(tot), (SUBLANES, LANES))], axis=0)
    cols = jnp.where(lane_h, cols, 0.0)
    hi = cols.astype(BF16).astype(F32)
    mid = (cols - hi).astype(BF16).astype(F32)
    lo = (cols - hi - mid).astype(BF16).astype(F32)
    packed = (hi + pltpu.roll(mid, SSD_N_HEADS, 1) + pltpu.roll(lo, 2 * SSD_N_HEADS, 1)).astype(BF16)
    wide = jnp.dot(packed, expand_ref[...], preferred_element_type=F32)
    acs_w = wide[0:L]
    eacs_w = wide[L:2 * L]
    w_w = wide[2 * L:3 * L]
    cd_w = wide[3 * L:3 * L + 1]

    for g in range(SSD_N_GROUPS):
        gs = slice(g * gw, (g + 1) * gw)
        bg = b_m[:, g * SSD_D_STATE:(g + 1) * SSD_D_STATE]
        bg_b = bg.astype(BF16)
        cg_b = c_m[:, g * SSD_D_STATE:(g + 1) * SSD_D_STATE].astype(BF16)
        if not paired:
            cb = lax.dot_general(cg_b, bg_b, (((1,), (1,)), ((), ())), preferred_element_type=F32)
        if L < LANES:
            bg = jnp.concatenate([bg, jnp.zeros((LANES - L, SSD_D_STATE), F32)], axis=0)
        bgT_b = bg.T[:, :L].astype(BF16)
        xs_g = xs_c[:, gs]
        st = st_refs[g][...]
        yo = jnp.dot(cg_b, st.astype(BF16), preferred_element_type=F32)
        stc = jnp.dot(bgT_b, (xs_g * w_w[:, gs]).astype(BF16), preferred_element_type=F32)
        st_refs[g][...] = st * cd_w[:, gs] + stc
        yds = []
        if paired:
            bg2_b = jnp.concatenate([bg_b, bg_b], axis=0)
            cb2 = lax.dot_general(cg_b, bg2_b, (((1,), (1,)), ((), ())), preferred_element_type=F32)
            for q in range(SSD_HEADS_PER_GROUP // 2):
                pair = g * (SSD_HEADS_PER_GROUP // 2) + q
                ls = slice(pair * LANES, (pair + 1) * LANES)
                decay = jnp.exp(jnp.where(tri2, acs_w[:, ls] - acsT2[pair:pair + 1, :], -jnp.inf))
                m = (cb2 * decay * dtT2[pair:pair + 1, :]).astype(BF16)
                xs2 = xs_c[:, ls]
                xs_bd = jnp.concatenate([jnp.where(low_half, xs2, 0.0), jnp.where(low_half, 0.0, xs2)],
                                        axis=0).astype(BF16)
                yds.append(jnp.dot(m, xs_bd, preferred_element_type=F32))
        else:
            for r in range(SSD_HEADS_PER_GROUP):
                h = g * SSD_HEADS_PER_GROUP + r
                hr = (h % 2) * (SSD_N_HEADS // 2) + h // 2
                col = acs_w[:, h * SSD_HEAD_DIM:h * SSD_HEAD_DIM + L]
                decay = jnp.exp(jnp.where(tri, col - acsT[hr:hr + 1, :], -jnp.inf))
                m = (cb * decay * dtT[hr:hr + 1, :]).astype(BF16)
                yds.append(jnp.dot(m, xs_g[:, r * SSD_HEAD_DIM:(r + 1) * SSD_HEAD_DIM].astype(BF16),
                                   preferred_element_type=F32))
        y_g = jnp.concatenate(yds, axis=1) + yo * eacs_w[:, gs] + dexp_ref[:, gs] * xs_g
        for r in range(stride):
            for cb in range(gw // LANES):
                ynat_ref[g * (gw // LANES) + cb, pl.ds(r, SUBLANES, stride=stride), :] = (
                    y_g[r * SUBLANES:(r + 1) * SUBLANES, lanes(cb)])

    z = z_ref[...].astype(F32)
    y = jnp.concatenate([ynat_ref[cb] for cb in range(SSD_D_INNER // LANES)], axis=1) * (z * _sigmoid(z))
    for g in range(SSD_N_GROUPS):
        sl = slice(g * SSD_NORM_GROUP, (g + 1) * SSD_NORM_GROUP)
        blk = y[:, sl]
        ms = jnp.mean(blk * blk, axis=-1, keepdims=True)
        y_ref[:, sl] = (blk * lax.rsqrt(ms + LN_EPS) * normw_ref[:, sl]).astype(y_ref.dtype)

    if last is not None:
        @pl.when(last)
        def _():
            for g in range(SSD_N_GROUPS):
                stout_ref[0, g] = st_refs[g][...]


def _ssd_step_kernel(x_ref, z_ref, xs_ref, bc_ref, hist_ref, st0_ref,
                     wdt_ref, wdtT_ref, convw_ref, convb_ref, dtb_ref, dtbT_ref, alog_ref, alogT_ref,
                     dexp_ref, normw_ref, expand_ref,
                     y_ref, stout_ref, histout_ref,
                     ext_ref, xrow_ref, ynat_ref, *st_refs, L, nc, cps):
    c = pl.program_id(1)
    for sub in range(cps):
        rows = pl.ds(sub * L, L)
        _ssd_group_kernel(
            x_ref.at[rows], z_ref.at[rows], xs_ref.at[rows], bc_ref.at[rows], hist_ref, st0_ref,
            wdt_ref, wdtT_ref, convw_ref, convb_ref, dtb_ref, dtbT_ref, alog_ref, alogT_ref,
            dexp_ref, normw_ref, expand_ref,
            y_ref.at[rows], stout_ref, histout_ref,
            ext_ref.at[sub], ext_ref.at[(sub + 1) % cps], xrow_ref.at[sub], ynat_ref.at[sub], st_refs,
            L=L, first=(c == 0) if sub == 0 else None, last=(c == nc - 1) if sub == cps - 1 else None)


def _ssd_grouped(x2d, h_main, hist8, st0, p, bt, s, L, cps):
    assert L % SUBLANES == 0 and L <= SSD_HEAD_DIM and s % (L * cps) == 0
    nc = s // (L * cps)
    L_blk = L * cps
    t = bt * s
    w2 = 2 * D_MODEL
    gw = SSD_HEADS_PER_GROUP * SSD_HEAD_DIM
    row = lambda b, c: (b * nc + c, 0)
    const2 = lambda b, c: (0, 0)
    state_spec = pl.BlockSpec((1, SSD_N_GROUPS, SSD_D_STATE, gw), lambda b, c: (b, 0, 0, 0))
    kern = functools.partial(_ssd_step_kernel, L=L, nc=nc, cps=cps)
    return pl.pallas_call(
        kern,
        grid=(bt, nc),
        in_specs=[
            pl.BlockSpec((L_blk, D_MODEL), row),
            pl.BlockSpec((L_blk, w2), lambda b, c: (b * nc + c, COL_Z)),
            pl.BlockSpec((L_blk, w2), lambda b, c: (b * nc + c, COL_XS)),
            pl.BlockSpec((L_blk, w2), lambda b, c: (b * nc + c, COL_BC)),
            pl.BlockSpec((1, SUBLANES, SSD_CONV_DIM), lambda b, c: (b, 0, 0)),
            state_spec,
            pl.BlockSpec((D_MODEL, LANES), const2),
            pl.BlockSpec((SSD_N_HEADS, D_MODEL), const2),
            pl.BlockSpec((SSD_CONV, SSD_CONV_DIM), const2),
            pl.BlockSpec((1, SSD_CONV_DIM), const2),
            pl.BlockSpec((1, LANES), const2),
            pl.BlockSpec((SSD_N_HEADS, 1), const2),
            pl.BlockSpec((1, LANES), const2),
            pl.BlockSpec((SSD_N_HEADS, 1), const2),
            pl.BlockSpec((1, SSD_D_INNER), const2),
            pl.BlockSpec((1, SSD_D_INNER), const2),
            pl.BlockSpec((LANES, SSD_D_INNER), const2),
        ],
        out_specs=[
            pl.BlockSpec((L_blk, SSD_D_INNER), row),
            state_spec,
            pl.BlockSpec((1, SUBLANES, SSD_CONV_DIM), lambda b, c: (b, 0, 0)),
        ],
        out_shape=[
            jax.ShapeDtypeStruct((t, SSD_D_INNER), BF16),
            jax.ShapeDtypeStruct((bt, SSD_N_GROUPS, SSD_D_STATE, gw), F32),
            jax.ShapeDtypeStruct((bt, SUBLANES, SSD_CONV_DIM), F32),
        ],
        scratch_shapes=[
            pltpu.VMEM((cps, SSD_CONV_DIM // LANES, L + SUBLANES, LANES), F32),
            pltpu.VMEM((cps, D_MODEL // LANES, L, LANES), F32),
            pltpu.VMEM((cps, SSD_D_INNER // LANES, L, LANES), F32),
        ] + [pltpu.VMEM((SSD_D_STATE, gw), F32) for _ in range(SSD_N_GROUPS)],
        compiler_params=_cparams("arbitrary", "arbitrary"),
        name="ssd_scan",
    )(x2d, h_main, h_main, h_main, hist8, st0,
      p["w_dt"], p["w_dtT"], p["conv_w"], p["conv_b"], p["dt_b128"], p["dt_bT"], p["a_log128"], p["a_logT"],
      p["d_exp"], p["norm_w"], p["expand"])


_HEADS_EVEN_ODD = np.concatenate([np.arange(0, SSD_N_HEADS, 2), np.arange(1, SSD_N_HEADS, 2)])


def _head_expand_matrix():
    e = np.zeros((LANES, SSD_D_INNER), np.float32)
    for part in range(3):
        for h in range(SSD_N_HEADS):
            e[part * SSD_N_HEADS + h, h * SSD_HEAD_DIM:(h + 1) * SSD_HEAD_DIM] = 1.0
    return e


def _state_to_kernel(state):
    b = state.shape[0]
    s5 = state.reshape(b, SSD_N_GROUPS, SSD_HEADS_PER_GROUP, SSD_HEAD_DIM, SSD_D_STATE)
    return jnp.transpose(s5, (0, 1, 4, 2, 3)).reshape(b, SSD_N_GROUPS, SSD_D_STATE,
                                                       SSD_HEADS_PER_GROUP * SSD_HEAD_DIM)


def _state_from_kernel(st):
    b = st.shape[0]
    s5 = st.reshape(b, SSD_N_GROUPS, SSD_D_STATE, SSD_HEADS_PER_GROUP, SSD_HEAD_DIM)
    return jnp.transpose(s5, (0, 1, 3, 4, 2)).reshape(b, SSD_N_HEADS, SSD_HEAD_DIM, SSD_D_STATE)


def _attn_kernel(q_ref, k_ref, v_ref, bias_ref, sink_ref, o_ref, *, cq, n_sub):
    step = pl.program_id(1)
    kw = WINDOW + cq
    scale = ATT_HEAD_DIM ** -0.5
    gw = ATT_GROUP * ATT_HEAD_DIM
    pairs = [(sub, hk) for sub in range(n_sub) for hk in range(ATT_KV_HEADS)]
    scs, vhs = {}, {}
    for sub in range(n_sub):
        r0 = pl.multiple_of((step * n_sub + sub) * cq, cq)
        qq = q_ref[sub * cq:(sub + 1) * cq, :]
        variant = jnp.minimum(step * n_sub + sub, bias_ref.shape[0] - 1)
        for hk in range(ATT_KV_HEADS):
            q4 = jnp.concatenate(
                [qq[:, hk * gw + g * ATT_HEAD_DIM:hk * gw + (g + 1) * ATT_HEAD_DIM] for g in range(ATT_GROUP)],
                axis=0)
            kh = k_ref[0, hk, pl.ds(r0, kw), :]
            sc = lax.dot_general(q4, kh, (((1,), (1,)), ((), ())), preferred_element_type=F32)
            scs[sub, hk] = sc * scale + bias_ref[variant, hk]
            vhs[sub, hk] = v_ref[0, hk, pl.ds(r0, kw), :]
    es, rs = {}, {}
    ones = jnp.ones((kw, LANES), BF16)
    for sub, hk in pairs:
        sink = sink_ref[hk]
        mx = jnp.maximum(jnp.max(scs[sub, hk], axis=-1, keepdims=True), sink)
        e = jnp.exp(scs[sub, hk] - mx).astype(BF16)
        den = jnp.dot(e, ones, preferred_element_type=F32)[:, :ATT_HEAD_DIM] + jnp.exp(sink - mx)
        es[sub, hk] = e
        rs[sub, hk] = 1.0 / den
    for sub, hk in pairs:
        o4 = jnp.dot(es[sub, hk], vhs[sub, hk], preferred_element_type=F32) * rs[sub, hk]
        o_ref[sub * cq:(sub + 1) * cq, hk * gw:(hk + 1) * gw] = jnp.concatenate(
            [o4[g * cq:(g + 1) * cq, :] for g in range(ATT_GROUP)], axis=1).astype(o_ref.dtype)


def _attention(h_main, kall, vall, bias, sinks, bt, s, cq, n_sub, mask_prefix):
    tq = cq * n_sub
    nq = s // tq
    kw = WINDOW + cq
    bias4 = bias.reshape(1, ATT_KV_HEADS, ATT_GROUP * cq, kw)
    if mask_prefix:
        key = jnp.arange(kw)
        bias4 = jnp.concatenate([jnp.where(c * cq + key >= WINDOW, bias4, -1e30) for c in range(WINDOW // cq)]
                                + [bias4], axis=0)
    n_var = bias4.shape[0]
    sink4 = jnp.repeat(sinks, cq).reshape(ATT_KV_HEADS, ATT_GROUP * cq, 1)
    per_head = lambda a: jnp.swapaxes(a.reshape(bt, WINDOW + s, ATT_KV_HEADS, ATT_HEAD_DIM), 1, 2)
    kern = functools.partial(_attn_kernel, cq=cq, n_sub=n_sub)
    return pl.pallas_call(
        kern,
        grid=(bt, nq),
        in_specs=[
            pl.BlockSpec((tq, D_MODEL), lambda b, i: (b * nq + i, COL_Q)),
            pl.BlockSpec((1, ATT_KV_HEADS, WINDOW + s, ATT_HEAD_DIM), lambda b, i: (b, 0, 0, 0)),
            pl.BlockSpec((1, ATT_KV_HEADS, WINDOW + s, ATT_HEAD_DIM), lambda b, i: (b, 0, 0, 0)),
            pl.BlockSpec((n_var, ATT_KV_HEADS, ATT_GROUP * cq, kw), lambda b, i: (0, 0, 0, 0)),
            pl.BlockSpec((ATT_KV_HEADS, ATT_GROUP * cq, 1), lambda b, i: (0, 0, 0)),
        ],
        out_specs=pl.BlockSpec((tq, D_MODEL), lambda b, i: (b * nq + i, 0)),
        out_shape=jax.ShapeDtypeStruct((bt * s, D_MODEL), BF16),
        compiler_params=_cparams("arbitrary", "arbitrary"),
        name="swa_attention",
    )(h_main, per_head(kall), per_head(vall), bias4, sink4)


def _merge_kernel(x_ref, ssd_ref, att_ref, scb_ref, scc_ref, sch_ref, gssd_ref, gatt_ref, gsc_ref, hist_ref,
                  wssd_ref, watt_ref, wsc_ref, wo_ref, scw_ref, lng_ref, lnb_ref, rw_ref, rb_ref,
                  x1r_ref, ir_ref, gate_ref, cnt_ref, histout_ref, stat_ref,
                  ext_ref, base_ref, *, tm, tiles_per_batch):
    i = pl.program_id(0)

    @pl.when(i == 0)
    def _():
        base_ref[...] = jnp.zeros_like(base_ref)

    @pl.when(i % tiles_per_batch == 0)
    def _():
        ext_ref[0:SUBLANES, :] = hist_ref[0]

    u = scc_ref[...].astype(F32) * sch_ref[...].astype(F32)
    ext_ref[SUBLANES:SUBLANES + tm, :] = u
    off = SUBLANES - (SC_WIDTH - 1)
    conv = ext_ref[off:off + tm, :] * scw_ref[0:1, :]
    for j in range(1, SC_WIDTH):
        conv = conv + ext_ref[off + j:off + j + tm, :] * scw_ref[j:j + 1, :]
    tail = ext_ref[tm:tm + SUBLANES, :]
    ext_ref[0:SUBLANES, :] = tail
    histout_ref[0] = tail
    sc_pre = (scb_ref[...].astype(F32) * conv).astype(BF16)

    y_sc = jnp.dot(sc_pre, wsc_ref[...], preferred_element_type=F32)
    y_ssd = jnp.dot(ssd_ref[...], wssd_ref[...], preferred_element_type=F32)
    y_att = jnp.dot(att_ref[...], watt_ref[...], preferred_element_type=F32)
    merged = (_sigmoid(gssd_ref[...].astype(F32)) * y_ssd
              + _sigmoid(gatt_ref[...].astype(F32)) * y_att
              + _sigmoid(gsc_ref[...].astype(F32)) * y_sc)
    v = ALPHA * x_ref[...] + jnp.dot(merged.astype(BF16), wo_ref[...], preferred_element_type=F32)
    mu = jnp.mean(v, axis=-1, keepdims=True)
    vc = v - mu
    var = jnp.mean(vc * vc, axis=-1, keepdims=True)
    x1 = vc * lax.rsqrt(var + LN_EPS) * lng_ref[...] + lnb_ref[...]
    for j in range(ROW_TILES):
        x1r_ref[pl.ds(j, tm, stride=ROW_TILES), :] = x1[:, j * LANES:(j + 1) * LANES]

    rt = min(tm, ROUTE_TILE)
    lane = lax.broadcasted_iota(jnp.int32, (rt, LANES), 1)
    lane1 = lax.broadcasted_iota(jnp.int32, (1, LANES), 1)
    row8 = lax.broadcasted_iota(jnp.int32, (SUBLANES, LANES), 0)
    ri = lax.broadcasted_iota(jnp.int32, (rt, rt), 0)
    ci = lax.broadcasted_iota(jnp.int32, (rt, rt), 1)
    ltri = (ri > ci).astype(BF16)
    gi = lax.broadcasted_iota(jnp.int32, (LANES, LANES), 0)
    gj = lax.broadcasted_iota(jnp.int32, (LANES, LANES), 1)
    earlier_expert = jnp.logical_and(gi < gj, gi // N_EXPERTS == gj // N_EXPERTS).astype(BF16)
    logits_all = jnp.dot(x1.astype(BF16), rw_ref[...], preferred_element_type=F32) + rb_ref[...]
    for sub in range(tm // rt):
        rows = slice(sub * rt, (sub + 1) * rt)
        work = jnp.where(lane < N_EXPERTS, logits_all[rows], -jnp.inf)
        vals, idxs = [], []
        onehot = jnp.zeros((rt, LANES), F32)
        for k in range(TOP_K):
            mv = jnp.max(work, axis=-1, keepdims=True)
            mi = jnp.min(jnp.where(work == mv, lane, LANES), axis=-1, keepdims=True)
            vals.append(mv)
            idxs.append(mi)
            work = jnp.where(lane == mi, -jnp.inf, work)
            onehot = onehot + (lane == mi + k * N_EXPERTS).astype(F32)
        es = [jnp.exp(vk - vals[0]) for vk in vals]
        den = es[0] + es[1] + es[2] + es[3]
        prefix = jnp.dot(ltri, onehot.astype(BF16), preferred_element_type=F32)
        cnt = jnp.sum(onehot, axis=0, keepdims=True)
        base = base_ref[0:1, :]
        within = jnp.zeros((1, LANES), F32)
        tot = cnt
        for sh in range(1, TOP_K):
            rolled = pltpu.roll(cnt, sh * N_EXPERTS, 1)
            within = within + jnp.where(lane1 >= sh * N_EXPERTS, rolled, 0.0)
            tot = tot + rolled
        chunks = jnp.floor((tot + (SEG_ALIGN - 1)) * (1.0 / SEG_ALIGN))
        seg_start = SEG_ALIGN * jnp.dot(jnp.broadcast_to(chunks, (SUBLANES, LANES)).astype(BF16), earlier_expert,
                                        preferred_element_type=F32)[0:1, :]
        rank_all = prefix + (within + base)
        loc_all = prefix + (within + seg_start)
        ir = jnp.zeros((rt, LANES), jnp.int32)
        gt = jnp.zeros((rt, LANES), F32)
        for k in range(TOP_K):
            sel = lane == idxs[k] + k * N_EXPERTS
            rk = jnp.sum(jnp.where(sel, rank_all, 0.0), axis=-1, keepdims=True).astype(jnp.int32)
            lc = jnp.sum(jnp.where(sel, loc_all, 0.0), axis=-1, keepdims=True).astype(jnp.int32)
            ir = jnp.where(lane == k, idxs[k], ir)
            ir = jnp.where(lane == TOP_K + k, rk, ir)
            ir = jnp.where(lane == 2 * TOP_K + k, lc, ir)
            gt = jnp.where(lane == k, es[k] / den, gt)
        ir_ref[rows, :] = ir
        gate_ref[rows, :] = gt
        stat_ref[sub * SUBLANES:(sub + 1) * SUBLANES, :] = jnp.where(
            row8 == 0, tot, jnp.where(row8 == 1, seg_start, jnp.where(row8 == 2, base, 0.0)))
        base_ref[...] = jnp.broadcast_to(base + tot, base_ref.shape)
    cnt_ref[...] = base_ref[...]


def _merge(x2d, h_main, ssd_pre, att, schist8, p, bt, s, tm):
    t = bt * s
    nt = t // tm
    tiles_per_batch = s // tm
    row = lambda i: (i, 0)
    const2 = lambda i: (0, 0)
    hcol = lambda cidx: pl.BlockSpec((tm, D_MODEL), lambda i: (i, cidx))
    kern = functools.partial(_merge_kernel, tm=tm, tiles_per_batch=tiles_per_batch)
    return pl.pallas_call(
        kern,
        grid=(nt,),
        in_specs=[
            pl.BlockSpec((tm, D_MODEL), row),
            pl.BlockSpec((tm, SSD_D_INNER), row),
            pl.BlockSpec((tm, D_MODEL), row),
            hcol(COL_SCB), hcol(COL_SCC), hcol(COL_SCH), hcol(COL_GSSD), hcol(COL_GATT), hcol(COL_GSC),
            pl.BlockSpec((1, SUBLANES, D_MODEL), lambda i: (i // tiles_per_batch, 0, 0)),
            pl.BlockSpec((SSD_D_INNER, D_MODEL), const2, pipeline_mode=pl.Buffered(1)),
            pl.BlockSpec((D_MODEL, D_MODEL), const2, pipeline_mode=pl.Buffered(1)),
            pl.BlockSpec((D_MODEL, D_MODEL), const2, pipeline_mode=pl.Buffered(1)),
            pl.BlockSpec((D_MODEL, D_MODEL), const2, pipeline_mode=pl.Buffered(1)),
            pl.BlockSpec((SC_WIDTH, D_MODEL), const2),
            pl.BlockSpec((1, D_MODEL), const2),
            pl.BlockSpec((1, D_MODEL), const2),
            pl.BlockSpec((D_MODEL, LANES), const2),
            pl.BlockSpec((1, LANES), const2),
        ],
        out_specs=[
            pl.BlockSpec((tm * ROW_TILES, LANES), row),
            pl.BlockSpec((tm, LANES), row),
            pl.BlockSpec((tm, LANES), row),
            pl.BlockSpec((SUBLANES, LANES), const2),
            pl.BlockSpec((1, SUBLANES, D_MODEL), lambda i: (i // tiles_per_batch, 0, 0)),
            pl.BlockSpec((tm // min(tm, ROUTE_TILE) * SUBLANES, LANES), row),
        ],
        out_shape=[
            jax.ShapeDtypeStruct((t * ROW_TILES, LANES), F32),
            jax.ShapeDtypeStruct((t, LANES), jnp.int32),
            jax.ShapeDtypeStruct((t, LANES), F32),
            jax.ShapeDtypeStruct((SUBLANES, LANES), F32),
            jax.ShapeDtypeStruct((bt, SUBLANES, D_MODEL), F32),
            jax.ShapeDtypeStruct((t // min(tm, ROUTE_TILE) * SUBLANES, LANES), F32),
        ],
        scratch_shapes=[
            pltpu.VMEM((tm + SUBLANES, D_MODEL), F32),
            pltpu.VMEM((SUBLANES, LANES), F32),
        ],
        compiler_params=_cparams("arbitrary"),
        name="merge_ln_router",
    )(x2d, ssd_pre, att, h_main, h_main, h_main, h_main, h_main, h_main, schist8,
      p["w_ssd_out"], p["w_att_out"], p["w_sc_out"], p["w_o"], p["sc_w"], p["ln1_g"], p["ln1_b"],
      p["router_w"], p["router_b"])


def _row_copy_wait(src_hbm, dst, sem, n_rows):
    pltpu.make_async_copy(src_hbm.at[pl.ds(0, n_rows * ROW_TILES), :],
                          dst.at[pl.ds(0, n_rows * ROW_TILES), :], sem).wait()


def _dispatch_kernel(pstart_ref, pad_ref, ir_ref, x_ref, xs_hbm, zero_ref, sem, zsem, *, tm):
    @pl.when(pl.program_id(0) == 0)
    def _():
        zero_ref[...] = jnp.zeros_like(zero_ref)

        def zero_copy(row, n_rows):
            dst = pl.multiple_of(row * ROW_TILES, ROW_TILES)
            return pltpu.make_async_copy(zero_ref.at[pl.ds(0, n_rows * ROW_TILES), :],
                                         xs_hbm.at[pl.ds(dst, n_rows * ROW_TILES), :], zsem)

        def per_expert(start):
            def body(e, carry):
                first = pad_ref[e]
                n = pad_ref[N_EXPERTS + 1 + e]
                n_big = n // ZERO_ROWS

                def big(j, c):
                    cp = zero_copy(first + j * ZERO_ROWS, ZERO_ROWS)
                    cp.start() if start else cp.wait()
                    return c

                def small(j, c):
                    cp = zero_copy(first + j, 1)
                    cp.start() if start else cp.wait()
                    return c

                lax.fori_loop(0, n_big, big, 0)
                lax.fori_loop(n_big * ZERO_ROWS, n, small, 0)
                return carry

            lax.fori_loop(0, N_EXPERTS + 1, body, 0)

        per_expert(True)
        per_expert(False)

    def issue(tk, carry):
        src = pl.multiple_of(tk * ROW_TILES, ROW_TILES)
        for k in range(TOP_K):
            dest = pstart_ref[ir_ref[tk * SUBLANES + k]] + ir_ref[tk * SUBLANES + TOP_K + k]
            dst = pl.multiple_of(dest * ROW_TILES, ROW_TILES)
            pltpu.make_async_copy(x_ref.at[pl.ds(src, ROW_TILES), :],
                                  xs_hbm.at[pl.ds(dst, ROW_TILES), :], sem).start()
        return carry

    lax.fori_loop(0, tm, issue, 0)
    for _ in range(TOP_K):
        _row_copy_wait(x_ref, xs_hbm, sem, tm)


def _dispatch(x1r, ir_flat, pstart, pad_info, rows, t, tm):
    return pl.pallas_call(
        functools.partial(_dispatch_kernel, tm=tm),
        grid_spec=pltpu.PrefetchScalarGridSpec(
            num_scalar_prefetch=2,
            grid=(t // tm,),
            in_specs=[
                pl.BlockSpec((tm * SUBLANES,), lambda i, ps, pd: (i,), memory_space=pltpu.SMEM),
                pl.BlockSpec((tm * ROW_TILES, LANES), lambda i, ps, pd: (i, 0)),
            ],
            out_specs=pl.BlockSpec(memory_space=pl.ANY),
            scratch_shapes=[pltpu.VMEM((ZERO_ROWS * ROW_TILES, LANES), F32),
                            pltpu.SemaphoreType.DMA, pltpu.SemaphoreType.DMA],
        ),
        out_shape=jax.ShapeDtypeStruct((rows * ROW_TILES, LANES), F32),
        compiler_params=_cparams("arbitrary"),
        name="moe_dispatch",
    )(pstart, pad_info, ir_flat, x1r)


def _moe_kernel(be_ref, nv_ref, x_ref, wu_ref, bu_ref, wd_ref, bd_ref, o_ref, wub_ref, wdb_ref, *, tmo):
    i = pl.program_id(0)
    valid = i < nv_ref[0]
    new_expert = jnp.logical_or(i == 0, be_ref[i] != be_ref[jnp.maximum(i - 1, 0)])

    @pl.when(jnp.logical_and(valid, new_expert))
    def _():
        wub_ref[...] = wu_ref[0, 0].astype(BF16)
        wdb_ref[...] = wd_ref[0, 0].astype(BF16)

    @pl.when(valid)
    def _():
        x = jnp.concatenate([x_ref[pl.ds(j, tmo, stride=ROW_TILES), :] for j in range(ROW_TILES)], axis=1)
        h = jnp.dot(x.astype(BF16), wub_ref[...], preferred_element_type=F32) + bu_ref[0, 0]
        gate = jnp.minimum(h[:, :D_FF], SWIGLU_LIMIT)
        up = jnp.clip(h[:, D_FF:], -SWIGLU_LIMIT, SWIGLU_LIMIT)
        act = (up + 1.0) * gate * _sigmoid(SWIGLU_ALPHA * gate)
        o = jnp.dot(act.astype(BF16), wdb_ref[...], preferred_element_type=F32) + bd_ref[0, 0]
        for j in range(ROW_TILES):
            o_ref[pl.ds(j, tmo, stride=ROW_TILES), :] = o[:, j * LANES:(j + 1) * LANES]

    @pl.when(jnp.logical_not(valid))
    def _():
        o_ref[...] = jnp.zeros_like(o_ref)


def _moe_experts(xs, block_e, nvalid, p, n_blocks, tmo):
    l = p["layer"]
    last_valid = lambda i, nv: jnp.minimum(i, nv[0] - 1)
    return pl.pallas_call(
        functools.partial(_moe_kernel, tmo=tmo),
        grid_spec=pltpu.PrefetchScalarGridSpec(
            num_scalar_prefetch=2,
            grid=(n_blocks,),
            in_specs=[
                pl.BlockSpec((tmo * ROW_TILES, LANES), lambda i, be, nv: (last_valid(i, nv), 0)),
                pl.BlockSpec((1, 1, D_MODEL, 2 * D_FF), lambda i, be, nv: (l, be[i], 0, 0)),
                pl.BlockSpec((1, 1, 1, 2 * D_FF), lambda i, be, nv: (l, be[i], 0, 0)),
                pl.BlockSpec((1, 1, D_FF, D_MODEL), lambda i, be, nv: (l, be[i], 0, 0)),
                pl.BlockSpec((1, 1, 1, D_MODEL), lambda i, be, nv: (l, be[i], 0, 0)),
            ],
            out_specs=pl.BlockSpec((tmo * ROW_TILES, LANES), lambda i, be, nv: (i, 0)),
            scratch_shapes=[pltpu.VMEM((D_MODEL, 2 * D_FF), BF16), pltpu.VMEM((D_FF, D_MODEL), BF16)],
        ),
        out_shape=jax.ShapeDtypeStruct(xs.shape, F32),
        compiler_params=_cparams("arbitrary"),
        name="moe_experts",
    )(block_e, nvalid, xs, p["w_up"], p["b_up"], p["w_down"], p["b_down"])


def _combine_seg_kernel(seg_ref, segn_ref, ir_ref, x1r_ref, gate_ref, lng_ref, lnb_ref, eo_hbm, o_ref,
                        stage_ref, sem, *, rt, nt):
    i = pl.program_id(0)
    slot = i % 2
    sr = stage_ref.shape[1] // ROW_TILES

    @pl.when(i == 0)
    def _():
        stage_ref[...] = jnp.zeros_like(stage_ref)

    def copies(s_ref, s, start):
        def per_expert(e, carry):
            n_copies = (s_ref[e] + (SEG_ALIGN - 1)) // SEG_ALIGN
            dst0 = s_ref[N_EXPERTS + e]
            src0 = s_ref[2 * N_EXPERTS + e]

            def one(j, c):
                src = pl.multiple_of((src0 + j * SEG_ALIGN) * ROW_TILES, ROW_TILES)
                dst = pl.multiple_of((dst0 + j * SEG_ALIGN) * ROW_TILES, SEG_ALIGN * ROW_TILES)
                cp = pltpu.make_async_copy(eo_hbm.at[pl.ds(src, SEG_ALIGN * ROW_TILES), :],
                                           stage_ref.at[s, pl.ds(dst, SEG_ALIGN * ROW_TILES), :], sem.at[s])
                cp.start() if start else cp.wait()
                return c

            lax.fori_loop(0, n_copies, one, 0)
            return carry

        lax.fori_loop(0, N_EXPERTS, per_expert, 0)

    @pl.when(i == 0)
    def _():
        copies(seg_ref, 0, True)

    @pl.when(i + 1 < nt)
    def _():
        copies(segn_ref, 1 - slot, True)

    copies(seg_ref, slot, False)

    rows = jnp.concatenate([stage_ref[slot, pl.ds(j, sr, stride=ROW_TILES), :] for j in range(ROW_TILES)], axis=1)
    rows_b = rows.astype(BF16)
    pos = lax.broadcasted_iota(jnp.int32, (rt, sr), 1)
    ir = ir_ref[...]
    gates = gate_ref[...]
    weight = jnp.zeros((rt, sr), F32)
    for k in range(TOP_K):
        weight = jnp.where(pos == ir[:, 2 * TOP_K + k:2 * TOP_K + k + 1], gates[:, k:k + 1], weight)
    w_hi = weight.astype(BF16)
    w_lo = (weight - w_hi.astype(F32)).astype(BF16)
    moe = jnp.dot(w_hi, rows_b, preferred_element_type=F32) + jnp.dot(w_lo, rows_b, preferred_element_type=F32)
    x1 = jnp.concatenate([x1r_ref[pl.ds(j, rt, stride=ROW_TILES), :] for j in range(ROW_TILES)], axis=1)
    v = ALPHA * x1 + moe
    mu = jnp.mean(v, axis=-1, keepdims=True)
    vc = v - mu
    var = jnp.mean(vc * vc, axis=-1, keepdims=True)
    o_ref[...] = vc * lax.rsqrt(var + LN_EPS) * lng_ref[...] + lnb_ref[...]


def _combine_seg(x1r, ir, gates, seg, eo, p, t, rt):
    nt = t // rt
    sr = rt * TOP_K + N_EXPERTS * SEG_ALIGN
    seg_len = 4 * N_EXPERTS
    return pl.pallas_call(
        functools.partial(_combine_seg_kernel, rt=rt, nt=nt),
        grid=(nt,),
        in_specs=[
            pl.BlockSpec((seg_len,), lambda i: (i,), memory_space=pltpu.SMEM),
            pl.BlockSpec((seg_len,), lambda i: (jnp.minimum(i + 1, nt - 1),), memory_space=pltpu.SMEM),
            pl.BlockSpec((rt, LANES), lambda i: (i, 0)),
            pl.BlockSpec((rt * ROW_TILES, LANES), lambda i: (i, 0)),
            pl.BlockSpec((rt, LANES), lambda i: (i, 0)),
            pl.BlockSpec((1, D_MODEL), lambda i: (0, 0)),
            pl.BlockSpec((1, D_MODEL), lambda i: (0, 0)),
            pl.BlockSpec(memory_space=pl.ANY),
        ],
        out_specs=pl.BlockSpec((rt, D_MODEL), lambda i: (i, 0)),
        out_shape=jax.ShapeDtypeStruct((t, D_MODEL), F32),
        scratch_shapes=[pltpu.VMEM((2, sr * ROW_TILES, LANES), F32), pltpu.SemaphoreType.DMA((2,))],
        compiler_params=_cparams("arbitrary"),
        name="moe_combine_ln",
    )(seg, seg, ir, x1r, gates, p["ln2_g"], p["ln2_b"], eo)


def _moe(x1r, ir, gates, cnt, stat, p, t, tm, tmo, rt):
    n_assign = t * TOP_K
    n_blocks = (n_assign + N_EXPERTS * (tmo - 1) + tmo - 1) // tmo
    rows = n_blocks * tmo
    counts = cnt[0, :N_EXPERTS].astype(jnp.int32)
    padded = (counts + tmo - 1) // tmo * tmo
    pend = jnp.cumsum(padded)
    pstart = (pend - padded).astype(jnp.int32)
    pad_info = jnp.concatenate([pstart + counts, pend[-1:], padded - counts, rows - pend[-1:]]).astype(jnp.int32)
    block_start = jnp.arange(n_blocks, dtype=jnp.int32) * tmo
    block_e = jnp.minimum(jnp.sum((pend[None, :] <= block_start[:, None]).astype(jnp.int32), axis=1),
                          N_EXPERTS - 1)
    nvalid = (pend[-1:] // tmo).astype(jnp.int32)
    ir_flat = ir[:, :SUBLANES].reshape(t * SUBLANES)
    xs = _dispatch(x1r, ir_flat, pstart, pad_info, rows, t, tm)
    eo = _moe_experts(xs, block_e, nvalid, p, n_blocks, tmo)
    st3 = stat.reshape(t // rt, SUBLANES, LANES)[:, :3, :N_EXPERTS].astype(jnp.int32)
    seg = jnp.concatenate([st3[:, 0], st3[:, 1], st3[:, 2] + pstart[None, :], jnp.zeros_like(st3[:, 0])],
                          axis=1).reshape(-1)
    return _combine_seg(x1r, ir, gates, seg, eo, p, t, rt)


def _t5_bucket(rel):
    half = N_BUCKETS // 2
    max_exact = half // 2
    ret = jnp.where(rel > 0, half, 0)
    n = jnp.abs(rel)
    nf = jnp.maximum(n, 1).astype(F32)
    large = max_exact + (jnp.log(nf / max_exact) / math.log(MAX_DISTANCE / max_exact)
                         * (half - max_exact)).astype(jnp.int32)
    large = jnp.minimum(large, half - 1)
    return ret + jnp.where(n < max_exact, n, large)


def _rel_bias(table, cq):
    koff = jnp.arange(WINDOW + cq) - WINDOW
    rel = koff[None, :] - jnp.arange(cq)[:, None]
    onehot = jax.nn.one_hot(_t5_bucket(rel).reshape(-1), N_BUCKETS, dtype=F32)
    bias = jnp.dot(onehot, table.astype(F32), precision=lax.Precision.HIGHEST)
    return jnp.transpose(bias.reshape(cq, WINDOW + cq, ATT_HEADS), (2, 0, 1))


def _pad_rows_front(a, rows):
    pad = jnp.zeros(a.shape[:1] + (rows - a.shape[1],) + a.shape[2:], a.dtype)
    return jnp.concatenate([pad, a], axis=1)


def _layer_params(l, w_in, ssd_conv_w, ssd_conv_b, ssd_dt_bias, ssd_a_log, ssd_d, ssd_norm_w, w_ssd_out,
                  attn_sinks, w_attn_out, sc_conv_w, w_sc_out, w_o, ln1_g, ln1_b,
                  router_w, router_b, w_up, b_up, w_down, b_down, ln2_g, ln2_b):
    wi = w_in[l]
    w_dt = wi[:, OFF_DT:OFF_Q]
    return {
        "w_main": jnp.concatenate([wi[:, :OFF_DT], wi[:, OFF_Q:OFF_K], wi[:, OFF_SCB:]], axis=1).astype(BF16),
        "w_kv": wi[:, OFF_K:OFF_SCB].astype(BF16),
        "w_dt": jnp.pad(w_dt, ((0, 0), (0, LANES - SSD_N_HEADS))).astype(BF16),
        "w_dtT": w_dt.T[_HEADS_EVEN_ODD].astype(BF16),
        "conv_w": ssd_conv_w[l],
        "conv_b": ssd_conv_b[l][None, :],
        "dt_b128": jnp.pad(ssd_dt_bias[l], (0, LANES - SSD_N_HEADS))[None, :],
        "dt_bT": ssd_dt_bias[l][_HEADS_EVEN_ODD][:, None],
        "a_log128": jnp.pad(ssd_a_log[l], (0, LANES - SSD_N_HEADS))[None, :],
        "expand": jnp.asarray(_head_expand_matrix(), BF16),
        "a_logT": ssd_a_log[l][_HEADS_EVEN_ODD][:, None],
        "d_exp": jnp.repeat(ssd_d[l], SSD_HEAD_DIM)[None, :],
        "norm_w": ssd_norm_w[l][None, :],
        "w_ssd_out": w_ssd_out[l].astype(BF16),
        "sinks": attn_sinks[l],
        "w_att_out": w_attn_out[l].astype(BF16),
        "sc_w": sc_conv_w[l],
        "w_sc_out": w_sc_out[l].astype(BF16),
        "w_o": w_o[l].astype(BF16),
        "ln1_g": ln1_g[l][None, :],
        "ln1_b": ln1_b[l][None, :],
        "router_w": jnp.pad(router_w[l], ((0, 0), (0, LANES - N_EXPERTS))).astype(BF16),
        "router_b": jnp.pad(router_b[l], (0, LANES - N_EXPERTS))[None, :],
        "layer": l,
        "w_up": w_up,
        "b_up": b_up[:, :, None, :],
        "w_down": w_down,
        "b_down": b_down[:, :, None, :],
        "ln2_g": ln2_g[l][None, :],
        "ln2_b": ln2_b[l][None, :],
    }


def _trunk_layer(x2d, bt, s, p, bias, ssd_hist, ssd_state, sc_hist, kv_cache, cfg):
    t = bt * s
    h_main, h_kv = _in_proj(x2d, p["w_main"], p["w_kv"], cfg["tm_in"], cfg["tn_in"])
    k_new = h_kv[:, :ATT_KV_DIM].reshape(bt, s, ATT_KV_DIM)
    v_new = h_kv[:, ATT_KV_DIM:].reshape(bt, s, ATT_KV_DIM)

    if kv_cache is None:
        zpad = jnp.zeros((bt, WINDOW, ATT_KV_DIM), BF16)
        kall = jnp.concatenate([zpad, k_new.astype(BF16)], axis=1)
        vall = jnp.concatenate([zpad, v_new.astype(BF16)], axis=1)
    else:
        kall = jnp.concatenate([kv_cache[0].astype(BF16), k_new.astype(BF16)], axis=1)
        vall = jnp.concatenate([kv_cache[1].astype(BF16), v_new.astype(BF16)], axis=1)
    att = _attention(h_main, kall, vall, bias, p["sinks"], bt, s, cfg["cq"], cfg["n_sub"], kv_cache is None)

    if ssd_hist is None:
        hist8 = jnp.zeros((bt, SUBLANES, SSD_CONV_DIM), F32)
        st0 = jnp.zeros((bt, SSD_N_GROUPS, SSD_D_STATE, SSD_HEADS_PER_GROUP * SSD_HEAD_DIM), F32)
        schist8 = jnp.zeros((bt, SUBLANES, D_MODEL), F32)
    else:
        hist8 = _pad_rows_front(ssd_hist, SUBLANES)
        st0 = _state_to_kernel(ssd_state)
        schist8 = _pad_rows_front(sc_hist, SUBLANES)
    ssd_pre, st_out, hist_out = _ssd_grouped(x2d, h_main, hist8, st0, p, bt, s, cfg["cq"], cfg["ssd_cps"])

    x1r, ir, gates, cnt, schist_out, stat = _merge(x2d, h_main, ssd_pre, att, schist8, p, bt, s, cfg["tm_merge"])
    x2 = _moe(x1r, ir, gates, cnt, stat, p, t, cfg["tm_moe"], cfg["tmo"], min(cfg["tm_merge"], ROUTE_TILE))

    n_keep = min(s, WINDOW)
    new_k = k_new[:, s - n_keep:].reshape(bt, n_keep, ATT_KV_HEADS, ATT_HEAD_DIM)
    new_v = v_new[:, s - n_keep:].reshape(bt, n_keep, ATT_KV_HEADS, ATT_HEAD_DIM)
    new_state = _state_from_kernel(st_out)
    new_hist = hist_out[:, SUBLANES - (SSD_CONV - 1):]
    new_sc_hist = schist_out[:, SUBLANES - (SC_WIDTH - 1):]
    return x2, new_k, new_v, new_state, new_hist, new_sc_hist


def _config(bt, s, prompt):
    t = bt * s
    if prompt:
        return {"tm_in": min(t, 2048), "tn_in": 1024, "cq": CHUNK, "n_sub": min(8, s // CHUNK),
                "tm_merge": min(s, 512), "tm_moe": min(t, 256), "tmo": min(512, max(64, t // 32)),
                "ssd_cps": 4 if s % (4 * CHUNK) == 0 else 1}
    return {"tm_in": min(t, 256), "tn_in": 1024, "cq": s, "n_sub": 1,
            "tm_merge": s, "tm_moe": min(t, 256), "tmo": 64, "ssd_cps": 1}


def _forward(x_prompt, x_sample, cache_attn_k, cache_attn_v, state_ssd, state_ssd_conv, state_short_conv,
             rel_bias, layer_weights):
    bp, sp, _ = x_prompt.shape
    bd, sd, _ = x_sample.shape
    cfg_p = _config(bp, sp, True)
    cfg_d = _config(bd, sd, False)
    bias_p = _rel_bias(rel_bias, cfg_p["cq"])
    bias_d = _rel_bias(rel_bias, cfg_d["cq"])
    yp = x_prompt.reshape(bp * sp, D_MODEL)
    ys = x_sample.reshape(bd * sd, D_MODEL)
    outs_p, outs_d = [], []
    for l in range(DEPTH):
        p = _layer_params(l, *layer_weights)
        yp, *rest = _trunk_layer(yp, bp, sp, p, bias_p, None, None, None, None, cfg_p)
        outs_p.append(rest)
        cache = (cache_attn_k[l].reshape(bd, WINDOW, ATT_KV_DIM), cache_attn_v[l].reshape(bd, WINDOW, ATT_KV_DIM))
        ys, *rest = _trunk_layer(ys, bd, sd, p, bias_d, state_ssd_conv[l], state_ssd[l], state_short_conv[l],
                                 cache, cfg_d)
        outs_d.append(rest)
    stack = lambda outs, i: jnp.stack([o[i] for o in outs])
    return (yp.reshape(bp, sp, D_MODEL), ys.reshape(bd, sd, D_MODEL),
            stack(outs_p, 0), stack(outs_p, 1), stack(outs_p, 2), stack(outs_p, 3), stack(outs_p, 4),
            stack(outs_d, 0), stack(outs_d, 1), stack(outs_d, 2), stack(outs_d, 3), stack(outs_d, 4))


def kernel(x_prompt, x_sample, cache_attn_k, cache_attn_v, state_ssd, state_ssd_conv, state_short_conv, w_in, ssd_conv_w, ssd_conv_b, ssd_dt_bias, ssd_a_log, ssd_d, ssd_norm_w, w_ssd_out, attn_sinks, w_attn_out, rel_bias, sc_conv_w, w_sc_out, w_o, ln1_g, ln1_b, router_w, router_b, w_up, b_up, w_down, b_down, ln2_g, ln2_b):
    layer_weights = (w_in, ssd_conv_w, ssd_conv_b, ssd_dt_bias, ssd_a_log, ssd_d, ssd_norm_w, w_ssd_out,
                     attn_sinks, w_attn_out, sc_conv_w, w_sc_out, w_o, ln1_g, ln1_b,
                     router_w, router_b, w_up, b_up, w_down, b_down, ln2_g, ln2_b)
    return _forward(x_prompt, x_sample, cache_attn_k, cache_attn_v, state_ssd, state_ssd_conv, state_short_conv,
                    rel_bias, layer_weights)
```

```python
import functools
import math

import jax
import jax.numpy as jnp
import numpy as np
from jax import lax
from jax.experimental import pallas as pl
from jax.experimental.pallas import tpu as pltpu

F32 = jnp.float32
BF16 = jnp.bfloat16

D_MODEL = 1024
DEPTH = 2
CHUNK = 64
SSD_D_INNER = 2 * D_MODEL
SSD_HEAD_DIM = 64
SSD_N_HEADS = SSD_D_INNER // SSD_HEAD_DIM
SSD_N_GROUPS = 8
SSD_D_STATE = 128
SSD_CONV = 4
SSD_GN = SSD_N_GROUPS * SSD_D_STATE
SSD_CONV_DIM = SSD_D_INNER + 2 * SSD_GN
SSD_HEADS_PER_GROUP = SSD_N_HEADS // SSD_N_GROUPS
SSD_NORM_GROUP = SSD_D_INNER // SSD_N_GROUPS
ATT_HEAD_DIM = 64
ATT_HEADS = D_MODEL // ATT_HEAD_DIM
ATT_KV_HEADS = ATT_HEADS // 4
ATT_GROUP = ATT_HEADS // ATT_KV_HEADS
ATT_KV_DIM = ATT_KV_HEADS * ATT_HEAD_DIM
WINDOW = 128
N_BUCKETS = 32
MAX_DISTANCE = 128
SC_WIDTH = 3
N_EXPERTS = 32
TOP_K = 4
D_FF = D_MODEL
SWIGLU_LIMIT = 7.0
SWIGLU_ALPHA = 1.702
LN_EPS = 1e-5
ALPHA = (2.0 * DEPTH) ** 0.25

LANES = 128
SUBLANES = 8
ROW_TILES = D_MODEL // LANES
VMEM_LIMIT = 56 * 1024 * 1024
ZERO_ROWS = 16
ROUTE_TILE = 256
SEG_ALIGN = 8

COL_Z, COL_XS, COL_BC = 0, 1, 2
COL_Q, COL_SCB, COL_SCC, COL_SCH, COL_GSSD, COL_GATT, COL_GSC = 6, 7, 8, 9, 10, 11, 12
N_MAIN = 13 * D_MODEL
OFF_DT = SSD_D_INNER + SSD_CONV_DIM
OFF_Q = OFF_DT + SSD_N_HEADS
OFF_K = OFF_Q + D_MODEL
OFF_SCB = OFF_K + 2 * ATT_KV_DIM


def _sigmoid(v):
    return 0.5 * jnp.tanh(0.5 * v) + 0.5


def _cparams(*sem):
    return pltpu.CompilerParams(dimension_semantics=sem, vmem_limit_bytes=VMEM_LIMIT)


def _in_proj_kernel(x_ref, w_ref, wkv_ref, o_ref, kv_ref, xb_ref, *, n_main):
    j = pl.program_id(1)

    @pl.when(j == 0)
    def _():
        xb_ref[...] = x_ref[...].astype(BF16)

    @pl.when(j < n_main)
    def _():
        o_ref[...] = jnp.dot(xb_ref[...], w_ref[...], preferred_element_type=F32).astype(o_ref.dtype)

    @pl.when(j == n_main)
    def _():
        kv_ref[...] = jnp.dot(xb_ref[...], wkv_ref[...], preferred_element_type=F32)


def _in_proj(x, w, w_kv, tm, tn):
    t, k = x.shape
    n = w.shape[1]
    n_kv = w_kv.shape[1]
    n_main = n // tn
    main_col = lambda i, j: (i, jnp.minimum(j, n_main - 1))
    return pl.pallas_call(
        functools.partial(_in_proj_kernel, n_main=n_main),
        grid=(t // tm, n_main + 1),
        in_specs=[pl.BlockSpec((tm, k), lambda i, j: (i, 0)),
                  pl.BlockSpec((k, tn), lambda i, j: (0, jnp.minimum(j, n_main - 1))),
                  pl.BlockSpec((k, n_kv), lambda i, j: (0, 0))],
        out_specs=[pl.BlockSpec((tm, tn), main_col),
                   pl.BlockSpec((tm, n_kv), lambda i, j: (i, 0))],
        out_shape=[jax.ShapeDtypeStruct((t, n), BF16), jax.ShapeDtypeStruct((t, n_kv), F32)],
        scratch_shapes=[pltpu.VMEM((tm, k), BF16)],
        compiler_params=_cparams("arbitrary", "arbitrary"),
        name="in_proj",
    )(x, w, w_kv)


def _ssd_group_kernel(x_ref, z_ref, xs_ref, bc_ref, hist_ref, st0_ref,
                      wdt_ref, wdtT_ref, convw_ref, convb_ref, dtb_ref, dtbT_ref, alog_ref, alogT_ref,
                      dexp_ref, normw_ref, expand_ref,
                      y_ref, stout_ref, histout_ref,
                      ext_ref, ext_next_ref, xrow_ref, ynat_ref, st_refs, *, L, first, last):
    stride = L // SUBLANES
    gw = SSD_HEADS_PER_GROUP * SSD_HEAD_DIM

    n_cb = SSD_CONV_DIM // LANES
    lanes = lambda cb: slice(cb * LANES, (cb + 1) * LANES)

    def strided_rows(ref, start):
        return jnp.concatenate([ref[cb, pl.ds(start, SUBLANES, stride=stride), :] for cb in range(ref.shape[0])],
                               axis=1)

    if first is not None:
        @pl.when(first)
        def _():
            for cb in range(n_cb):
                ext_ref[cb, 0:SUBLANES, :] = hist_ref[0, :, lanes(cb)]
            for g in range(SSD_N_GROUPS):
                st_refs[g][...] = st0_ref[0, g]

    for cb in range(n_cb):
        src = xs_ref if cb < SSD_D_INNER // LANES else bc_ref
        ext_ref[cb, SUBLANES:SUBLANES + L, :] = src[:, lanes(cb % (SSD_D_INNER // LANES))].astype(F32)
    base = SUBLANES - (SSD_CONV - 1)
    taps = {o: strided_rows(ext_ref, o) for o in range(base, base + stride + SSD_CONV - 1)}
    pieces = []
    for r in range(stride):
        conv = convb_ref[...] + taps[base + r] * convw_ref[0:1, :]
        for j in range(1, SSD_CONV):
            conv = conv + taps[base + r + j] * convw_ref[j:j + 1, :]
        pieces.append(conv * _sigmoid(conv))
    xc = jnp.concatenate(pieces, axis=0)
    for cb in range(n_cb):
        tail = ext_ref[cb, L:L + SUBLANES, :]
        ext_next_ref[cb, 0:SUBLANES, :] = tail
        histout_ref[0, :, lanes(cb)] = tail
    xs_c = xc[:, :SSD_D_INNER]
    b_m = xc[:, SSD_D_INNER:SSD_D_INNER + SSD_GN]
    c_m = xc[:, SSD_D_INNER + SSD_GN:]

    for cb in range(D_MODEL // LANES):
        xrow_ref[cb] = x_ref[:, lanes(cb)]
    xb = jnp.concatenate([strided_rows(xrow_ref, r) for r in range(stride)], axis=0).astype(BF16)
    lane_h = lax.broadcasted_iota(jnp.int32, (1, LANES), 1) < SSD_N_HEADS
    dtr = jnp.dot(xb, wdt_ref[...], preferred_element_type=F32)
    dtrT = lax.dot_general(wdtT_ref[...], xb, (((1,), (1,)), ((), ())),
                           preferred_element_type=F32)

    def softplus(v):
        return jnp.maximum(v, 0.0) + jnp.log1p(jnp.exp(-jnp.abs(v)))

    dt = jnp.where(lane_h, softplus(dtr + dtb_ref[...]), 0.0)
    dtT = softplus(dtrT + dtbT_ref[...])
    adt = dt * (-jnp.exp(alog_ref[...]))
    adtT = dtT * (-jnp.exp(alogT_ref[...]))
    ri = lax.broadcasted_iota(jnp.int32, (L, L), 0)
    ci = lax.broadcasted_iota(jnp.int32, (L, L), 1)
    time_r = stride * (ri % SUBLANES) + ri // SUBLANES
    time_c = stride * (ci % SUBLANES) + ci // SUBLANES
    tri = time_r >= time_c
    acs = jnp.dot(tri.astype(F32), adt, precision=lax.Precision.HIGHEST,
                  preferred_element_type=F32)
    acsT = jnp.dot(adtT, (time_r <= time_c).astype(F32), precision=lax.Precision.HIGHEST,
                   preferred_element_type=F32)
    tot = jnp.sum(adt, axis=0, keepdims=True)
    paired = L == SSD_HEAD_DIM and 2 * L == LANES
    if paired:
        half = SSD_N_HEADS // 2
        acsT2 = jnp.concatenate([acsT[:half], acsT[half:]], axis=1)
        dtT2 = jnp.concatenate([dtT[:half], dtT[half:]], axis=1)
        rj = lax.broadcasted_iota(jnp.int32, (L, LANES), 0)
        cj = lax.broadcasted_iota(jnp.int32, (L, LANES), 1) % L
        tri2 = (stride * (rj % SUBLANES) + rj // SUBLANES) >= (stride * (cj % SUBLANES) + cj // SUBLANES)
        low_half = lax.broadcasted_iota(jnp.int32, (L, LANES), 1) < L

    cols = jnp.concatenate([acs, jnp.exp(acs), jnp.exp(tot - acs) * dt,
                            jnp.broadcast_to(jnp.exp(tot), (SUBLANES, LANES))], axis=0)
    cols = jnp.where(lane_h, cols, 0.0)
    hi = cols.astype(BF16).astype(F32)
    mid = (cols - hi).astype(BF16).astype(F32)
    lo = (cols - hi - mid).astype(BF16).astype(F32)
    packed = (hi + pltpu.roll(mid, SSD_N_HEADS, 1) + pltpu.roll(lo, 2 * SSD_N_HEADS, 1)).astype(BF16)
    wide = jnp.dot(packed, expand_ref[...], preferred_element_type=F32)
    acs_w = wide[0:L]
    eacs_w = wide[L:2 * L]
    w_w = wide[2 * L:3 * L]
    cd_w = wide[3 * L:3 * L + 1]

    for g in range(SSD_N_GROUPS):
        gs = slice(g * gw, (g + 1) * gw)
        bg = b_m[:, g * SSD_D_STATE:(g + 1) * SSD_D_STATE]
        bg_b = bg.astype(BF16)
        cg_b = c_m[:, g * SSD_D_STATE:(g + 1) * SSD_D_STATE].astype(BF16)
        if not paired:
            cb = lax.dot_general(cg_b, bg_b, (((1,), (1,)), ((), ())), preferred_element_type=F32)
        if L < LANES:
            bg = jnp.concatenate([bg, jnp.zeros((LANES - L, SSD_D_STATE), F32)], axis=0)
        bgT_b = bg.T[:, :L].astype(BF16)
        xs_g = xs_c[:, gs]
        st = st_refs[g][...]
        yo = jnp.dot(cg_b, st.astype(BF16), preferred_element_type=F32)
        stc = jnp.dot(bgT_b, (xs_g * w_w[:, gs]).astype(BF16), preferred_element_type=F32)
        st_refs[g][...] = st * cd_w[:, gs] + stc
        yds = []
        if paired:
            bg2_b = jnp.concatenate([bg_b, bg_b], axis=0)
            cb2 = lax.dot_general(cg_b, bg2_b, (((1,), (1,)), ((), ())), preferred_element_type=F32)
            for q in range(SSD_HEADS_PER_GROUP // 2):
                pair = g * (SSD_HEADS_PER_GROUP // 2) + q
                ls = slice(pair * LANES, (pair + 1) * LANES)
                decay = jnp.exp(jnp.where(tri2, acs_w[:, ls] - acsT2[pair:pair + 1, :], -jnp.inf))
                m = (cb2 * decay * dtT2[pair:pair + 1, :]).astype(BF16)
                xs2 = xs_c[:, ls]
                xs_bd = jnp.concatenate([jnp.where(low_half, xs2, 0.0), jnp.where(low_half, 0.0, xs2)],
                                        axis=0).astype(BF16)
                yds.append(jnp.dot(m, xs_bd, preferred_element_type=F32))
        else:
            for r in range(SSD_HEADS_PER_GROUP):
                h = g * SSD_HEADS_PER_GROUP + r
                hr = (h % 2) * (SSD_N_HEADS // 2) + h // 2
                col = acs_w[:, h * SSD_HEAD_DIM:h * SSD_HEAD_DIM + L]
                decay = jnp.exp(jnp.where(tri, col - acsT[hr:hr + 1, :], -jnp.inf))
                m = (cb * decay * dtT[hr:hr + 1, :]).astype(BF16)
                yds.append(jnp.dot(m, xs_g[:, r * SSD_HEAD_DIM:(r + 1) * SSD_HEAD_DIM].astype(BF16),
                                   preferred_element_type=F32))
        y_g = jnp.concatenate(yds, axis=1) + yo * eacs_w[:, gs] + dexp_ref[:, gs] * xs_g
        for r in range(stride):
            for cb in range(gw // LANES):
                ynat_ref[g * (gw // LANES) + cb, pl.ds(r, SUBLANES, stride=stride), :] = (
                    y_g[r * SUBLANES:(r + 1) * SUBLANES, lanes(cb)])

    z = z_ref[...].astype(F32)
    y = jnp.concatenate([ynat_ref[cb] for cb in range(SSD_D_INNER // LANES)], axis=1) * (z * _sigmoid(z))
    for g in range(SSD_N_GROUPS):
        sl = slice(g * SSD_NORM_GROUP, (g + 1) * SSD_NORM_GROUP)
        blk = y[:, sl]
        ms = jnp.mean(blk * blk, axis=-1, keepdims=True)
        y_ref[:, sl] = (blk * lax.rsqrt(ms + LN_EPS) * normw_ref[:, sl]).astype(y_ref.dtype)

    if last is not None:
        @pl.when(last)
        def _():
            for g in range(SSD_N_GROUPS):
                stout_ref[0, g] = st_refs[g][...]


def _ssd_step_kernel(x_ref, z_ref, xs_ref, bc_ref, hist_ref, st0_ref,
                     wdt_ref, wdtT_ref, convw_ref, convb_ref, dtb_ref, dtbT_ref, alog_ref, alogT_ref,
                     dexp_ref, normw_ref, expand_ref,
                     y_ref, stout_ref, histout_ref,
                     ext_ref, xrow_ref, ynat_ref, *st_refs, L, nc, cps):
    c = pl.program_id(1)
    for sub in range(cps):
        rows = pl.ds(sub * L, L)
        _ssd_group_kernel(
            x_ref.at[rows], z_ref.at[rows], xs_ref.at[rows], bc_ref.at[rows], hist_ref, st0_ref,
            wdt_ref, wdtT_ref, convw_ref, convb_ref, dtb_ref, dtbT_ref, alog_ref, alogT_ref,
            dexp_ref, normw_ref, expand_ref,
            y_ref.at[rows], stout_ref, histout_ref,
            ext_ref.at[sub], ext_ref.at[(sub + 1) % cps], xrow_ref.at[sub], ynat_ref.at[sub], st_refs,
            L=L, first=(c == 0) if sub == 0 else None, last=(c == nc - 1) if sub == cps - 1 else None)


def _ssd_grouped(x2d, h_main, hist8, st0, p, bt, s, L, cps):
    assert L % SUBLANES == 0 and L <= SSD_HEAD_DIM and s % (L * cps) == 0
    nc = s // (L * cps)
    L_blk = L * cps
    t = bt * s
    w2 = 2 * D_MODEL
    gw = SSD_HEADS_PER_GROUP * SSD_HEAD_DIM
    row = lambda b, c: (b * nc + c, 0)
    const2 = lambda b, c: (0, 0)
    state_spec = pl.BlockSpec((1, SSD_N_GROUPS, SSD_D_STATE, gw), lambda b, c: (b, 0, 0, 0))
    kern = functools.partial(_ssd_step_kernel, L=L, nc=nc, cps=cps)
    return pl.pallas_call(
        kern,
        grid=(bt, nc),
        in_specs=[
            pl.BlockSpec((L_blk, D_MODEL), row),
            pl.BlockSpec((L_blk, w2), lambda b, c: (b * nc + c, COL_Z)),
            pl.BlockSpec((L_blk, w2), lambda b, c: (b * nc + c, COL_XS)),
            pl.BlockSpec((L_blk, w2), lambda b, c: (b * nc + c, COL_BC)),
            pl.BlockSpec((1, SUBLANES, SSD_CONV_DIM), lambda b, c: (b, 0, 0)),
            state_spec,
            pl.BlockSpec((D_MODEL, LANES), const2),
            pl.BlockSpec((SSD_N_HEADS, D_MODEL), const2),
            pl.BlockSpec((SSD_CONV, SSD_CONV_DIM), const2),
            pl.BlockSpec((1, SSD_CONV_DIM), const2),
            pl.BlockSpec((1, LANES), const2),
            pl.BlockSpec((SSD_N_HEADS, 1), const2),
            pl.BlockSpec((1, LANES), const2),
            pl.BlockSpec((SSD_N_HEADS, 1), const2),
            pl.BlockSpec((1, SSD_D_INNER), const2),
            pl.BlockSpec((1, SSD_D_INNER), const2),
            pl.BlockSpec((LANES, SSD_D_INNER), const2),
        ],
        out_specs=[
            pl.BlockSpec((L_blk, SSD_D_INNER), row),
            state_spec,
            pl.BlockSpec((1, SUBLANES, SSD_CONV_DIM), lambda b, c: (b, 0, 0)),
        ],
        out_shape=[
            jax.ShapeDtypeStruct((t, SSD_D_INNER), BF16),
            jax.ShapeDtypeStruct((bt, SSD_N_GROUPS, SSD_D_STATE, gw), F32),
            jax.ShapeDtypeStruct((bt, SUBLANES, SSD_CONV_DIM), F32),
        ],
        scratch_shapes=[
            pltpu.VMEM((cps, SSD_CONV_DIM // LANES, L + SUBLANES, LANES), F32),
            pltpu.VMEM((cps, D_MODEL // LANES, L, LANES), F32),
            pltpu.VMEM((cps, SSD_D_INNER // LANES, L, LANES), F32),
        ] + [pltpu.VMEM((SSD_D_STATE, gw), F32) for _ in range(SSD_N_GROUPS)],
        compiler_params=_cparams("arbitrary", "arbitrary"),
        name="ssd_scan",
    )(x2d, h_main, h_main, h_main, hist8, st0,
      p["w_dt"], p["w_dtT"], p["conv_w"], p["conv_b"], p["dt_b128"], p["dt_bT"], p["a_log128"], p["a_logT"],
      p["d_exp"], p["norm_w"], p["expand"])


_HEADS_EVEN_ODD = np.concatenate([np.arange(0, SSD_N_HEADS, 2), np.arange(1, SSD_N_HEADS, 2)])


def _head_expand_matrix():
    e = np.zeros((LANES, SSD_D_INNER), np.float32)
    for part in range(3):
        for h in range(SSD_N_HEADS):
            e[part * SSD_N_HEADS + h, h * SSD_HEAD_DIM:(h + 1) * SSD_HEAD_DIM] = 1.0
    return e


def _state_to_kernel(state):
    b = state.shape[0]
    s5 = state.reshape(b, SSD_N_GROUPS, SSD_HEADS_PER_GROUP, SSD_HEAD_DIM, SSD_D_STATE)
    return jnp.transpose(s5, (0, 1, 4, 2, 3)).reshape(b, SSD_N_GROUPS, SSD_D_STATE,
                                                       SSD_HEADS_PER_GROUP * SSD_HEAD_DIM)


def _state_from_kernel(st):
    b = st.shape[0]
    s5 = st.reshape(b, SSD_N_GROUPS, SSD_D_STATE, SSD_HEADS_PER_GROUP, SSD_HEAD_DIM)
    return jnp.transpose(s5, (0, 1, 3, 4, 2)).reshape(b, SSD_N_HEADS, SSD_HEAD_DIM, SSD_D_STATE)


def _attn_kernel(q_ref, k_ref, v_ref, bias_ref, sink_ref, o_ref, *, cq, n_sub):
    step = pl.program_id(1)
    kw = WINDOW + cq
    scale = ATT_HEAD_DIM ** -0.5
    gw = ATT_GROUP * ATT_HEAD_DIM
    pairs = [(sub, hk) for sub in range(n_sub) for hk in range(ATT_KV_HEADS)]
    scs, vhs = {}, {}
    for sub in range(n_sub):
        r0 = pl.multiple_of((step * n_sub + sub) * cq, cq)
        qq = q_ref[sub * cq:(sub + 1) * cq, :]
        variant = jnp.minimum(step * n_sub + sub, bias_ref.shape[0] - 1)
        for hk in range(ATT_KV_HEADS):
            q4 = jnp.concatenate(
                [qq[:, hk * gw + g * ATT_HEAD_DIM:hk * gw + (g + 1) * ATT_HEAD_DIM] for g in range(ATT_GROUP)],
                axis=0)
            kh = k_ref[0, hk, pl.ds(r0, kw), :]
            sc = lax.dot_general(q4, kh, (((1,), (1,)), ((), ())), preferred_element_type=F32)
            scs[sub, hk] = sc * scale + bias_ref[variant, hk]
            vhs[sub, hk] = v_ref[0, hk, pl.ds(r0, kw), :]
    es, rs = {}, {}
    ones = jnp.ones((kw, LANES), BF16)
    for sub, hk in pairs:
        sink = sink_ref[hk]
        mx = jnp.maximum(jnp.max(scs[sub, hk], axis=-1, keepdims=True), sink)
        e = jnp.exp(scs[sub, hk] - mx).astype(BF16)
        den = jnp.dot(e, ones, preferred_element_type=F32)[:, :ATT_HEAD_DIM] + jnp.exp(sink - mx)
        es[sub, hk] = e
        rs[sub, hk] = 1.0 / den
    for sub, hk in pairs:
        o4 = jnp.dot(es[sub, hk], vhs[sub, hk], preferred_element_type=F32) * rs[sub, hk]
        o_ref[sub * cq:(sub + 1) * cq, hk * gw:(hk + 1) * gw] = jnp.concatenate(
            [o4[g * cq:(g + 1) * cq, :] for g in range(ATT_GROUP)], axis=1).astype(o_ref.dtype)


def _attention(h_main, kall, vall, bias, sinks, bt, s, cq, n_sub, mask_prefix):
    tq = cq * n_sub
    nq = s // tq
    kw = WINDOW + cq
    bias4 = bias.reshape(1, ATT_KV_HEADS, ATT_GROUP * cq, kw)
    if mask_prefix:
        key = jnp.arange(kw)
        bias4 = jnp.concatenate([jnp.where(c * cq + key >= WINDOW, bias4, -1e30) for c in range(WINDOW // cq)]
                                + [bias4], axis=0)
    n_var = bias4.shape[0]
    sink4 = jnp.repeat(sinks, cq).reshape(ATT_KV_HEADS, ATT_GROUP * cq, 1)
    per_head = lambda a: jnp.swapaxes(a.reshape(bt, WINDOW + s, ATT_KV_HEADS, ATT_HEAD_DIM), 1, 2)
    kern = functools.partial(_attn_kernel, cq=cq, n_sub=n_sub)
    return pl.pallas_call(
        kern,
        grid=(bt, nq),
        in_specs=[
            pl.BlockSpec((tq, D_MODEL), lambda b, i: (b * nq + i, COL_Q)),
            pl.BlockSpec((1, ATT_KV_HEADS, WINDOW + s, ATT_HEAD_DIM), lambda b, i: (b, 0, 0, 0)),
            pl.BlockSpec((1, ATT_KV_HEADS, WINDOW + s, ATT_HEAD_DIM), lambda b, i: (b, 0, 0, 0)),
            pl.BlockSpec((n_var, ATT_KV_HEADS, ATT_GROUP * cq, kw), lambda b, i: (0, 0, 0, 0)),
            pl.BlockSpec((ATT_KV_HEADS, ATT_GROUP * cq, 1), lambda b, i: (0, 0, 0)),
        ],
        out_specs=pl.BlockSpec((tq, D_MODEL), lambda b, i: (b * nq + i, 0)),
        out_shape=jax.ShapeDtypeStruct((bt * s, D_MODEL), BF16),
        compiler_params=_cparams("arbitrary", "arbitrary"),
        name="swa_attention",
    )(h_main, per_head(kall), per_head(vall), bias4, sink4)


def _merge_kernel(x_ref, ssd_ref, att_ref, scb_ref, scc_ref, sch_ref, gssd_ref, gatt_ref, gsc_ref, hist_ref,
                  wssd_ref, watt_ref, wsc_ref, wo_ref, scw_ref, lng_ref, lnb_ref, rw_ref, rb_ref,
                  x1r_ref, ir_ref, gate_ref, cnt_ref, histout_ref, stat_ref,
                  ext_ref, base_ref, *, tm, tiles_per_batch):
    i = pl.program_id(0)

    @pl.when(i == 0)
    def _():
        base_ref[...] = jnp.zeros_like(base_ref)

    @pl.when(i % tiles_per_batch == 0)
    def _():
        ext_ref[0:SUBLANES, :] = hist_ref[0]

    u = scc_ref[...].astype(F32) * sch_ref[...].astype(F32)
    ext_ref[SUBLANES:SUBLANES + tm, :] = u
    off = SUBLANES - (SC_WIDTH - 1)
    conv = ext_ref[off:off + tm, :] * scw_ref[0:1, :]
    for j in range(1, SC_WIDTH):
        conv = conv + ext_ref[off + j:off + j + tm, :] * scw_ref[j:j + 1, :]
    tail = ext_ref[tm:tm + SUBLANES, :]
    ext_ref[0:SUBLANES, :] = tail
    histout_ref[0] = tail
    sc_pre = (scb_ref[...].astype(F32) * conv).astype(BF16)

    y_sc = jnp.dot(sc_pre, wsc_ref[...], preferred_element_type=F32)
    y_ssd = jnp.dot(ssd_ref[...], wssd_ref[...], preferred_element_type=F32)
    y_att = jnp.dot(att_ref[...], watt_ref[...], preferred_element_type=F32)
    merged = (_sigmoid(gssd_ref[...].astype(F32)) * y_ssd
              + _sigmoid(gatt_ref[...].astype(F32)) * y_att
              + _sigmoid(gsc_ref[...].astype(F32)) * y_sc)
    v = ALPHA * x_ref[...] + jnp.dot(merged.astype(BF16), wo_ref[...], preferred_element_type=F32)
    mu = jnp.mean(v, axis=-1, keepdims=True)
    vc = v - mu
    var = jnp.mean(vc * vc, axis=-1, keepdims=True)
    x1 = vc * lax.rsqrt(var + LN_EPS) * lng_ref[...] + lnb_ref[...]
    for j in range(ROW_TILES):
        x1r_ref[pl.ds(j, tm, stride=ROW_TILES), :] = x1[:, j * LANES:(j + 1) * LANES]

    rt = min(tm, ROUTE_TILE)
    lane = lax.broadcasted_iota(jnp.int32, (rt, LANES), 1)
    lane1 = lax.broadcasted_iota(jnp.int32, (1, LANES), 1)
    row8 = lax.broadcasted_iota(jnp.int32, (SUBLANES, LANES), 0)
    ri = lax.broadcasted_iota(jnp.int32, (rt, rt), 0)
    ci = lax.broadcasted_iota(jnp.int32, (rt, rt), 1)
    ltri = (ri > ci).astype(BF16)
    gi = lax.broadcasted_iota(jnp.int32, (LANES, LANES), 0)
    gj = lax.broadcasted_iota(jnp.int32, (LANES, LANES), 1)
    earlier_expert = jnp.logical_and(gi < gj, gi // N_EXPERTS == gj // N_EXPERTS).astype(BF16)
    logits_all = jnp.dot(x1.astype(BF16), rw_ref[...], preferred_element_type=F32) + rb_ref[...]
    for sub in range(tm // rt):
        rows = slice(sub * rt, (sub + 1) * rt)
        work = jnp.where(lane < N_EXPERTS, logits_all[rows], -jnp.inf)
        vals, idxs = [], []
        onehot = jnp.zeros((rt, LANES), F32)
        for k in range(TOP_K):
            mv = jnp.max(work, axis=-1, keepdims=True)
            mi = jnp.min(jnp.where(work == mv, lane, LANES), axis=-1, keepdims=True)
            vals.append(mv)
            idxs.append(mi)
            work = jnp.where(lane == mi, -jnp.inf, work)
            onehot = onehot + (lane == mi + k * N_EXPERTS).astype(F32)
        es = [jnp.exp(vk - vals[0]) for vk in vals]
        den = es[0] + es[1] + es[2] + es[3]
        prefix = jnp.dot(ltri, onehot.astype(BF16), preferred_element_type=F32)
        cnt = jnp.sum(onehot, axis=0, keepdims=True)
        base = base_ref[0:1, :]
        within = jnp.zeros((1, LANES), F32)
        tot = cnt
        for sh in range(1, TOP_K):
            rolled = pltpu.roll(cnt, sh * N_EXPERTS, 1)
            within = within + jnp.where(lane1 >= sh * N_EXPERTS, rolled, 0.0)
            tot = tot + rolled
        chunks = jnp.floor((tot + (SEG_ALIGN - 1)) * (1.0 / SEG_ALIGN))
        seg_start = SEG_ALIGN * jnp.dot(jnp.broadcast_to(chunks, (SUBLANES, LANES)).astype(BF16), earlier_expert,
                                        preferred_element_type=F32)[0:1, :]
        rank_all = prefix + (within + base)
        loc_all = prefix + (within + seg_start)
        ir = jnp.zeros((rt, LANES), jnp.int32)
        gt = jnp.zeros((rt, LANES), F32)
        for k in range(TOP_K):
            sel = lane == idxs[k] + k * N_EXPERTS
            rk = jnp.sum(jnp.where(sel, rank_all, 0.0), axis=-1, keepdims=True).astype(jnp.int32)
            lc = jnp.sum(jnp.where(sel, loc_all, 0.0), axis=-1, keepdims=True).astype(jnp.int32)
            ir = jnp.where(lane == k, idxs[k], ir)
            ir = jnp.where(lane == TOP_K + k, rk, ir)
            ir = jnp.where(lane == 2 * TOP_K + k, lc, ir)
            gt = jnp.where(lane == k, es[k] / den, gt)
        ir_ref[rows, :] = ir
        gate_ref[rows, :] = gt
        stat_ref[sub * SUBLANES:(sub + 1) * SUBLANES, :] = jnp.where(
            row8 == 0, tot, jnp.where(row8 == 1, seg_start, jnp.where(row8 == 2, base, 0.0)))
        base_ref[...] = jnp.broadcast_to(base + tot, base_ref.shape)
    cnt_ref[...] = base_ref[...]


def _merge(x2d, h_main, ssd_pre, att, schist8, p, bt, s, tm):
    t = bt * s
    nt = t // tm
    tiles_per_batch = s // tm
    row = lambda i: (i, 0)
    const2 = lambda i: (0, 0)
    hcol = lambda cidx: pl.BlockSpec((tm, D_MODEL), lambda i: (i, cidx))
    kern = functools.partial(_merge_kernel, tm=tm, tiles_per_batch=tiles_per_batch)
    return pl.pallas_call(
        kern,
        grid=(nt,),
        in_specs=[
            pl.BlockSpec((tm, D_MODEL), row),
            pl.BlockSpec((tm, SSD_D_INNER), row),
            pl.BlockSpec((tm, D_MODEL), row),
            hcol(COL_SCB), hcol(COL_SCC), hcol(COL_SCH), hcol(COL_GSSD), hcol(COL_GATT), hcol(COL_GSC),
            pl.BlockSpec((1, SUBLANES, D_MODEL), lambda i: (i // tiles_per_batch, 0, 0)),
            pl.BlockSpec((SSD_D_INNER, D_MODEL), const2, pipeline_mode=pl.Buffered(1)),
            pl.BlockSpec((D_MODEL, D_MODEL), const2, pipeline_mode=pl.Buffered(1)),
            pl.BlockSpec((D_MODEL, D_MODEL), const2, pipeline_mode=pl.Buffered(1)),
            pl.BlockSpec((D_MODEL, D_MODEL), const2, pipeline_mode=pl.Buffered(1)),
            pl.BlockSpec((SC_WIDTH, D_MODEL), const2),
            pl.BlockSpec((1, D_MODEL), const2),
            pl.BlockSpec((1, D_MODEL), const2),
            pl.BlockSpec((D_MODEL, LANES), const2),
            pl.BlockSpec((1, LANES), const2),
        ],
        out_specs=[
            pl.BlockSpec((tm * ROW_TILES, LANES), row),
            pl.BlockSpec((tm, LANES), row),
            pl.BlockSpec((tm, LANES), row),
            pl.BlockSpec((SUBLANES, LANES), const2),
            pl.BlockSpec((1, SUBLANES, D_MODEL), lambda i: (i // tiles_per_batch, 0, 0)),
            pl.BlockSpec((tm // min(tm, ROUTE_TILE) * SUBLANES, LANES), row),
        ],
        out_shape=[
            jax.ShapeDtypeStruct((t * ROW_TILES, LANES), F32),
            jax.ShapeDtypeStruct((t, LANES), jnp.int32),
            jax.ShapeDtypeStruct((t, LANES), F32),
            jax.ShapeDtypeStruct((SUBLANES, LANES), F32),
            jax.ShapeDtypeStruct((bt, SUBLANES, D_MODEL), F32),
            jax.ShapeDtypeStruct((t // min(tm, ROUTE_TILE) * SUBLANES, LANES), F32),
        ],
        scratch_shapes=[
            pltpu.VMEM((tm + SUBLANES, D_MODEL), F32),
            pltpu.VMEM((SUBLANES, LANES), F32),
        ],
        compiler_params=_cparams("arbitrary"),
        name="merge_ln_router",
    )(x2d, ssd_pre, att, h_main, h_main, h_main, h_main, h_main, h_main, schist8,
      p["w_ssd_out"], p["w_att_out"], p["w_sc_out"], p["w_o"], p["sc_w"], p["ln1_g"], p["ln1_b"],
      p["router_w"], p["router_b"])


def _row_copy_wait(src_hbm, dst, sem, n_rows):
    pltpu.make_async_copy(src_hbm.at[pl.ds(0, n_rows * ROW_TILES), :],
                          dst.at[pl.ds(0, n_rows * ROW_TILES), :], sem).wait()


def _dispatch_kernel(pstart_ref, pad_ref, ir_ref, x_ref, xs_hbm, zero_ref, sem, zsem, *, tm):
    @pl.when(pl.program_id(0) == 0)
    def _():
        zero_ref[...] = jnp.zeros_like(zero_ref)

        def zero_copy(row, n_rows):
            dst = pl.multiple_of(row * ROW_TILES, ROW_TILES)
            return pltpu.make_async_copy(zero_ref.at[pl.ds(0, n_rows * ROW_TILES), :],
                                         xs_hbm.at[pl.ds(dst, n_rows * ROW_TILES), :], zsem)

        def per_expert(start):
            def body(e, carry):
                first = pad_ref[e]
                n = pad_ref[N_EXPERTS + 1 + e]
                n_big = n // ZERO_ROWS

                def big(j, c):
                    cp = zero_copy(first + j * ZERO_ROWS, ZERO_ROWS)
                    cp.start() if start else cp.wait()
                    return c

                def small(j, c):
                    cp = zero_copy(first + j, 1)
                    cp.start() if start else cp.wait()
                    return c

                lax.fori_loop(0, n_big, big, 0)
                lax.fori_loop(n_big * ZERO_ROWS, n, small, 0)
                return carry

            lax.fori_loop(0, N_EXPERTS + 1, body, 0)

        per_expert(True)
        per_expert(False)

    def issue(tk, carry):
        src = pl.multiple_of(tk * ROW_TILES, ROW_TILES)
        for k in range(TOP_K):
            dest = pstart_ref[ir_ref[tk * SUBLANES + k]] + ir_ref[tk * SUBLANES + TOP_K + k]
            dst = pl.multiple_of(dest * ROW_TILES, ROW_TILES)
            pltpu.make_async_copy(x_ref.at[pl.ds(src, ROW_TILES), :],
                                  xs_hbm.at[pl.ds(dst, ROW_TILES), :], sem).start(priority=k % 2)
        return carry

    lax.fori_loop(0, tm, issue, 0)
    for _ in range(TOP_K):
        _row_copy_wait(x_ref, xs_hbm, sem, tm)


def _dispatch(x1r, ir_flat, pstart, pad_info, rows, t, tm):
    return pl.pallas_call(
        functools.partial(_dispatch_kernel, tm=tm),
        grid_spec=pltpu.PrefetchScalarGridSpec(
            num_scalar_prefetch=2,
            grid=(t // tm,),
            in_specs=[
                pl.BlockSpec((tm * SUBLANES,), lambda i, ps, pd: (i,), memory_space=pltpu.SMEM),
                pl.BlockSpec((tm * ROW_TILES, LANES), lambda i, ps, pd: (i, 0)),
            ],
            out_specs=pl.BlockSpec(memory_space=pl.ANY),
            scratch_shapes=[pltpu.VMEM((ZERO_ROWS * ROW_TILES, LANES), F32),
                            pltpu.SemaphoreType.DMA, pltpu.SemaphoreType.DMA],
        ),
        out_shape=jax.ShapeDtypeStruct((rows * ROW_TILES, LANES), F32),
        compiler_params=_cparams("arbitrary"),
        name="moe_dispatch",
    )(pstart, pad_info, ir_flat, x1r)


def _moe_kernel(be_ref, nv_ref, x_ref, wu_ref, bu_ref, wd_ref, bd_ref, o_ref, wub_ref, wdb_ref, *, tmo):
    i = pl.program_id(0)
    valid = i < nv_ref[0]
    new_expert = jnp.logical_or(i == 0, be_ref[i] != be_ref[jnp.maximum(i - 1, 0)])

    @pl.when(jnp.logical_and(valid, new_expert))
    def _():
        wub_ref[...] = wu_ref[0, 0].astype(BF16)
        wdb_ref[...] = wd_ref[0, 0].astype(BF16)

    @pl.when(valid)
    def _():
        x = jnp.concatenate([x_ref[pl.ds(j, tmo, stride=ROW_TILES), :] for j in range(ROW_TILES)], axis=1)
        h = jnp.dot(x.astype(BF16), wub_ref[...], preferred_element_type=F32) + bu_ref[0, 0]
        gate = jnp.minimum(h[:, :D_FF], SWIGLU_LIMIT)
        up = jnp.clip(h[:, D_FF:], -SWIGLU_LIMIT, SWIGLU_LIMIT)
        act = (up + 1.0) * gate * _sigmoid(SWIGLU_ALPHA * gate)
        o = jnp.dot(act.astype(BF16), wdb_ref[...], preferred_element_type=F32) + bd_ref[0, 0]
        for j in range(ROW_TILES):
            o_ref[pl.ds(j, tmo, stride=ROW_TILES), :] = o[:, j * LANES:(j + 1) * LANES]

    @pl.when(jnp.logical_not(valid))
    def _():
        o_ref[...] = jnp.zeros_like(o_ref)


def _moe_experts(xs, block_e, nvalid, p, n_blocks, tmo):
    l = p["layer"]
    last_valid = lambda i, nv: jnp.minimum(i, nv[0] - 1)
    return pl.pallas_call(
        functools.partial(_moe_kernel, tmo=tmo),
        grid_spec=pltpu.PrefetchScalarGridSpec(
            num_scalar_prefetch=2,
            grid=(n_blocks,),
            in_specs=[
                pl.BlockSpec((tmo * ROW_TILES, LANES), lambda i, be, nv: (last_valid(i, nv), 0)),
                pl.BlockSpec((1, 1, D_MODEL, 2 * D_FF), lambda i, be, nv: (l, be[i], 0, 0)),
                pl.BlockSpec((1, 1, 1, 2 * D_FF), lambda i, be, nv: (l, be[i], 0, 0)),
                pl.BlockSpec((1, 1, D_FF, D_MODEL), lambda i, be, nv: (l, be[i], 0, 0)),
                pl.BlockSpec((1, 1, 1, D_MODEL), lambda i, be, nv: (l, be[i], 0, 0)),
            ],
            out_specs=pl.BlockSpec((tmo * ROW_TILES, LANES), lambda i, be, nv: (i, 0)),
            scratch_shapes=[pltpu.VMEM((D_MODEL, 2 * D_FF), BF16), pltpu.VMEM((D_FF, D_MODEL), BF16)],
        ),
        out_shape=jax.ShapeDtypeStruct(xs.shape, F32),
        compiler_params=_cparams("arbitrary"),
        name="moe_experts",
    )(block_e, nvalid, xs, p["w_up"], p["b_up"], p["w_down"], p["b_down"])


def _combine_seg_kernel(seg_ref, segn_ref, ir_ref, x1r_ref, gate_ref, lng_ref, lnb_ref, eo_hbm, o_ref,
                        stage_ref, sem, *, rt, nt):
    i = pl.program_id(0)
    slot = i % 2
    sr = stage_ref.shape[1] // ROW_TILES

    @pl.when(i == 0)
    def _():
        stage_ref[...] = jnp.zeros_like(stage_ref)

    def copies(s_ref, s, start):
        def expert_pair(e2, carry):
            for parity in range(2):
                e = 2 * e2 + parity
                n_copies = (s_ref[e] + (SEG_ALIGN - 1)) // SEG_ALIGN
                dst0 = s_ref[N_EXPERTS + e]
                src0 = s_ref[2 * N_EXPERTS + e]

                def one(j, c, dst0=dst0, src0=src0, parity=parity):
                    src = pl.multiple_of((src0 + j * SEG_ALIGN) * ROW_TILES, ROW_TILES)
                    dst = pl.multiple_of((dst0 + j * SEG_ALIGN) * ROW_TILES, SEG_ALIGN * ROW_TILES)
                    cp = pltpu.make_async_copy(eo_hbm.at[pl.ds(src, SEG_ALIGN * ROW_TILES), :],
                                               stage_ref.at[s, pl.ds(dst, SEG_ALIGN * ROW_TILES), :], sem.at[s])
                    cp.start(priority=parity) if start else cp.wait()
                    return c

                lax.fori_loop(0, n_copies, one, 0)
            return carry

        lax.fori_loop(0, N_EXPERTS // 2, expert_pair, 0)

    @pl.when(i == 0)
    def _():
        copies(seg_ref, 0, True)

    @pl.when(i + 1 < nt)
    def _():
        copies(segn_ref, 1 - slot, True)

    copies(seg_ref, slot, False)

    rows = jnp.concatenate([stage_ref[slot, pl.ds(j, sr, stride=ROW_TILES), :] for j in range(ROW_TILES)], axis=1)
    rows_b = rows.astype(BF16)
    pos = lax.broadcasted_iota(jnp.int32, (rt, sr), 1)
    ir = ir_ref[...]
    gates = gate_ref[...]
    weight = jnp.zeros((rt, sr), F32)
    for k in range(TOP_K):
        weight = jnp.where(pos == ir[:, 2 * TOP_K + k:2 * TOP_K + k + 1], gates[:, k:k + 1], weight)
    w_hi = weight.astype(BF16)
    w_lo = (weight - w_hi.astype(F32)).astype(BF16)
    moe = jnp.dot(w_hi, rows_b, preferred_element_type=F32) + jnp.dot(w_lo, rows_b, preferred_element_type=F32)
    x1 = jnp.concatenate([x1r_ref[pl.ds(j, rt, stride=ROW_TILES), :] for j in range(ROW_TILES)], axis=1)
    v = ALPHA * x1 + moe
    mu = jnp.mean(v, axis=-1, keepdims=True)
    vc = v - mu
    var = jnp.mean(vc * vc, axis=-1, keepdims=True)
    o_ref[...] = vc * lax.rsqrt(var + LN_EPS) * lng_ref[...] + lnb_ref[...]


def _combine_seg(x1r, ir, gates, seg, eo, p, t, rt):
    nt = t // rt
    sr = rt * TOP_K + N_EXPERTS * SEG_ALIGN
    seg_len = 4 * N_EXPERTS
    return pl.pallas_call(
        functools.partial(_combine_seg_kernel, rt=rt, nt=nt),
        grid=(nt,),
        in_specs=[
            pl.BlockSpec((seg_len,), lambda i: (i,), memory_space=pltpu.SMEM),
            pl.BlockSpec((seg_len,), lambda i: (jnp.minimum(i + 1, nt - 1),), memory_space=pltpu.SMEM),
            pl.BlockSpec((rt, LANES), lambda i: (i, 0)),
            pl.BlockSpec((rt * ROW_TILES, LANES), lambda i: (i, 0)),
            pl.BlockSpec((rt, LANES), lambda i: (i, 0)),
            pl.BlockSpec((1, D_MODEL), lambda i: (0, 0)),
            pl.BlockSpec((1, D_MODEL), lambda i: (0, 0)),
            pl.BlockSpec(memory_space=pl.ANY),
        ],
        out_specs=pl.BlockSpec((rt, D_MODEL), lambda i: (i, 0)),
        out_shape=jax.ShapeDtypeStruct((t, D_MODEL), F32),
        scratch_shapes=[pltpu.VMEM((2, sr * ROW_TILES, LANES), F32), pltpu.SemaphoreType.DMA((2,))],
        compiler_params=_cparams("arbitrary"),
        name="moe_combine_ln",
    )(seg, seg, ir, x1r, gates, p["ln2_g"], p["ln2_b"], eo)


def _moe(x1r, ir, gates, cnt, stat, p, t, tm, tmo, rt):
    n_assign = t * TOP_K
    n_blocks = (n_assign + N_EXPERTS * (tmo - 1) + tmo - 1) // tmo
    rows = n_blocks * tmo
    counts = cnt[0, :N_EXPERTS].astype(jnp.int32)
    padded = (counts + tmo - 1) // tmo * tmo
    pend = jnp.cumsum(padded)
    pstart = (pend - padded).astype(jnp.int32)
    pad_info = jnp.concatenate([pstart + counts, pend[-1:], padded - counts, rows - pend[-1:]]).astype(jnp.int32)
    block_start = jnp.arange(n_blocks, dtype=jnp.int32) * tmo
    block_e = jnp.minimum(jnp.sum((pend[None, :] <= block_start[:, None]).astype(jnp.int32), axis=1),
                          N_EXPERTS - 1)
    nvalid = (pend[-1:] // tmo).astype(jnp.int32)
    ir_flat = ir[:, :SUBLANES].reshape(t * SUBLANES)
    xs = _dispatch(x1r, ir_flat, pstart, pad_info, rows, t, tm)
    eo = _moe_experts(xs, block_e, nvalid, p, n_blocks, tmo)
    st3 = stat.reshape(t // rt, SUBLANES, LANES)[:, :3, :N_EXPERTS].astype(jnp.int32)
    seg = jnp.concatenate([st3[:, 0], st3[:, 1], st3[:, 2] + pstart[None, :], jnp.zeros_like(st3[:, 0])],
                          axis=1).reshape(-1)
    return _combine_seg(x1r, ir, gates, seg, eo, p, t, rt)


def _t5_bucket(rel):
    half = N_BUCKETS // 2
    max_exact = half // 2
    ret = jnp.where(rel > 0, half, 0)
    n = jnp.abs(rel)
    nf = jnp.maximum(n, 1).astype(F32)
    large = max_exact + (jnp.log(nf / max_exact) / math.log(MAX_DISTANCE / max_exact)
                         * (half - max_exact)).astype(jnp.int32)
    large = jnp.minimum(large, half - 1)
    return ret + jnp.where(n < max_exact, n, large)


def _rel_bias(table, cq):
    koff = jnp.arange(WINDOW + cq) - WINDOW
    rel = koff[None, :] - jnp.arange(cq)[:, None]
    onehot = jax.nn.one_hot(_t5_bucket(rel).reshape(-1), N_BUCKETS, dtype=F32)
    bias = jnp.dot(onehot, table.astype(F32), precision=lax.Precision.HIGHEST)
    return jnp.transpose(bias.reshape(cq, WINDOW + cq, ATT_HEADS), (2, 0, 1))


def _pad_rows_front(a, rows):
    pad = jnp.zeros(a.shape[:1] + (rows - a.shape[1],) + a.shape[2:], a.dtype)
    return jnp.concatenate([pad, a], axis=1)


def _layer_params(l, w_in, ssd_conv_w, ssd_conv_b, ssd_dt_bias, ssd_a_log, ssd_d, ssd_norm_w, w_ssd_out,
                  attn_sinks, w_attn_out, sc_conv_w, w_sc_out, w_o, ln1_g, ln1_b,
                  router_w, router_b, w_up, b_up, w_down, b_down, ln2_g, ln2_b):
    wi = w_in[l]
    w_dt = wi[:, OFF_DT:OFF_Q]
    return {
        "w_main": jnp.concatenate([wi[:, :OFF_DT], wi[:, OFF_Q:OFF_K], wi[:, OFF_SCB:]], axis=1).astype(BF16),
        "w_kv": wi[:, OFF_K:OFF_SCB].astype(BF16),
        "w_dt": jnp.pad(w_dt, ((0, 0), (0, LANES - SSD_N_HEADS))).astype(BF16),
        "w_dtT": w_dt.T[_HEADS_EVEN_ODD].astype(BF16),
        "conv_w": ssd_conv_w[l],
        "conv_b": ssd_conv_b[l][None, :],
        "dt_b128": jnp.pad(ssd_dt_bias[l], (0, LANES - SSD_N_HEADS))[None, :],
        "dt_bT": ssd_dt_bias[l][_HEADS_EVEN_ODD][:, None],
        "a_log128": jnp.pad(ssd_a_log[l], (0, LANES - SSD_N_HEADS))[None, :],
        "expand": jnp.asarray(_head_expand_matrix(), BF16),
        "a_logT": ssd_a_log[l][_HEADS_EVEN_ODD][:, None],
        "d_exp": jnp.repeat(ssd_d[l], SSD_HEAD_DIM)[None, :],
        "norm_w": ssd_norm_w[l][None, :],
        "w_ssd_out": w_ssd_out[l].astype(BF16),
        "sinks": attn_sinks[l],
        "w_att_out": w_attn_out[l].astype(BF16),
        "sc_w": sc_conv_w[l],
        "w_sc_out": w_sc_out[l].astype(BF16),
        "w_o": w_o[l].astype(BF16),
        "ln1_g": ln1_g[l][None, :],
        "ln1_b": ln1_b[l][None, :],
        "router_w": jnp.pad(router_w[l], ((0, 0), (0, LANES - N_EXPERTS))).astype(BF16),
        "router_b": jnp.pad(router_b[l], (0, LANES - N_EXPERTS))[None, :],
        "layer": l,
        "w_up": w_up,
        "b_up": b_up[:, :, None, :],
        "w_down": w_down,
        "b_down": b_down[:, :, None, :],
        "ln2_g": ln2_g[l][None, :],
        "ln2_b": ln2_b[l][None, :],
    }


def _trunk_layer(x2d, bt, s, p, bias, ssd_hist, ssd_state, sc_hist, kv_cache, cfg):
    t = bt * s
    h_main, h_kv = _in_proj(x2d, p["w_main"], p["w_kv"], cfg["tm_in"], cfg["tn_in"])
    k_new = h_kv[:, :ATT_KV_DIM].reshape(bt, s, ATT_KV_DIM)
    v_new = h_kv[:, ATT_KV_DIM:].reshape(bt, s, ATT_KV_DIM)

    if kv_cache is None:
        zpad = jnp.zeros((bt, WINDOW, ATT_KV_DIM), BF16)
        kall = jnp.concatenate([zpad, k_new.astype(BF16)], axis=1)
        vall = jnp.concatenate([zpad, v_new.astype(BF16)], axis=1)
    else:
        kall = jnp.concatenate([kv_cache[0].astype(BF16), k_new.astype(BF16)], axis=1)
        vall = jnp.concatenate([kv_cache[1].astype(BF16), v_new.astype(BF16)], axis=1)
    att = _attention(h_main, kall, vall, bias, p["sinks"], bt, s, cfg["cq"], cfg["n_sub"], kv_cache is None)

    if ssd_hist is None:
        hist8 = jnp.zeros((bt, SUBLANES, SSD_CONV_DIM), F32)
        st0 = jnp.zeros((bt, SSD_N_GROUPS, SSD_D_STATE, SSD_HEADS_PER_GROUP * SSD_HEAD_DIM), F32)
        schist8 = jnp.zeros((bt, SUBLANES, D_MODEL), F32)
    else:
        hist8 = _pad_rows_front(ssd_hist, SUBLANES)
        st0 = _state_to_kernel(ssd_state)
        schist8 = _pad_rows_front(sc_hist, SUBLANES)
    ssd_pre, st_out, hist_out = _ssd_grouped(x2d, h_main, hist8, st0, p, bt, s, cfg["cq"], cfg["ssd_cps"])

    x1r, ir, gates, cnt, schist_out, stat = _merge(x2d, h_main, ssd_pre, att, schist8, p, bt, s, cfg["tm_merge"])
    x2 = _moe(x1r, ir, gates, cnt, stat, p, t, cfg["tm_moe"], cfg["tmo"], min(cfg["tm_merge"], ROUTE_TILE))

    n_keep = min(s, WINDOW)
    new_k = k_new[:, s - n_keep:].reshape(bt, n_keep, ATT_KV_HEADS, ATT_HEAD_DIM)
    new_v = v_new[:, s - n_keep:].reshape(bt, n_keep, ATT_KV_HEADS, ATT_HEAD_DIM)
    new_state = _state_from_kernel(st_out)
    new_hist = hist_out[:, SUBLANES - (SSD_CONV - 1):]
    new_sc_hist = schist_out[:, SUBLANES - (SC_WIDTH - 1):]
    return x2, new_k, new_v, new_state, new_hist, new_sc_hist


def _config(bt, s, prompt):
    t = bt * s
    if prompt:
        return {"tm_in": min(t, 2048), "tn_in": 1024, "cq": CHUNK, "n_sub": min(8, s // CHUNK),
                "tm_merge": min(s, 512), "tm_moe": min(t, 256), "tmo": min(512, max(64, t // 32)),
                "ssd_cps": 4 if s % (4 * CHUNK) == 0 else 1}
    return {"tm_in": min(t, 256), "tn_in": 1024, "cq": s, "n_sub": 1,
            "tm_merge": s, "tm_moe": min(t, 256), "tmo": 64, "ssd_cps": 1}


def _forward(x_prompt, x_sample, cache_attn_k, cache_attn_v, state_ssd, state_ssd_conv, state_short_conv,
             rel_bias, layer_weights):
    bp, sp, _ = x_prompt.shape
    bd, sd, _ = x_sample.shape
    cfg_p = _config(bp, sp, True)
    cfg_d = _config(bd, sd, False)
    bias_p = _rel_bias(rel_bias, cfg_p["cq"])
    bias_d = _rel_bias(rel_bias, cfg_d["cq"])
    yp = x_prompt.reshape(bp * sp, D_MODEL)
    ys = x_sample.reshape(bd * sd, D_MODEL)
    outs_p, outs_d = [], []
    for l in range(DEPTH):
        p = _layer_params(l, *layer_weights)
        yp, *rest = _trunk_layer(yp, bp, sp, p, bias_p, None, None, None, None, cfg_p)
        outs_p.append(rest)
        cache = (cache_attn_k[l].reshape(bd, WINDOW, ATT_KV_DIM), cache_attn_v[l].reshape(bd, WINDOW, ATT_KV_DIM))
        ys, *rest = _trunk_layer(ys, bd, sd, p, bias_d, state_ssd_conv[l], state_ssd[l], state_short_conv[l],
                                 cache, cfg_d)
        outs_d.append(rest)
    stack = lambda outs, i: jnp.stack([o[i] for o in outs])
    return (yp.reshape(bp, sp, D_MODEL), ys.reshape(bd, sd, D_MODEL),
            stack(outs_p, 0), stack(outs_p, 1), stack(outs_p, 2), stack(outs_p, 3), stack(outs_p, 4),
            stack(outs_d, 0), stack(outs_d, 1), stack(outs_d, 2), stack(outs_d, 3), stack(outs_d, 4))


def kernel(x_prompt, x_sample, cache_attn_k, cache_attn_v, state_ssd, state_ssd_conv, state_short_conv, w_in, ssd_conv_w, ssd_conv_b, ssd_dt_bias, ssd_a_log, ssd_d, ssd_norm_w, w_ssd_out, attn_sinks, w_attn_out, rel_bias, sc_conv_w, w_sc_out, w_o, ln1_g, ln1_b, router_w, router_b, w_up, b_up, w_down, b_down, ln2_g, ln2_b):
    layer_weights = (w_in, ssd_conv_w, ssd_conv_b, ssd_dt_bias, ssd_a_log, ssd_d, ssd_norm_w, w_ssd_out,
                     attn_sinks, w_attn_out, sc_conv_w, w_sc_out, w_o, ln1_g, ln1_b,
                     router_w, router_b, w_up, b_up, w_down, b_down, ln2_g, ln2_b)
    return _forward(x_prompt, x_sample, cache_attn_k, cache_attn_v, state_ssd, state_ssd_conv, state_short_conv,
                    rel_bias, layer_weights)
```
